```python
import math
import jax
import jax.numpy as jnp
from jax import lax
import numpy as np

D_MODEL = 1024
BATCH = 32
SEQ = 2048
DEPTH = 4

GRID_W = 64
CTX_LEN = 256
N_EVEN = (DEPTH + 1) // 2
N_ODD = DEPTH // 2
N_MOD = 6

A_HEAD_DIM = 64
A_WIDTH = D_MODEL // 2
A_HEADS = A_WIDTH // A_HEAD_DIM
DECAY_RANK = 64
ICLR_RANK = 64
GATE_RANK = 128
DECAY_SCALE = math.exp(-0.5)
GN_EPS = 64e-5
A_PROJ = 3 * A_WIDTH + 2 * DECAY_RANK + 2 * ICLR_RANK + GATE_RANK
A_SPLITS = (A_WIDTH, 2 * A_WIDTH, 3 * A_WIDTH,
            3 * A_WIDTH + DECAY_RANK, 3 * A_WIDTH + 2 * DECAY_RANK,
            3 * A_WIDTH + 2 * DECAY_RANK + ICLR_RANK, 3 * A_WIDTH + 2 * DECAY_RANK + 2 * ICLR_RANK)

B_WIDTH = D_MODEL // 2
B_BLOCK_DIM = 64
B_BLOCKS = B_WIDTH // B_BLOCK_DIM
CONV_W = 4
LRU_C = 8.0
EVEN_PROJ = A_PROJ + 2 * B_WIDTH
EVEN_MIX = A_WIDTH + B_WIDTH

C_QK_DIM = 64
C_V_DIM = 2 * C_QK_DIM
C_HEADS = D_MODEL // C_V_DIM
C_WIDTH = C_HEADS * C_V_DIM
Q_BLOCK = 128
ROPE_BASE = 10000.0

N_KEYS = 128
N_EXPERTS = N_KEYS * N_KEYS
PEER_HEADS = 8
PEER_QDIM = 256
PEER_TOPK = 16
PEER_BLOCK = 128

kernel_name = 'hybrid_rwkv7_rglru_diffattn_peer_dit'


def rms(x, eps=1e-6):
    xf = x.astype(jnp.float32)
    return (xf * lax.rsqrt(jnp.mean(xf * xf, axis=-1, keepdims=True) + eps)).astype(x.dtype)


def modulate(x, shift, scale):
    return rms(x) * (1.0 + scale) + shift


def per_segment(f, z, n_ctx):
    return jnp.concatenate([f(z[:, :n_ctx]), f(z[:, n_ctx:])], axis=1)


def centred_shift(p):
    pp = jnp.pad(p, ((0, 0), (1, 1), (0, 0)))
    return 0.5 * (pp[:, :-2] + pp[:, 2:])


def depthwise_conv(s, w, b):
    t_len = s.shape[1]
    left = CONV_W // 2
    sp = jnp.pad(s, ((0, 0), (left, CONV_W - 1 - left), (0, 0)))
    out = sp[:, 0:t_len] * w[0]
    for j in range(1, CONV_W):
        out = out + sp[:, j:j + t_len] * w[j]
    return out + b


def rope_1d(x, pos):
    half = x.shape[-1] // 2
    inv_freq = ROPE_BASE ** (-jnp.arange(half, dtype=jnp.float32) / half)
    ang = pos.astype(jnp.float32)[:, None] * inv_freq
    cos = jnp.cos(ang)[:, None, None, :].astype(x.dtype)
    sin = jnp.sin(ang)[:, None, None, :].astype(x.dtype)
    x1, x2 = x[..., :half], x[..., half:]
    return jnp.concatenate([x1 * cos - x2 * sin, x1 * sin + x2 * cos], axis=-1)


def rope_2d(x, row_pos, col_pos):
    r = x.shape[-1] // 2
    return jnp.concatenate([rope_1d(x[..., :r], row_pos), rope_1d(x[..., r:], col_pos)], axis=-1)


def rwkv7_scan(r, w, k, v, kk, akk, n_ctx, reverse):
    seqs = (r, w, k, v, kk, akk)
    if reverse:
        seqs = tuple(jnp.roll(s, -n_ctx, axis=1) for s in seqs)
    xs = tuple(jnp.moveaxis(s, 1, 0) for s in seqs)
    bsz, _, h, n = r.shape

    def step(state, inp):
        r_t, w_t, k_t, v_t, kk_t, akk_t = inp
        sa = jnp.einsum('bhvk,bhk->bhv', state, kk_t)
        state = (state * w_t[:, :, None, :] - sa[..., None] * akk_t[:, :, None, :]
                 + v_t[..., None] * k_t[:, :, None, :])
        return state, jnp.einsum('bhvk,bhk->bhv', state, r_t)

    _, out = lax.scan(step, jnp.zeros((bsz, h, n, n), jnp.float32), xs, reverse=reverse)
    out = jnp.moveaxis(out, 0, 1)
    if reverse:
        out = jnp.roll(out, n_ctx, axis=1)
    return out


def diag_scan(a, u, n_ctx, reverse):
    if reverse:
        a = jnp.roll(a, -n_ctx, axis=1)
        u = jnp.roll(u, -n_ctx, axis=1)

    def combine(c1, c2):
        a1, u1 = c1
        a2, u2 = c2
        return a1 * a2, a2 * u1 + u2

    _, h = lax.associative_scan(combine, (a, u), axis=1, reverse=reverse)
    if reverse:
        h = jnp.roll(h, n_ctx, axis=1)
    return h


def rwkv7_mix(pa, n_ctx, w0, w_up, a0, a_up, g_up, k_k, k_a, r_k, ln_w, ln_b):
    bsz, seq_len, _ = pa.shape
    r, k, v, wd_f, wd_b, ad_f, ad_b, gd = jnp.split(pa, A_SPLITS, axis=-1)

    def heads(t):
        return t.reshape(bsz, seq_len, A_HEADS, A_HEAD_DIM)

    r32 = heads(r.astype(jnp.float32))
    v32 = heads(v.astype(jnp.float32))
    kk = heads((k * k_k).astype(jnp.float32))
    kk = kk / jnp.maximum(jnp.sqrt(jnp.sum(kk * kk, axis=-1, keepdims=True)), 1e-12)

    def direction(d, wd, ad):
        decay = jnp.exp(-DECAY_SCALE * jax.nn.sigmoid((w0[d] + jnp.tanh(wd) @ w_up[d]).astype(jnp.float32)))
        a = jax.nn.sigmoid((a0[d] + ad @ a_up[d]).astype(jnp.float32))
        kt = k.astype(jnp.float32) * (1.0 + (a - 1.0) * k_a)
        o = rwkv7_scan(r32, heads(decay), heads(kt), v32, kk, heads(a) * kk, n_ctx, reverse=(d == 1))
        return o, kt

    o_f, kt_f = direction(0, wd_f, ad_f)
    o_b, kt_b = direction(1, wd_b, ad_b)
    o = o_f + o_b
    mean = jnp.mean(o, axis=-1, keepdims=True)
    var = jnp.mean(jnp.square(o - mean), axis=-1, keepdims=True)
    on = ((o - mean) * lax.rsqrt(var + GN_EPS)).reshape(bsz, seq_len, A_WIDTH) * ln_w + ln_b
    bonus = jnp.sum(r32 * heads(0.5 * (kt_f + kt_b)) * r_k, axis=-1, keepdims=True) * v32
    g = jax.nn.sigmoid(gd) @ g_up
    return (on + bonus.reshape(bsz, seq_len, A_WIDTH)).astype(pa.dtype) * g


def rglru_mix(pb, n_ctx, conv_w, conv_b, wa, ba, wx, bx, lam):
    bsz, seq_len, _ = pb.shape
    xb = per_segment(lambda s: depthwise_conv(s, conv_w, conv_b), pb[..., :B_WIDTH], n_ctx).astype(jnp.float32)
    gate = jax.nn.gelu(pb[..., B_WIDTH:])
    xblk = xb.reshape(bsz, seq_len, B_BLOCKS, B_BLOCK_DIM)

    def direction(d):
        rg = jax.nn.sigmoid(jnp.einsum('blnc,ncd->blnd', xblk, wa[d].astype(jnp.float32)).reshape(bsz, seq_len, B_WIDTH) + ba[d])
        ig = jax.nn.sigmoid(jnp.einsum('blnc,ncd->blnd', xblk, wx[d].astype(jnp.float32)).reshape(bsz, seq_len, B_WIDTH) + bx[d])
        log_a = -LRU_C * rg * jax.nn.softplus(-lam[d].astype(jnp.float32))
        u = jnp.sqrt(-jnp.expm1(2.0 * log_a)) * ig * xb
        return diag_scan(jnp.exp(log_a), u, n_ctx, reverse=(d == 1))

    h = direction(0) + direction(1)
    return h.astype(pb.dtype) * gate


def diff_attention(z, n_ctx, row_pos, col_pos, lam_init, need_ctx, w_qkv, q_g, k_g, lq1, lk1, lq2, lk2, sub_g):
    bsz, seq_len, _ = z.shape
    n_lat = seq_len - n_ctx
    qkv = z @ w_qkv
    q = qkv[..., :C_WIDTH].reshape(bsz, seq_len, C_HEADS, 2, C_QK_DIM)
    k = qkv[..., C_WIDTH:2 * C_WIDTH].reshape(bsz, seq_len, C_HEADS, 2, C_QK_DIM)
    v = qkv[..., 2 * C_WIDTH:].reshape(bsz, seq_len, C_HEADS, C_V_DIM)
    q = rms(q) * q_g
    k = rms(k) * k_g
    q_lat = rope_2d(q[:, n_ctx:], row_pos, col_pos)
    k = jnp.concatenate([k[:, :n_ctx], rope_2d(k[:, n_ctx:], row_pos, col_pos)], axis=1)
    lam = (jnp.exp(jnp.sum((lq1 * lk1).astype(jnp.float32))) - jnp.exp(jnp.sum((lq2 * lk2).astype(jnp.float32)))
           + lam_init)
    scale = C_QK_DIM ** -0.5

    def attend(qb, kb, vb):
        s = jnp.einsum('bqhmd,bkhmd->bmhqk', qb, kb).astype(jnp.float32) * scale
        p = jax.nn.softmax(s, axis=-1)
        a = (p[:, 0] - lam * p[:, 1]).astype(vb.dtype)
        o = jnp.einsum('bhqk,bkhe->bqhe', a, vb)
        return rms(o) * sub_g * (1.0 - lam_init)

    n_blocks = n_lat // Q_BLOCK
    q_blocks = jnp.moveaxis(q_lat.reshape(bsz, n_blocks, Q_BLOCK, C_HEADS, 2, C_QK_DIM), 1, 0)
    o_lat = lax.map(lambda qb: attend(qb, k, v), q_blocks)
    o_lat = jnp.moveaxis(o_lat, 0, 1).reshape(bsz, n_lat, C_WIDTH)
    o_ctx = None
    if need_ctx:
        o_ctx = attend(q[:, :n_ctx], k[:, :n_ctx], v[:, :n_ctx]).reshape(bsz, n_ctx, C_WIDTH)
    return o_ctx, o_lat


def peer_ffn(h, w_q, sub_keys, u_tab, v_tab):
    bsz, seq_len, dm = h.shape
    tok = h.reshape(-1, PEER_BLOCK, dm)

    def block(xb):
        p = xb.shape[0]
        q = (xb @ w_q).reshape(p, PEER_HEADS, 2, PEER_QDIM // 2)
        s = jnp.einsum('phmd,mnd->phmn', q, sub_keys).astype(jnp.float32)
        sv, si = lax.top_k(s, PEER_TOPK)
        cand = sv[:, :, 0, :, None] + sv[:, :, 1, None, :]
        cidx = si[:, :, 0, :, None] * N_KEYS + si[:, :, 1, None, :]
        best, pos = lax.top_k(cand.reshape(p, PEER_HEADS, PEER_TOPK * PEER_TOPK), PEER_TOPK)
        eidx = jnp.take_along_axis(cidx.reshape(p, PEER_HEADS, PEER_TOPK * PEER_TOPK), pos, axis=-1)
        g = jax.nn.softmax(best, axis=-1)
        act = jax.nn.gelu(jnp.einsum('phkd,pd->phk', u_tab[eidx], xb))
        return jnp.einsum('phk,phkd->pd', (g * act).astype(xb.dtype), v_tab[eidx])

    out = lax.map(block, tok)
    return out.reshape(bsz, seq_len, dm)


def setup_inputs(seed: int = 0) -> dict:
    key = jax.random.key(seed)
    ks = iter(jax.random.split(key, 48))
    D = D_MODEL

    def nrm(shape, s):
        return jax.random.normal(next(ks), shape, jnp.float32) * s

    x = nrm((BATCH, SEQ, D), 1.0)
    c = nrm((BATCH, D), 1.0)
    ctx = nrm((BATCH, CTX_LEN, D), 1.0)
    c_ctx = nrm((D,), 1.0)
    w_mod = nrm((DEPTH, D, N_MOD * D), 0.5 * D ** -0.5)
    b_mod = nrm((DEPTH, N_MOD * D), 0.02)
    even_w_in = nrm((N_EVEN, D, EVEN_PROJ), D ** -0.5)
    even_mu = jax.random.uniform(next(ks), (N_EVEN, A_PROJ), jnp.float32)
    even_w_out = nrm((N_EVEN, EVEN_MIX, D), EVEN_MIX ** -0.5)
    decay_base = jnp.tile(jnp.linspace(-6.5, -1.5, A_HEAD_DIM, dtype=jnp.float32), A_HEADS)
    rwkv_w0 = decay_base + nrm((N_EVEN, 2, A_WIDTH), 0.1)
    rwkv_w_up = nrm((N_EVEN, 2, DECAY_RANK, A_WIDTH), DECAY_RANK ** -0.5)
    rwkv_a0 = nrm((N_EVEN, 2, A_WIDTH), 0.1)
    rwkv_a_up = nrm((N_EVEN, 2, ICLR_RANK, A_WIDTH), ICLR_RANK ** -0.5)
    rwkv_g_up = nrm((N_EVEN, GATE_RANK, A_WIDTH), GATE_RANK ** -0.5)
    rwkv_k_k = 0.85 + nrm((N_EVEN, A_WIDTH), 0.05)
    rwkv_k_a = 1.0 + nrm((N_EVEN, A_WIDTH), 0.05)
    rwkv_r_k = nrm((N_EVEN, A_HEADS, A_HEAD_DIM), 0.1)
    rwkv_ln_w = 1.0 + nrm((N_EVEN, A_WIDTH), 0.05)
    rwkv_ln_b = nrm((N_EVEN, A_WIDTH), 0.02)
    lru_conv_w = nrm((N_EVEN, CONV_W, B_WIDTH), CONV_W ** -0.5)
    lru_conv_b = nrm((N_EVEN, B_WIDTH), 0.02)
    lru_wa = nrm((N_EVEN, 2, B_BLOCKS, B_BLOCK_DIM, B_BLOCK_DIM), B_BLOCK_DIM ** -0.5)
    lru_ba = nrm((N_EVEN, 2, B_WIDTH), 0.02)
    lru_wx = nrm((N_EVEN, 2, B_BLOCKS, B_BLOCK_DIM, B_BLOCK_DIM), B_BLOCK_DIM ** -0.5)
    lru_bx = nrm((N_EVEN, 2, B_WIDTH), 0.02)
    a_init = jax.random.uniform(next(ks), (N_EVEN, 2, B_WIDTH), jnp.float32, 0.9, 0.999)
    root = a_init ** (1.0 / LRU_C)
    lru_lam = jnp.log(root) - jnp.log1p(-root)
    attn_w_qkv = nrm((N_ODD, D, 3 * C_WIDTH), D ** -0.5)
    attn_q_g = 1.0 + nrm((N_ODD, C_QK_DIM), 0.05)
    attn_k_g = 1.0 + nrm((N_ODD, C_QK_DIM), 0.05)
    attn_lam_q1 = nrm((N_ODD, C_QK_DIM), 0.1)
    attn_lam_k1 = nrm((N_ODD, C_QK_DIM), 0.1)
    attn_lam_q2 = nrm((N_ODD, C_QK_DIM), 0.1)
    attn_lam_k2 = nrm((N_ODD, C_QK_DIM), 0.1)
    attn_sub_g = 1.0 + nrm((N_ODD, C_V_DIM), 0.05)
    attn_w_o = nrm((N_ODD, C_WIDTH, D), C_WIDTH ** -0.5)
    peer_w_q = nrm((DEPTH, D, PEER_HEADS * PEER_QDIM), D ** -0.5)
    peer_keys = nrm((DEPTH, 2, N_KEYS, PEER_QDIM // 2), (PEER_QDIM // 2) ** -0.5)
    peer_u = nrm((DEPTH, N_EXPERTS, D), D ** -0.5)
    peer_v = nrm((DEPTH, N_EXPERTS, D), PEER_HEADS ** -0.5)
    return {'x': x, 'c': c, 'ctx': ctx, 'c_ctx': c_ctx, 'w_mod': w_mod, 'b_mod': b_mod,
            'even_w_in': even_w_in, 'even_mu': even_mu, 'even_w_out': even_w_out,
            'rwkv_w0': rwkv_w0, 'rwkv_w_up': rwkv_w_up, 'rwkv_a0': rwkv_a0, 'rwkv_a_up': rwkv_a_up,
            'rwkv_g_up': rwkv_g_up, 'rwkv_k_k': rwkv_k_k, 'rwkv_k_a': rwkv_k_a, 'rwkv_r_k': rwkv_r_k,
            'rwkv_ln_w': rwkv_ln_w, 'rwkv_ln_b': rwkv_ln_b,
            'lru_conv_w': lru_conv_w, 'lru_conv_b': lru_conv_b, 'lru_wa': lru_wa, 'lru_ba': lru_ba,
            'lru_wx': lru_wx, 'lru_bx': lru_bx, 'lru_lam': lru_lam,
            'attn_w_qkv': attn_w_qkv, 'attn_q_g': attn_q_g, 'attn_k_g': attn_k_g,
            'attn_lam_q1': attn_lam_q1, 'attn_lam_k1': attn_lam_k1, 'attn_lam_q2': attn_lam_q2,
            'attn_lam_k2': attn_lam_k2, 'attn_sub_g': attn_sub_g, 'attn_w_o': attn_w_o,
            'peer_w_q': peer_w_q, 'peer_keys': peer_keys, 'peer_u': peer_u, 'peer_v': peer_v}


def reference(x, c, ctx, c_ctx, w_mod, b_mod, even_w_in, even_mu, even_w_out,
              rwkv_w0, rwkv_w_up, rwkv_a0, rwkv_a_up, rwkv_g_up, rwkv_k_k, rwkv_k_a, rwkv_r_k,
              rwkv_ln_w, rwkv_ln_b,
              lru_conv_w, lru_conv_b, lru_wa, lru_ba, lru_wx, lru_bx, lru_lam,
              attn_w_qkv, attn_q_g, attn_k_g, attn_lam_q1, attn_lam_k1, attn_lam_q2, attn_lam_k2,
              attn_sub_g, attn_w_o,
              peer_w_q, peer_keys, peer_u, peer_v):
    n_ctx = ctx.shape[1]
    n_lat = x.shape[1]
    n_rows = n_lat // GRID_W
    row_pos = jnp.repeat(jnp.arange(n_rows), GRID_W)
    col_pos = jnp.tile(jnp.arange(GRID_W), n_rows)
    s_lat = jax.nn.silu(c)
    s_ctx = jax.nn.silu(c_ctx)
    xc, xl = ctx, x
    for layer in range(DEPTH):
        last = layer == DEPTH - 1
        i = layer // 2
        m_l = jnp.split((s_lat @ w_mod[layer] + b_mod[layer])[:, None, :], N_MOD, axis=-1)
        m_c = jnp.split(s_ctx @ w_mod[layer] + b_mod[layer], N_MOD, axis=-1)
        z = jnp.concatenate([modulate(xc, m_c[0], m_c[1]), modulate(xl, m_l[0], m_l[1])], axis=1)
        if layer % 2 == 0:
            p = z @ even_w_in[i]
            pa, pb = p[..., :A_PROJ], p[..., A_PROJ:]
            pa = pa + even_mu[i] * (per_segment(centred_shift, pa, n_ctx) - pa)
            ya = rwkv7_mix(pa, n_ctx, rwkv_w0[i], rwkv_w_up[i], rwkv_a0[i], rwkv_a_up[i], rwkv_g_up[i],
                           rwkv_k_k[i], rwkv_k_a[i], rwkv_r_k[i], rwkv_ln_w[i], rwkv_ln_b[i])
            yb = rglru_mix(pb, n_ctx, lru_conv_w[i], lru_conv_b[i], lru_wa[i], lru_ba[i], lru_wx[i],
                           lru_bx[i], lru_lam[i])
            y = jnp.concatenate([ya, yb], axis=-1)
            yl = y[:, n_ctx:] @ even_w_out[i]
            yc = None if last else y[:, :n_ctx] @ even_w_out[i]
        else:
            lam_init = 0.8 - 0.6 * math.exp(-0.3 * layer)
            o_c, o_l = diff_attention(z, n_ctx, row_pos, col_pos, lam_init, not last, attn_w_qkv[i],
                                      attn_q_g[i], attn_k_g[i], attn_lam_q1[i], attn_lam_k1[i],
                                      attn_lam_q2[i], attn_lam_k2[i], attn_sub_g[i])
            yl = o_l @ attn_w_o[i]
            yc = None if last else o_c @ attn_w_o[i]
        xl = xl + m_l[2] * yl
        hl = modulate(xl, m_l[3], m_l[4])
        if last:
            xl = xl + m_l[5] * peer_ffn(hl, peer_w_q[layer], peer_keys[layer], peer_u[layer], peer_v[layer])
        else:
            xc = xc + m_c[2] * yc
            hc = modulate(xc, m_c[3], m_c[4])
            f = peer_ffn(jnp.concatenate([hc, hl], axis=1), peer_w_q[layer], peer_keys[layer],
                         peer_u[layer], peer_v[layer])
            xc = xc + m_c[5] * f[:, :n_ctx]
            xl = xl + m_l[5] * f[:, n_ctx:]
    return xl
```

```python
import functools
import math

import jax
import jax.numpy as jnp
from jax import lax
from jax.experimental import pallas as pl
from jax.experimental.pallas import tpu as pltpu

F32 = jnp.float32
BF16 = jnp.bfloat16
I32 = jnp.int32

V7X_LANES = 128
V7X_SUBLANES = 8
V7X_VMEM_LIMIT_BYTES = 56 * 1024 * 1024

GRID_W = 64
N_MOD = 6
A_HEAD_DIM = 64
DECAY_SCALE = math.exp(-0.5)
GN_EPS = 64e-5
DECAY_RANK = 64
ICLR_RANK = 64
GATE_RANK = 128
B_BLOCK_DIM = 64
CONV_W = 4
LRU_C = 8.0
C_QK_DIM = 64
C_V_DIM = 128
ROPE_BASE = 10000.0
N_KEYS = 128
PEER_HEADS = 8
PEER_TOPK = 16
RMS_EPS = 1e-6


def _cparams(sem):
    return pltpu.CompilerParams(dimension_semantics=sem, vmem_limit_bytes=V7X_VMEM_LIMIT_BYTES)


def _token_tile(n_ctx, n_lat, want):
    t = want
    while n_ctx % t or n_lat % t:
        t //= 2
    return t


def _mod_spec(d, ctx_tiles):
    return pl.BlockSpec((1, 1, 1, d), lambda b, j, *_: (b, jnp.where(j >= ctx_tiles, 1, 0), 0, 0))


def _rms_modulate(x, shift, scale):
    xn = x * lax.rsqrt(jnp.mean(x * x, axis=-1, keepdims=True) + RMS_EPS)
    return xn * (1.0 + scale) + shift


def _modmm_body(x_ref, sh_ref, sc_ref, w_ref, o_ref):
    z = _rms_modulate(x_ref[0], sh_ref[0, 0], sc_ref[0, 0])
    o_ref[0] = jnp.dot(z.astype(BF16), w_ref[...], preferred_element_type=F32)


def modulate_matmul(xs, shift, scale, w, n_ctx):
    b, l, d = xs.shape
    n = w.shape[1]
    tm = _token_tile(n_ctx, l - n_ctx, 256)
    ctx_tiles = n_ctx // tm
    return pl.pallas_call(
        _modmm_body,
        grid=(b, l // tm),
        in_specs=[
            pl.BlockSpec((1, tm, d), lambda i, j: (i, j, 0)),
            _mod_spec(d, ctx_tiles),
            _mod_spec(d, ctx_tiles),
            pl.BlockSpec((d, n), lambda i, j: (0, 0)),
        ],
        out_specs=pl.BlockSpec((1, tm, n), lambda i, j: (i, j, 0)),
        out_shape=jax.ShapeDtypeStruct((b, l, n), F32),
        compiler_params=_cparams(("parallel", "parallel")),
        name="modulate_matmul",
    )(xs, shift, scale, w.astype(BF16))


def _outproj_body(y_ref, x_ref, g_ref, w_ref, o_ref):
    acc = jnp.dot(y_ref[0].astype(BF16), w_ref[...], preferred_element_type=F32)
    o_ref[0] = x_ref[0] + g_ref[0, 0] * acc


def gated_out_proj(y, xs, gate, w, n_ctx):
    b, l, d = xs.shape
    k = y.shape[-1]
    tm = _token_tile(n_ctx, l - n_ctx, 256)
    ctx_tiles = n_ctx // tm
    return pl.pallas_call(
        _outproj_body,
        grid=(b, l // tm),
        in_specs=[
            pl.BlockSpec((1, tm, k), lambda i, j: (i, j, 0)),
            pl.BlockSpec((1, tm, d), lambda i, j: (i, j, 0)),
            _mod_spec(d, ctx_tiles),
            pl.BlockSpec((k, d), lambda i, j: (0, 0)),
        ],
        out_specs=pl.BlockSpec((1, tm, d), lambda i, j: (i, j, 0)),
        out_shape=jax.ShapeDtypeStruct((b, l, d), F32),
        compiler_params=_cparams(("parallel", "parallel")),
        name="gated_out_proj",
    )(y, xs, gate, w.astype(BF16))


RWKV_VBLK_A = 16
RWKV_VBLK_B = 8


def _rwkv_scan_body(r_ref, w_ref, k_ref, kk_ref, akk_ref, v_ref, o_ref, t_ref, sa_ref, *, tb, nk2, nv):
    @pl.when(pl.program_id(0) == 0)
    def _():
        t_ref[...] = jnp.zeros_like(t_ref)

    half = V7X_SUBLANES // 2

    def fold(acc):
        return acc + pltpu.roll(acc, half, axis=1)

    def step(t, carry):
        for vb in range(nv // RWKV_VBLK_A):
            vs = pl.ds(vb * RWKV_VBLK_A, RWKV_VBLK_A)

            def acc_sa(k2, acc):
                return acc + t_ref[k2, vs] * kk_ref[t, k2][None]

            acc = lax.fori_loop(0, nk2, acc_sa, jnp.zeros((RWKV_VBLK_A, V7X_SUBLANES, V7X_LANES), F32), unroll=4)
            sa_ref[vs] = fold(acc)
        for vb in range(nv // RWKV_VBLK_B):
            vs = pl.ds(vb * RWKV_VBLK_B, RWKV_VBLK_B)
            sa = sa_ref[vs]
            vv = v_ref[t, vs]

            def upd(k2, acc):
                tn = (t_ref[k2, vs] * w_ref[t, k2][None] - akk_ref[t, k2][None] * sa
                      + k_ref[t, k2][None] * vv)
                t_ref[k2, vs] = tn
                return acc + tn * r_ref[t, k2][None]

            acc = lax.fori_loop(0, nk2, upd, jnp.zeros((RWKV_VBLK_B, V7X_SUBLANES, V7X_LANES), F32), unroll=4)
            o_ref[t, vs] = fold(acc)
        return carry

    lax.fori_loop(0, tb, step, 0)


def rwkv_scan(r, w, k, kk, akk, v):
    l, nk2 = r.shape[0], r.shape[1]
    nv = v.shape[1]
    tb = 8
    kspec = pl.BlockSpec((tb, nk2, V7X_SUBLANES, V7X_LANES), lambda i: (i, 0, 0, 0))
    vspec = pl.BlockSpec((tb, nv, V7X_SUBLANES, V7X_LANES), lambda i: (i, 0, 0, 0))
    return pl.pallas_call(
        functools.partial(_rwkv_scan_body, tb=tb, nk2=nk2, nv=nv),
        grid=(l // tb,),
        in_specs=[kspec, kspec, kspec, kspec, kspec, vspec],
        out_specs=vspec,
        out_shape=jax.ShapeDtypeStruct((l, nv, V7X_SUBLANES, V7X_LANES), F32),
        scratch_shapes=[
            pltpu.VMEM((nk2, nv, V7X_SUBLANES, V7X_LANES), F32),
            pltpu.VMEM((nv, V7X_SUBLANES, V7X_LANES), F32),
        ],
        compiler_params=_cparams(("arbitrary",)),
        name="rwkv_scan",
    )(r, w, k, kk, akk, v)


def _diag_scan_body(a_ref, u_ref, o_ref, h_ref, *, tb):
    @pl.when(pl.program_id(0) == 0)
    def _():
        h_ref[...] = jnp.zeros_like(h_ref)

    def step(t, h):
        h = a_ref[t] * h + u_ref[t]
        o_ref[t] = h
        return h

    h_ref[...] = lax.fori_loop(0, tb, step, h_ref[...])


def diag_scan(a, u):
    l, rows, lanes = a.shape
    tb = 8
    spec = pl.BlockSpec((tb, rows, lanes), lambda i: (i, 0, 0))
    return pl.pallas_call(
        functools.partial(_diag_scan_body, tb=tb),
        grid=(l // tb,),
        in_specs=[spec, spec],
        out_specs=spec,
        out_shape=jax.ShapeDtypeStruct((l, rows, lanes), F32),
        scratch_shapes=[pltpu.VMEM((rows, lanes), F32)],
        compiler_params=_cparams(("arbitrary",)),
        name="diag_scan",
    )(a, u)


def _softmax_rows(s):
    m = jnp.max(s, axis=-1, keepdims=True)
    e = jnp.exp(s - m)
    return e / jnp.sum(e, axis=-1, keepdims=True)


def _attn_body(lam_ref, q_ref, k_ref, v_ref, sg_ref, o_ref, *, ctx_tiles, n_ctx, scale, out_scale):
    j = pl.program_id(2)
    q = q_ref[0]
    lane = lax.broadcasted_iota(I32, q.shape, 1)
    q1 = jnp.where(lane < C_QK_DIM, q, 0.0).astype(BF16)
    q2 = jnp.where(lane >= C_QK_DIM, q, 0.0).astype(BF16)
    lam = lam_ref[0]
    nt = (((1,), (1,)), ((), ()))

    def attend(kb, vb):
        p1 = _softmax_rows(lax.dot_general(q1, kb, nt, preferred_element_type=F32) * scale)
        p2 = _softmax_rows(lax.dot_general(q2, kb, nt, preferred_element_type=F32) * scale)
        a = (p1 - lam * p2).astype(BF16)
        o = jnp.dot(a, vb, preferred_element_type=F32)
        o = o * lax.rsqrt(jnp.mean(o * o, axis=-1, keepdims=True) + RMS_EPS)
        o_ref[0] = o * sg_ref[...] * out_scale

    @pl.when(j < ctx_tiles)
    def _():
        attend(k_ref[0, :n_ctx], v_ref[0, :n_ctx])

    @pl.when(j >= ctx_tiles)
    def _():
        attend(k_ref[0], v_ref[0])


def diff_attention_core(q, k, v, lam, sub_g, n_ctx, lam_init):
    b, l, width = q.shape
    heads = width // C_V_DIM
    tq = _token_tile(n_ctx, l - n_ctx, 256)
    ctx_tiles = n_ctx // tq
    body = functools.partial(_attn_body, ctx_tiles=ctx_tiles, n_ctx=n_ctx, scale=C_QK_DIM ** -0.5,
                             out_scale=1.0 - lam_init)
    return pl.pallas_call(
        body,
        grid=(b, heads, l // tq),
        in_specs=[
            pl.BlockSpec(memory_space=pltpu.SMEM),
            pl.BlockSpec((1, tq, C_V_DIM), lambda i, h, j: (i, j, h)),
            pl.BlockSpec((1, l, C_V_DIM), lambda i, h, j: (i, 0, h)),
            pl.BlockSpec((1, l, C_V_DIM), lambda i, h, j: (i, 0, h)),
            pl.BlockSpec((1, C_V_DIM), lambda i, h, j: (0, 0)),
        ],
        out_specs=pl.BlockSpec((1, tq, C_V_DIM), lambda i, h, j: (i, j, h)),
        out_shape=jax.ShapeDtypeStruct((b, l, width), F32),
        compiler_params=_cparams(("parallel", "parallel", "parallel")),
        name="diff_attention",
    )(lam, q, k, v, sub_g)


def _peer_candidates():
    return [(a, b) for a in range(PEER_TOPK) for b in range(PEER_TOPK) if (a + 1) * (b + 1) <= PEER_TOPK]


def _peer_route_body(x_ref, sh_ref, sc_ref, wq_ref, keys_ref, h_ref, i_ref, j_ref, g_ref,
                     sv_ref, si_ref, cand_ref, best_ref, bi_ref, bj_ref, *, tm):
    h = _rms_modulate(x_ref[0], sh_ref[0, 0], sc_ref[0, 0]).astype(BF16)
    h_ref[0] = h
    q = jnp.dot(h, wq_ref[...], preferred_element_type=F32).astype(BF16)
    nt = (((1,), (1,)), ((), ()))
    key_iota = lax.broadcasted_iota(I32, (N_KEYS, tm), 0).astype(F32)
    neg = jnp.float32(-jnp.inf)

    for hd in range(PEER_HEADS):
        for m in range(2):
            c0 = (hd * 2 + m) * N_KEYS
            s = lax.dot_general(keys_ref[m], q[:, c0:c0 + N_KEYS], nt, preferred_element_type=F32)

            def extract(a, s, hd=hd, m=m):
                mx = jnp.max(s, axis=0, keepdims=True)
                idx = jnp.min(jnp.where(s == mx, key_iota, N_KEYS), axis=0, keepdims=True)
                sv_ref[m, a, pl.ds(hd, 1), :] = mx
                si_ref[m, a, pl.ds(hd, 1), :] = idx
                return jnp.where(key_iota == idx, neg, s)

            lax.fori_loop(0, PEER_TOPK, extract, s)

    cands = _peer_candidates()
    for c, (a, b) in enumerate(cands):
        cand_ref[c] = sv_ref[0, a] + sv_ref[1, b]
    big = jnp.float32(PEER_TOPK * PEER_TOPK)

    def pick(k, carry):
        vals = [cand_ref[c] for c in range(len(cands))]
        mx = functools.reduce(jnp.maximum, vals)
        pos = functools.reduce(
            jnp.minimum,
            [jnp.where(vals[c] == mx, jnp.float32(a * PEER_TOPK + b), big) for c, (a, b) in enumerate(cands)])
        ii = jnp.zeros_like(pos)
        jj = jnp.zeros_like(pos)
        for c, (a, b) in enumerate(cands):
            hit = pos == jnp.float32(a * PEER_TOPK + b)
            cand_ref[c] = jnp.where(hit, neg, vals[c])
            ii = jnp.where(hit, si_ref[0, a], ii)
            jj = jnp.where(hit, si_ref[1, b], jj)
        best_ref[k] = mx
        bi_ref[k] = ii
        bj_ref[k] = jj
        return carry

    lax.fori_loop(0, PEER_TOPK, pick, 0)
    best = best_ref[...]
    e = jnp.exp(best - best[0][None])
    g = e / jnp.sum(e, axis=0, keepdims=True)
    g_ref[0] = g.reshape(PEER_TOPK * PEER_HEADS, tm)
    i_ref[0] = bi_ref[...].reshape(PEER_TOPK * PEER_HEADS, tm)
    j_ref[0] = bj_ref[...].reshape(PEER_TOPK * PEER_HEADS, tm)


def peer_route(xs, shift, scale, w_q, keys, n_ctx):
    b, l, d = xs.shape
    tm = _token_tile(n_ctx, l - n_ctx, 256)
    ctx_tiles = n_ctx // tm
    nsel = PEER_TOPK * PEER_HEADS
    ncand = len(_peer_candidates())
    sel_spec = pl.BlockSpec((1, nsel, tm), lambda i, j: (i, 0, j))
    return pl.pallas_call(
        functools.partial(_peer_route_body, tm=tm),
        grid=(b, l // tm),
        in_specs=[
            pl.BlockSpec((1, tm, d), lambda i, j: (i, j, 0)),
            _mod_spec(d, ctx_tiles),
            _mod_spec(d, ctx_tiles),
            pl.BlockSpec(w_q.shape, lambda i, j: (0, 0)),
            pl.BlockSpec(keys.shape, lambda i, j: (0, 0, 0)),
        ],
        out_specs=[pl.BlockSpec((1, tm, d), lambda i, j: (i, j, 0)), sel_spec, sel_spec, sel_spec],
        out_shape=[
            jax.ShapeDtypeStruct((b, l, d), BF16),
            jax.ShapeDtypeStruct((b, nsel, l), F32),
            jax.ShapeDtypeStruct((b, nsel, l), F32),
            jax.ShapeDtypeStruct((b, nsel, l), F32),
        ],
        scratch_shapes=[
            pltpu.VMEM((2, PEER_TOPK, PEER_HEADS, tm), F32),
            pltpu.VMEM((2, PEER_TOPK, PEER_HEADS, tm), F32),
            pltpu.VMEM((ncand, PEER_HEADS, tm), F32),
            pltpu.VMEM((PEER_TOPK, PEER_HEADS, tm), F32),
            pltpu.VMEM((PEER_TOPK, PEER_HEADS, tm), F32),
            pltpu.VMEM((PEER_TOPK, PEER_HEADS, tm), F32),
        ],
        compiler_params=_cparams(("parallel", "parallel")),
        name="peer_route",
    )(xs, shift, scale, w_q.astype(BF16), keys.astype(BF16))


PEER_G_CHUNK = 32
PEER_PAIR = 2 * N_KEYS


def _peer_expert_body(h_ref, i_ref, j_ref, g_ref, u_ref, v_ref, x_ref, gate_ref, o_ref, gw_ref, acc_ref,
                      *, tm, eb):
    e = pl.program_id(2)
    nt = (((1,), (1,)), ((), ()))

    @pl.when(e == 0)
    def _():
        acc_ref[...] = jnp.zeros_like(acc_ref)
        isel = i_ref[0].T
        jsel = j_ref[0].T
        gsel = g_ref[0].T
        kio = lax.broadcasted_iota(I32, (PEER_G_CHUNK, N_KEYS, isel.shape[1]), 1).astype(F32)
        for c in range(tm // PEER_G_CHUNK):
            rows = slice(c * PEER_G_CHUNK, (c + 1) * PEER_G_CHUNK)
            oh_i = jnp.where(isel[rows][:, None, :] == kio, 1.0, 0.0).astype(BF16)
            oh_j = jnp.where(jsel[rows][:, None, :] == kio, gsel[rows][:, None, :], 0.0).astype(BF16)
            g3 = jnp.einsum("pis,pjs->pij", oh_i, oh_j, preferred_element_type=F32)
            gw_ref[:, rows, :] = jnp.swapaxes(g3, 0, 1)

    h = h_ref[0]
    for ip in range(eb // PEER_PAIR):
        rows = slice(ip * PEER_PAIR, (ip + 1) * PEER_PAIR)
        act = jax.nn.gelu(lax.dot_general(h, u_ref[rows, :], nt, preferred_element_type=F32))
        i0 = e * (eb // N_KEYS) + ip * 2
        gw = jnp.concatenate([gw_ref[i0], gw_ref[i0 + 1]], axis=-1)
        acc_ref[...] += jnp.dot((act * gw).astype(BF16), v_ref[rows, :], preferred_element_type=F32)

    @pl.when(e == pl.num_programs(2) - 1)
    def _():
        o_ref[0] = x_ref[0] + gate_ref[0, 0] * acc_ref[...]


def peer_experts(h, isel, jsel, gsel, u_tab, v_tab, xs, gate, n_ctx):
    b, l, d = xs.shape
    ne = u_tab.shape[0]
    tm = _token_tile(n_ctx, l - n_ctx, 256)
    ctx_tiles = n_ctx // tm
    eb = 2048
    nsel = isel.shape[1]
    sel_spec = pl.BlockSpec((1, nsel, tm), lambda i, j, e: (i, 0, j))
    tok_spec = pl.BlockSpec((1, tm, d), lambda i, j, e: (i, j, 0))
    return pl.pallas_call(
        functools.partial(_peer_expert_body, tm=tm, eb=eb),
        grid=(b, l // tm, ne // eb),
        in_specs=[
            tok_spec, sel_spec, sel_spec, sel_spec,
            pl.BlockSpec((eb, d), lambda i, j, e: (e, 0)),
            pl.BlockSpec((eb, d), lambda i, j, e: (e, 0)),
            tok_spec,
            _mod_spec(d, ctx_tiles),
        ],
        out_specs=tok_spec,
        out_shape=jax.ShapeDtypeStruct((b, l, d), F32),
        scratch_shapes=[
            pltpu.VMEM((N_KEYS, tm, N_KEYS), F32),
            pltpu.VMEM((tm, d), F32),
        ],
        compiler_params=_cparams(("parallel", "parallel", "arbitrary")),
        name="peer_experts",
    )(h, isel, jsel, gsel, u_tab, v_tab, xs, gate)


def _per_segment(f, z, n_ctx):
    return jnp.concatenate([f(z[:, :n_ctx]), f(z[:, n_ctx:])], axis=1)


def _centred_shift(p):
    pp = jnp.pad(p, ((0, 0), (1, 1), (0, 0)))
    return 0.5 * (pp[:, :-2] + pp[:, 2:])


def _flip_segments(z, n_ctx):
    return _per_segment(lambda s: s[:, ::-1], z, n_ctx)


RWKV_CHAINS = (V7X_SUBLANES // 2) * V7X_LANES


def _rwkv_chain_layout(zf, zb, n_ctx, heads):
    b, l, width = zf.shape
    n = width // heads
    z = jnp.stack([zf, _flip_segments(zb, n_ctx)], axis=0).reshape(2, b, l, heads, n)
    z = z.transpose(2, 4, 0, 1, 3).reshape(l, n, 2 * b * heads)
    z = jnp.pad(z, ((0, 0), (0, 0), (0, RWKV_CHAINS - 2 * b * heads)))
    return z.reshape(l, n, V7X_SUBLANES // 2, V7X_LANES)


def _rwkv_keyed(zf, zb, n_ctx, heads):
    z = _rwkv_chain_layout(zf, zb, n_ctx, heads)
    l, n = z.shape[:2]
    return z.reshape(l, n // 2, V7X_SUBLANES, V7X_LANES)


def _rwkv_valued(zf, zb, n_ctx, heads):
    z = _rwkv_chain_layout(zf, zb, n_ctx, heads)
    return jnp.concatenate([z, z], axis=2)


def _rwkv_mix(pa, n_ctx, w0, w_up, a0, a_up, g_up, k_k, k_a, r_k, ln_w, ln_b):
    b, l, _ = pa.shape
    width = w0.shape[-1]
    heads = width // A_HEAD_DIM
    assert 2 * b * heads <= RWKV_CHAINS
    cuts = (width, 2 * width, 3 * width, 3 * width + DECAY_RANK, 3 * width + 2 * DECAY_RANK,
            3 * width + 2 * DECAY_RANK + ICLR_RANK, 3 * width + 2 * DECAY_RANK + 2 * ICLR_RANK)
    r, k, v, wd_f, wd_b, ad_f, ad_b, gd = jnp.split(pa, cuts, axis=-1)

    def heads_of(t):
        return t.reshape(b, l, heads, A_HEAD_DIM)

    kk = heads_of(k * k_k)
    kk = kk / jnp.maximum(jnp.sqrt(jnp.sum(kk * kk, axis=-1, keepdims=True)), 1e-12)
    kk = kk.reshape(b, l, width)

    def direction(d, wd, ad):
        decay = jnp.exp(-DECAY_SCALE * jax.nn.sigmoid(w0[d] + jnp.tanh(wd) @ w_up[d]))
        a = jax.nn.sigmoid(a0[d] + ad @ a_up[d])
        kt = k * (1.0 + (a - 1.0) * k_a)
        return decay, kt, a * kk

    dec_f, kt_f, akk_f = direction(0, wd_f, ad_f)
    dec_b, kt_b, akk_b = direction(1, wd_b, ad_b)
    o = rwkv_scan(
        _rwkv_keyed(r, r, n_ctx, heads), _rwkv_keyed(dec_f, dec_b, n_ctx, heads),
        _rwkv_keyed(kt_f, kt_b, n_ctx, heads), _rwkv_keyed(kk, kk, n_ctx, heads),
        _rwkv_keyed(akk_f, akk_b, n_ctx, heads), _rwkv_valued(v, v, n_ctx, heads))
    o = o[:, :, :V7X_SUBLANES // 2].reshape(l, A_HEAD_DIM, RWKV_CHAINS)[:, :, :2 * b * heads]
    o = o.reshape(l, A_HEAD_DIM, 2, b, heads).transpose(2, 3, 0, 4, 1).reshape(2, b, l, width)
    o = heads_of(o[0] + _flip_segments(o[1], n_ctx))
    mean = jnp.mean(o, axis=-1, keepdims=True)
    var = jnp.mean(jnp.square(o - mean), axis=-1, keepdims=True)
    on = ((o - mean) * lax.rsqrt(var + GN_EPS)).reshape(b, l, width) * ln_w + ln_b
    bonus = jnp.sum(heads_of(r) * heads_of(0.5 * (kt_f + kt_b)) * r_k, axis=-1, keepdims=True) * heads_of(v)
    g = jax.nn.sigmoid(gd) @ g_up
    return (on + bonus.reshape(b, l, width)) * g


def _depthwise_conv(s, w, bias):
    t_len = s.shape[1]
    left = CONV_W // 2
    sp = jnp.pad(s, ((0, 0), (left, CONV_W - 1 - left), (0, 0)))
    out = sp[:, 0:t_len] * w[0]
    for j in range(1, CONV_W):
        out = out + sp[:, j:j + t_len] * w[j]
    return out + bias


def _rglru_mix(pb, n_ctx, conv_w, conv_b, wa, ba, wx, bx, lam):
    b, l, _ = pb.shape
    width = conv_b.shape[-1]
    xb = _per_segment(lambda s: _depthwise_conv(s, conv_w, conv_b), pb[..., :width], n_ctx)
    gate = jax.nn.gelu(pb[..., width:])
    xblk = xb.reshape(b, l, width // B_BLOCK_DIM, B_BLOCK_DIM)

    def direction(d):
        rg = jax.nn.sigmoid(jnp.einsum('blnc,ncd->blnd', xblk, wa[d]).reshape(b, l, width) + ba[d])
        ig = jax.nn.sigmoid(jnp.einsum('blnc,ncd->blnd', xblk, wx[d]).reshape(b, l, width) + bx[d])
        log_a = -LRU_C * rg * jax.nn.softplus(-lam[d])
        u = jnp.sqrt(-jnp.expm1(2.0 * log_a)) * ig * xb
        return jnp.exp(log_a), u

    a_f, u_f = direction(0)
    a_b, u_b = direction(1)

    def chains(zf, zb):
        z = jnp.stack([zf, _flip_segments(zb, n_ctx)], axis=0)
        return z.transpose(2, 0, 1, 3).reshape(l, 2 * b * width // V7X_LANES, V7X_LANES)

    h = diag_scan(chains(a_f, a_b), chains(u_f, u_b)).reshape(l, 2, b, width).transpose(1, 2, 0, 3)
    return (h[0] + _flip_segments(h[1], n_ctx)) * gate


def _rms(x):
    return x * lax.rsqrt(jnp.mean(x * x, axis=-1, keepdims=True) + RMS_EPS)


def _rope_1d(x, pos):
    half = x.shape[-1] // 2
    inv_freq = ROPE_BASE ** (-jnp.arange(half, dtype=F32) / half)
    ang = pos.astype(F32)[:, None] * inv_freq
    cos = jnp.cos(ang)[:, None, None, :]
    sin = jnp.sin(ang)[:, None, None, :]
    x1, x2 = x[..., :half], x[..., half:]
    return jnp.concatenate([x1 * cos - x2 * sin, x1 * sin + x2 * cos], axis=-1)


def _rope_2d(x, row_pos, col_pos):
    r = x.shape[-1] // 2
    return jnp.concatenate([_rope_1d(x[..., :r], row_pos), _rope_1d(x[..., r:], col_pos)], axis=-1)


def _diff_attention(qkv, n_ctx, lam_init, q_g, k_g, lq1, lk1, lq2, lk2, sub_g):
    b, l, width3 = qkv.shape
    width = width3 // 3
    heads = width // C_V_DIM
    n_lat = l - n_ctx
    n_rows = n_lat // GRID_W
    row_pos = jnp.repeat(jnp.arange(n_rows), GRID_W)
    col_pos = jnp.tile(jnp.arange(GRID_W), n_rows)
    q = _rms(qkv[..., :width].reshape(b, l, heads, 2, C_QK_DIM)) * q_g
    k = _rms(qkv[..., width:2 * width].reshape(b, l, heads, 2, C_QK_DIM)) * k_g
    v = qkv[..., 2 * width:]
    q = jnp.concatenate([q[:, :n_ctx], _rope_2d(q[:, n_ctx:], row_pos, col_pos)], axis=1)
    k = jnp.concatenate([k[:, :n_ctx], _rope_2d(k[:, n_ctx:], row_pos, col_pos)], axis=1)
    lam = jnp.exp(jnp.sum(lq1 * lk1)) - jnp.exp(jnp.sum(lq2 * lk2)) + lam_init
    return diff_attention_core(q.reshape(b, l, width), k.reshape(b, l, width).astype(BF16), v.astype(BF16),
                               lam.reshape(1), sub_g.reshape(1, C_V_DIM), n_ctx, lam_init)


def kernel(x, c, ctx, c_ctx, w_mod, b_mod, even_w_in, even_mu, even_w_out, rwkv_w0, rwkv_w_up, rwkv_a0, rwkv_a_up, rwkv_g_up, rwkv_k_k, rwkv_k_a, rwkv_r_k, rwkv_ln_w, rwkv_ln_b, lru_conv_w, lru_conv_b, lru_wa, lru_ba, lru_wx, lru_bx, lru_lam, attn_w_qkv, attn_q_g, attn_k_g, attn_lam_q1, attn_lam_k1, attn_lam_q2, attn_lam_k2, attn_sub_g, attn_w_o, peer_w_q, peer_keys, peer_u, peer_v):
    bsz, n_lat, d = x.shape
    n_ctx = ctx.shape[1]
    depth = w_mod.shape[0]
    a_proj = even_mu.shape[-1]
    s_lat = jax.nn.silu(c)
    s_ctx = jax.nn.silu(c_ctx)
    xs = jnp.concatenate([ctx, x], axis=1)
    for layer in range(depth):
        i = layer // 2
        m_l = s_lat @ w_mod[layer] + b_mod[layer]
        m_c = s_ctx @ w_mod[layer] + b_mod[layer]
        mods = jnp.stack([jnp.broadcast_to(m_c, m_l.shape), m_l], axis=1).reshape(bsz, 2, N_MOD, 1, d)
        shift1, scale1, gate1, shift2, scale2, gate2 = (mods[:, :, n] for n in range(N_MOD))
        if layer % 2 == 0:
            p = modulate_matmul(xs, shift1, scale1, even_w_in[i], n_ctx)
            pa, pb = p[..., :a_proj], p[..., a_proj:]
            pa = pa + even_mu[i] * (_per_segment(_centred_shift, pa, n_ctx) - pa)
            ya = _rwkv_mix(pa, n_ctx, rwkv_w0[i], rwkv_w_up[i], rwkv_a0[i], rwkv_a_up[i], rwkv_g_up[i],
                           rwkv_k_k[i], rwkv_k_a[i], rwkv_r_k[i], rwkv_ln_w[i], rwkv_ln_b[i])
            yb = _rglru_mix(pb, n_ctx, lru_conv_w[i], lru_conv_b[i], lru_wa[i], lru_ba[i], lru_wx[i],
                            lru_bx[i], lru_lam[i])
            xs = gated_out_proj(jnp.concatenate([ya, yb], axis=-1), xs, gate1, even_w_out[i], n_ctx)
        else:
            lam_init = 0.8 - 0.6 * math.exp(-0.3 * layer)
            qkv = modulate_matmul(xs, shift1, scale1, attn_w_qkv[i], n_ctx)
            o = _diff_attention(qkv, n_ctx, lam_init, attn_q_g[i], attn_k_g[i], attn_lam_q1[i],
                                attn_lam_k1[i], attn_lam_q2[i], attn_lam_k2[i], attn_sub_g[i])
            xs = gated_out_proj(o, xs, gate1, attn_w_o[i], n_ctx)
        h, isel, jsel, gsel = peer_route(xs, shift2, scale2, peer_w_q[layer], peer_keys[layer], n_ctx)
        xs = peer_experts(h, isel, jsel, gsel, peer_u[layer].astype(BF16), peer_v[layer].astype(BF16), xs,
                          gate2, n_ctx)
    return xs[:, n_ctx:]
```

```python
import functools
import math

import jax
import jax.numpy as jnp
from jax import lax
from jax.experimental import pallas as pl
from jax.experimental.pallas import tpu as pltpu

F32 = jnp.float32
BF16 = jnp.bfloat16
I32 = jnp.int32

V7X_LANES = 128
V7X_SUBLANES = 8
V7X_VMEM_LIMIT_BYTES = 56 * 1024 * 1024

GRID_W = 64
N_MOD = 6
A_HEAD_DIM = 64
DECAY_SCALE = math.exp(-0.5)
GN_EPS = 64e-5
DECAY_RANK = 64
ICLR_RANK = 64
GATE_RANK = 128
B_BLOCK_DIM = 64
CONV_W = 4
LRU_C = 8.0
C_QK_DIM = 64
C_V_DIM = 128
ROPE_BASE = 10000.0
N_KEYS = 128
PEER_HEADS = 8
PEER_TOPK = 16
RMS_EPS = 1e-6


def _cparams(sem):
    return pltpu.CompilerParams(dimension_semantics=sem, vmem_limit_bytes=V7X_VMEM_LIMIT_BYTES)


def _token_tile(n_ctx, n_lat, want):
    t = want
    while n_ctx % t or n_lat % t:
        t //= 2
    return t


def _mod_spec(d, ctx_tiles):
    return pl.BlockSpec((1, 1, 1, d), lambda b, j, *_: (b, jnp.where(j >= ctx_tiles, 1, 0), 0, 0))


def _rms_modulate(x, shift, scale):
    xn = x * lax.rsqrt(jnp.mean(x * x, axis=-1, keepdims=True) + RMS_EPS)
    return xn * (1.0 + scale) + shift


def _modmm_body(x_ref, sh_ref, sc_ref, w_ref, o_ref):
    z = _rms_modulate(x_ref[0], sh_ref[0, 0], sc_ref[0, 0])
    o_ref[0] = jnp.dot(z.astype(BF16), w_ref[...], preferred_element_type=F32)


def modulate_matmul(xs, shift, scale, w, n_ctx):
    b, l, d = xs.shape
    n = w.shape[1]
    tm = _token_tile(n_ctx, l - n_ctx, 256)
    ctx_tiles = n_ctx // tm
    return pl.pallas_call(
        _modmm_body,
        grid=(b, l // tm),
        in_specs=[
            pl.BlockSpec((1, tm, d), lambda i, j: (i, j, 0)),
            _mod_spec(d, ctx_tiles),
            _mod_spec(d, ctx_tiles),
            pl.BlockSpec((d, n), lambda i, j: (0, 0)),
        ],
        out_specs=pl.BlockSpec((1, tm, n), lambda i, j: (i, j, 0)),
        out_shape=jax.ShapeDtypeStruct((b, l, n), F32),
        compiler_params=_cparams(("parallel", "parallel")),
        name="modulate_matmul",
    )(xs, shift, scale, w.astype(BF16))


def _outproj_body(y_ref, x_ref, g_ref, w_ref, o_ref):
    acc = jnp.dot(y_ref[0].astype(BF16), w_ref[...], preferred_element_type=F32)
    o_ref[0] = x_ref[0] + g_ref[0, 0] * acc


def gated_out_proj(y, xs, gate, w, n_ctx):
    b, l, d = xs.shape
    k = y.shape[-1]
    tm = _token_tile(n_ctx, l - n_ctx, 256)
    ctx_tiles = n_ctx // tm
    return pl.pallas_call(
        _outproj_body,
        grid=(b, l // tm),
        in_specs=[
            pl.BlockSpec((1, tm, k), lambda i, j: (i, j, 0)),
            pl.BlockSpec((1, tm, d), lambda i, j: (i, j, 0)),
            _mod_spec(d, ctx_tiles),
            pl.BlockSpec((k, d), lambda i, j: (0, 0)),
        ],
        out_specs=pl.BlockSpec((1, tm, d), lambda i, j: (i, j, 0)),
        out_shape=jax.ShapeDtypeStruct((b, l, d), F32),
        compiler_params=_cparams(("parallel", "parallel")),
        name="gated_out_proj",
    )(y, xs, gate, w.astype(BF16))


RWKV_VBLK_A = 16
RWKV_VBLK_B = 8
RWKV_KPAR = 4
RWKV_CHAINS = (V7X_SUBLANES // RWKV_KPAR) * V7X_LANES
SCAN_TB = 8


def _reversed_block_map(n_blocks, ctx_blocks):
    def rev(i):
        return jnp.where(i < ctx_blocks, ctx_blocks - 1 - i, n_blocks - 1 - i + ctx_blocks)
    return rev


def _rwkv_scan_body(rf_ref, wf_ref, kf_ref, kkf_ref, akkf_ref, vf_ref,
                    rb_ref, wb_ref, kb_ref, kkb_ref, akkb_ref, vb_ref,
                    of_ref, ob_ref, tf_ref, tb_ref, sa_ref, *, tb, nk4, nv):
    @pl.when(pl.program_id(0) == 0)
    def _():
        tf_ref[...] = jnp.zeros_like(tf_ref)
        tb_ref[...] = jnp.zeros_like(tb_ref)

    def fold(acc):
        acc = acc + pltpu.roll(acc, V7X_SUBLANES // 2, axis=1)
        return acc + pltpu.roll(acc, V7X_SUBLANES // 4, axis=1)

    def one_direction(t, r_ref, w_ref, k_ref, kk_ref, akk_ref, v_ref, o_ref, t_ref):
        for vb in range(nv // RWKV_VBLK_A):
            vs = pl.ds(vb * RWKV_VBLK_A, RWKV_VBLK_A)

            def acc_sa(k4, acc):
                return acc + t_ref[k4, vs] * kk_ref[t, k4][None]

            acc = lax.fori_loop(0, nk4, acc_sa, jnp.zeros((RWKV_VBLK_A, V7X_SUBLANES, V7X_LANES), F32), unroll=4)
            sa_ref[vs] = fold(acc)
        for vb in range(nv // RWKV_VBLK_B):
            vs = pl.ds(vb * RWKV_VBLK_B, RWKV_VBLK_B)
            sa = sa_ref[vs]
            vv = v_ref[t, vs]

            def upd(k4, acc):
                tn = (t_ref[k4, vs] * w_ref[t, k4][None] - akk_ref[t, k4][None] * sa
                      + k_ref[t, k4][None] * vv)
                t_ref[k4, vs] = tn
                return acc + tn * r_ref[t, k4][None]

            acc = lax.fori_loop(0, nk4, upd, jnp.zeros((RWKV_VBLK_B, V7X_SUBLANES, V7X_LANES), F32), unroll=4)
            o_ref[t, vs] = fold(acc)

    def step(t, carry):
        one_direction(t, rf_ref, wf_ref, kf_ref, kkf_ref, akkf_ref, vf_ref, of_ref, tf_ref)
        one_direction(tb - 1 - t, rb_ref, wb_ref, kb_ref, kkb_ref, akkb_ref, vb_ref, ob_ref, tb_ref)
        return carry

    lax.fori_loop(0, tb, step, 0)


def rwkv_scan(r, w_f, w_b, k_f, k_b, kk, akk_f, akk_b, v, n_ctx):
    l, nk4 = r.shape[0], r.shape[1]
    nv = v.shape[1]
    tb = SCAN_TB
    rev = _reversed_block_map(l // tb, n_ctx // tb)
    kblk = (tb, nk4, V7X_SUBLANES, V7X_LANES)
    vblk = (tb, nv, V7X_SUBLANES, V7X_LANES)
    kf = pl.BlockSpec(kblk, lambda i: (i, 0, 0, 0))
    kb = pl.BlockSpec(kblk, lambda i: (rev(i), 0, 0, 0))
    vf = pl.BlockSpec(vblk, lambda i: (i, 0, 0, 0))
    vb = pl.BlockSpec(vblk, lambda i: (rev(i), 0, 0, 0))
    out = jax.ShapeDtypeStruct((l, nv, V7X_SUBLANES, V7X_LANES), F32)
    state = pltpu.VMEM((nk4, nv, V7X_SUBLANES, V7X_LANES), F32)
    return pl.pallas_call(
        functools.partial(_rwkv_scan_body, tb=tb, nk4=nk4, nv=nv),
        grid=(l // tb,),
        in_specs=[kf, kf, kf, kf, kf, vf, kb, kb, kb, kb, kb, vb],
        out_specs=[vf, vb],
        out_shape=[out, out],
        scratch_shapes=[state, state, pltpu.VMEM((nv, V7X_SUBLANES, V7X_LANES), F32)],
        compiler_params=_cparams(("arbitrary",)),
        name="rwkv_scan",
    )(r, w_f, k_f, kk, akk_f, v, r, w_b, k_b, kk, akk_b, v)


def _diag_scan_body(af_ref, uf_ref, ab_ref, ub_ref, of_ref, ob_ref, hf_ref, hb_ref, *, tb):
    @pl.when(pl.program_id(0) == 0)
    def _():
        hf_ref[...] = jnp.zeros_like(hf_ref)
        hb_ref[...] = jnp.zeros_like(hb_ref)

    def step(t, carry):
        hf, hb = carry
        hf = af_ref[t] * hf + uf_ref[t]
        of_ref[t] = hf
        tr = tb - 1 - t
        hb = ab_ref[tr] * hb + ub_ref[tr]
        ob_ref[tr] = hb
        return hf, hb

    hf, hb = lax.fori_loop(0, tb, step, (hf_ref[...], hb_ref[...]))
    hf_ref[...] = hf
    hb_ref[...] = hb


def diag_scan(a_f, u_f, a_b, u_b, n_ctx):
    l, rows, lanes = a_f.shape
    tb = SCAN_TB
    rev = _reversed_block_map(l // tb, n_ctx // tb)
    fwd = pl.BlockSpec((tb, rows, lanes), lambda i: (i, 0, 0))
    bwd = pl.BlockSpec((tb, rows, lanes), lambda i: (rev(i), 0, 0))
    out = jax.ShapeDtypeStruct((l, rows, lanes), F32)
    return pl.pallas_call(
        functools.partial(_diag_scan_body, tb=tb),
        grid=(l // tb,),
        in_specs=[fwd, fwd, bwd, bwd],
        out_specs=[fwd, bwd],
        out_shape=[out, out],
        scratch_shapes=[pltpu.VMEM((rows, lanes), F32), pltpu.VMEM((rows, lanes), F32)],
        compiler_params=_cparams(("arbitrary",)),
        name="diag_scan",
    )(a_f, u_f, a_b, u_b)


def _attn_body(lam_ref, q_ref, k_ref, v_ref, sg_ref, o_ref, *, ctx_tiles, n_ctx, scale, out_scale):
    j = pl.program_id(2)
    q = q_ref[0]
    lane = lax.broadcasted_iota(I32, q.shape, 1)
    qs = q * scale
    q1 = jnp.where(lane < C_QK_DIM, qs, 0.0).astype(BF16)
    q2 = jnp.where(lane >= C_QK_DIM, qs, 0.0).astype(BF16)
    lam = lam_ref[0]
    nt = (((1,), (1,)), ((), ()))

    def attend(kb, vb):
        def softmax_times_v(qm):
            s = lax.dot_general(qm, kb, nt, preferred_element_type=F32)
            e = jnp.exp(s - jnp.max(s, axis=-1, keepdims=True))
            pv = jnp.dot(e.astype(BF16), vb, preferred_element_type=F32)
            return pv / jnp.sum(e, axis=-1, keepdims=True)

        o = softmax_times_v(q1) - lam * softmax_times_v(q2)
        o = o * lax.rsqrt(jnp.mean(o * o, axis=-1, keepdims=True) + RMS_EPS)
        o_ref[0] = o * sg_ref[...] * out_scale

    @pl.when(j < ctx_tiles)
    def _():
        attend(k_ref[0, :n_ctx], v_ref[0, :n_ctx])

    @pl.when(j >= ctx_tiles)
    def _():
        attend(k_ref[0], v_ref[0])


def diff_attention_core(q, k, v, lam, sub_g, n_ctx, lam_init):
    b, l, width = q.shape
    heads = width // C_V_DIM
    tq = _token_tile(n_ctx, l - n_ctx, 256)
    ctx_tiles = n_ctx // tq
    scale = C_QK_DIM ** -0.5
    assert math.frexp(scale)[0] == 0.5, "the kernel folds the score scale into q, exact only for a power of two"
    body = functools.partial(_attn_body, ctx_tiles=ctx_tiles, n_ctx=n_ctx, scale=scale, out_scale=1.0 - lam_init)
    return pl.pallas_call(
        body,
        grid=(b, heads, l // tq),
        in_specs=[
            pl.BlockSpec(memory_space=pltpu.SMEM),
            pl.BlockSpec((1, tq, C_V_DIM), lambda i, h, j: (i, j, h)),
            pl.BlockSpec((1, l, C_V_DIM), lambda i, h, j: (i, 0, h)),
            pl.BlockSpec((1, l, C_V_DIM), lambda i, h, j: (i, 0, h)),
            pl.BlockSpec((1, C_V_DIM), lambda i, h, j: (0, 0)),
        ],
        out_specs=pl.BlockSpec((1, tq, C_V_DIM), lambda i, h, j: (i, j, h)),
        out_shape=jax.ShapeDtypeStruct((b, l, width), F32),
        compiler_params=_cparams(("parallel", "parallel", "parallel")),
        name="diff_attention",
    )(lam, q, k, v, sub_g)


def _peer_candidates():
    return [(a, b) for a in range(PEER_TOPK) for b in range(PEER_TOPK) if (a + 1) * (b + 1) <= PEER_TOPK]


def _segment_rows(mod_ref, tile, tm, n_ctx):
    row = tile * tm + lax.broadcasted_iota(I32, (tm, 1), 0)
    return jnp.where(row < n_ctx, mod_ref[0, 0], mod_ref[0, 1])


def _peer_route_body(x_ref, sh_ref, sc_ref, wq_ref, keys_ref, h_ref, i_ref, j_ref, g_ref,
                     sv_ref, si_ref, cand_ref, best_ref, bi_ref, bj_ref, *, tm, n_ctx):
    tile = pl.program_id(1)
    h = _rms_modulate(x_ref[0], _segment_rows(sh_ref, tile, tm, n_ctx),
                      _segment_rows(sc_ref, tile, tm, n_ctx)).astype(BF16)
    h_ref[0] = h
    q = jnp.dot(h, wq_ref[...], preferred_element_type=F32).astype(BF16)
    nt = (((1,), (1,)), ((), ()))
    key_iota = lax.broadcasted_iota(I32, (N_KEYS, V7X_LANES), 0).astype(F32)
    neg = jnp.float32(-jnp.inf)

    for hd in range(PEER_HEADS):
        for m in range(2):
            c0 = (hd * 2 + m) * N_KEYS
            for col in range(tm // V7X_LANES):
                toks = slice(col * V7X_LANES, (col + 1) * V7X_LANES)
                s = lax.dot_general(keys_ref[m], q[toks, c0:c0 + N_KEYS], nt,
                                    preferred_element_type=F32)

                def extract(a, s, hd=hd, m=m, toks=toks):
                    mx = jnp.max(s, axis=0, keepdims=True)
                    idx = jnp.min(jnp.where(s == mx, key_iota, N_KEYS), axis=0, keepdims=True)
                    sv_ref[m, a, pl.ds(hd, 1), toks] = mx
                    si_ref[m, a, pl.ds(hd, 1), toks] = idx
                    return jnp.where(key_iota == idx, neg, s)

                lax.fori_loop(0, PEER_TOPK, extract, s)

    cands = _peer_candidates()
    for c, (a, b) in enumerate(cands):
        cand_ref[c] = sv_ref[0, a] + sv_ref[1, b]
    big = jnp.float32(PEER_TOPK * PEER_TOPK)

    def pick(k, carry):
        vals = [cand_ref[c] for c in range(len(cands))]
        mx = functools.reduce(jnp.maximum, vals)
        pos = functools.reduce(
            jnp.minimum,
            [jnp.where(vals[c] == mx, jnp.float32(a * PEER_TOPK + b), big) for c, (a, b) in enumerate(cands)])
        ii = jnp.zeros_like(pos)
        jj = jnp.zeros_like(pos)
        for c, (a, b) in enumerate(cands):
            hit = pos == jnp.float32(a * PEER_TOPK + b)
            cand_ref[c] = jnp.where(hit, neg, vals[c])
            ii = jnp.where(hit, si_ref[0, a], ii)
            jj = jnp.where(hit, si_ref[1, b], jj)
        best_ref[k] = mx
        bi_ref[k] = ii
        bj_ref[k] = jj
        return carry

    lax.fori_loop(0, PEER_TOPK, pick, 0)
    best = best_ref[...]
    e = jnp.exp(best - best[0][None])
    g = e / jnp.sum(e, axis=0, keepdims=True)
    g_ref[0] = g.reshape(PEER_TOPK * PEER_HEADS, tm)
    i_ref[0] = bi_ref[...].reshape(PEER_TOPK * PEER_HEADS, tm)
    j_ref[0] = bj_ref[...].reshape(PEER_TOPK * PEER_HEADS, tm)


PEER_TILE_TARGET = 512


def _peer_tile(l):
    t = (PEER_TILE_TARGET // V7X_LANES) * V7X_LANES
    while l % t:
        t -= V7X_LANES
    return t


def _mod_pair_spec(d):
    return pl.BlockSpec((1, 2, 1, d), lambda b, *_: (b, 0, 0, 0))


def peer_route(xs, shift, scale, w_q, keys, n_ctx):
    b, l, d = xs.shape
    tm = _peer_tile(l)
    nsel = PEER_TOPK * PEER_HEADS
    ncand = len(_peer_candidates())
    sel_spec = pl.BlockSpec((1, nsel, tm), lambda i, j: (i, 0, j))
    return pl.pallas_call(
        functools.partial(_peer_route_body, tm=tm, n_ctx=n_ctx),
        grid=(b, l // tm),
        in_specs=[
            pl.BlockSpec((1, tm, d), lambda i, j: (i, j, 0)),
            _mod_pair_spec(d),
            _mod_pair_spec(d),
            pl.BlockSpec(w_q.shape, lambda i, j: (0, 0)),
            pl.BlockSpec(keys.shape, lambda i, j: (0, 0, 0)),
        ],
        out_specs=[pl.BlockSpec((1, tm, d), lambda i, j: (i, j, 0)), sel_spec, sel_spec, sel_spec],
        out_shape=[
            jax.ShapeDtypeStruct((b, l, d), BF16),
            jax.ShapeDtypeStruct((b, nsel, l), F32),
            jax.ShapeDtypeStruct((b, nsel, l), F32),
            jax.ShapeDtypeStruct((b, nsel, l), F32),
        ],
        scratch_shapes=[
            pltpu.VMEM((2, PEER_TOPK, PEER_HEADS, tm), F32),
            pltpu.VMEM((2, PEER_TOPK, PEER_HEADS, tm), F32),
            pltpu.VMEM((ncand, PEER_HEADS, tm), F32),
            pltpu.VMEM((PEER_TOPK, PEER_HEADS, tm), F32),
            pltpu.VMEM((PEER_TOPK, PEER_HEADS, tm), F32),
            pltpu.VMEM((PEER_TOPK, PEER_HEADS, tm), F32),
        ],
        compiler_params=_cparams(("parallel", "parallel")),
        name="peer_route",
    )(xs, shift, scale, w_q.astype(BF16), keys.astype(BF16))


PEER_G_CHUNK = 32
PEER_PAIR = 2 * N_KEYS
PEER_EXPERT_BLOCK = 2048


def _peer_expert_body(h_ref, i_ref, j_ref, g_ref, u_ref, v_ref, x_ref, gate_ref, o_ref, gw_ref, acc_ref,
                      *, tm, eb, n_ctx):
    e = pl.program_id(2)
    nt = (((1,), (1,)), ((), ()))

    @pl.when(e == 0)
    def _():
        acc_ref[...] = jnp.zeros_like(acc_ref)
        isel = i_ref[0].T
        jsel = j_ref[0].T
        gsel = g_ref[0].T
        kio = lax.broadcasted_iota(I32, (PEER_G_CHUNK, N_KEYS, isel.shape[1]), 1).astype(F32)
        for c in range(tm // PEER_G_CHUNK):
            rows = slice(c * PEER_G_CHUNK, (c + 1) * PEER_G_CHUNK)
            oh_i = jnp.where(isel[rows][:, None, :] == kio, 1.0, 0.0).astype(BF16)
            oh_j = jnp.where(jsel[rows][:, None, :] == kio, gsel[rows][:, None, :], 0.0).astype(BF16)
            g3 = jnp.einsum("pis,pjs->pij", oh_i, oh_j, preferred_element_type=F32)
            gw_ref[:, rows, :] = jnp.swapaxes(g3, 0, 1).astype(BF16)

    h = h_ref[0]
    for ip in range(eb // PEER_PAIR):
        rows = slice(ip * PEER_PAIR, (ip + 1) * PEER_PAIR)
        act = jax.nn.gelu(lax.dot_general(h, u_ref[rows, :], nt, preferred_element_type=F32))
        i0 = e * (eb // N_KEYS) + ip * 2
        gw = jnp.concatenate([gw_ref[i0], gw_ref[i0 + 1]], axis=-1)
        acc_ref[...] += jnp.dot(act.astype(BF16) * gw, v_ref[rows, :], preferred_element_type=F32)

    @pl.when(e == pl.num_programs(2) - 1)
    def _():
        gate = _segment_rows(gate_ref, pl.program_id(1), tm, n_ctx)
        o_ref[0] = x_ref[0] + gate * acc_ref[...]


def peer_experts(h, isel, jsel, gsel, u_tab, v_tab, xs, gate, n_ctx):
    b, l, d = xs.shape
    ne = u_tab.shape[0]
    tm = _peer_tile(l)
    eb = PEER_EXPERT_BLOCK
    nsel = isel.shape[1]
    sel_spec = pl.BlockSpec((1, nsel, tm), lambda i, j, e: (i, 0, j))
    tok_spec = pl.BlockSpec((1, tm, d), lambda i, j, e: (i, j, 0))
    return pl.pallas_call(
        functools.partial(_peer_expert_body, tm=tm, eb=eb, n_ctx=n_ctx),
        grid=(b, l // tm, ne // eb),
        in_specs=[
            tok_spec, sel_spec, sel_spec, sel_spec,
            pl.BlockSpec((eb, d), lambda i, j, e: (e, 0)),
            pl.BlockSpec((eb, d), lambda i, j, e: (e, 0)),
            tok_spec,
            _mod_pair_spec(d),
        ],
        out_specs=tok_spec,
        out_shape=jax.ShapeDtypeStruct((b, l, d), F32),
        scratch_shapes=[
            pltpu.VMEM((N_KEYS, tm, N_KEYS), BF16),
            pltpu.VMEM((tm, d), F32),
        ],
        compiler_params=_cparams(("parallel", "parallel", "arbitrary")),
        name="peer_experts",
    )(h, isel, jsel, gsel, u_tab, v_tab, xs, gate)


def _per_segment(f, z, n_ctx):
    return jnp.concatenate([f(z[:, :n_ctx]), f(z[:, n_ctx:])], axis=1)


def _centred_shift(p):
    pp = jnp.pad(p, ((0, 0), (1, 1), (0, 0)))
    return 0.5 * (pp[:, :-2] + pp[:, 2:])


def _rwkv_chain_layout(z, heads):
    b, l, width = z.shape
    n = width // heads
    z = z.reshape(b, l, heads, n).transpose(1, 3, 0, 2).reshape(l, n, b * heads)
    z = jnp.pad(z, ((0, 0), (0, 0), (0, RWKV_CHAINS - b * heads)))
    return z.reshape(l, n, V7X_SUBLANES // RWKV_KPAR, V7X_LANES)


def _rwkv_keyed(z, heads):
    z = _rwkv_chain_layout(z, heads)
    l, n = z.shape[:2]
    return z.reshape(l, n // RWKV_KPAR, V7X_SUBLANES, V7X_LANES)


def _rwkv_valued(z, heads):
    z = _rwkv_chain_layout(z, heads)
    return jnp.concatenate([z] * RWKV_KPAR, axis=2)


def _rwkv_unchain(o, b, heads):
    l, n = o.shape[:2]
    o = o[:, :, :V7X_SUBLANES // RWKV_KPAR].reshape(l, n, RWKV_CHAINS)[:, :, :b * heads]
    return o.reshape(l, n, b, heads).transpose(2, 0, 3, 1).reshape(b, l, heads * n)


def _rwkv_mix(pa, n_ctx, w0, w_up, a0, a_up, g_up, k_k, k_a, r_k, ln_w, ln_b):
    b, l, _ = pa.shape
    width = w0.shape[-1]
    heads = width // A_HEAD_DIM
    assert b * heads <= RWKV_CHAINS
    cuts = (width, 2 * width, 3 * width, 3 * width + DECAY_RANK, 3 * width + 2 * DECAY_RANK,
            3 * width + 2 * DECAY_RANK + ICLR_RANK, 3 * width + 2 * DECAY_RANK + 2 * ICLR_RANK)
    r, k, v, wd_f, wd_b, ad_f, ad_b, gd = jnp.split(pa, cuts, axis=-1)

    def heads_of(t):
        return t.reshape(b, l, heads, A_HEAD_DIM)

    kk = heads_of(k * k_k)
    kk = kk / jnp.maximum(jnp.sqrt(jnp.sum(kk * kk, axis=-1, keepdims=True)), 1e-12)
    kk = kk.reshape(b, l, width)

    def direction(d, wd, ad):
        decay = jnp.exp(-DECAY_SCALE * jax.nn.sigmoid(w0[d] + jnp.tanh(wd) @ w_up[d]))
        a = jax.nn.sigmoid(a0[d] + ad @ a_up[d])
        kt = k * (1.0 + (a - 1.0) * k_a)
        return decay, kt, a * kk

    dec_f, kt_f, akk_f = direction(0, wd_f, ad_f)
    dec_b, kt_b, akk_b = direction(1, wd_b, ad_b)
    o_f, o_b = rwkv_scan(
        _rwkv_keyed(r, heads), _rwkv_keyed(dec_f, heads), _rwkv_keyed(dec_b, heads),
        _rwkv_keyed(kt_f, heads), _rwkv_keyed(kt_b, heads), _rwkv_keyed(kk, heads),
        _rwkv_keyed(akk_f, heads), _rwkv_keyed(akk_b, heads), _rwkv_valued(v, heads), n_ctx)
    o = heads_of(_rwkv_unchain(o_f, b, heads) + _rwkv_unchain(o_b, b, heads))
    mean = jnp.mean(o, axis=-1, keepdims=True)
    var = jnp.mean(jnp.square(o - mean), axis=-1, keepdims=True)
    on = ((o - mean) * lax.rsqrt(var + GN_EPS)).reshape(b, l, width) * ln_w + ln_b
    bonus = jnp.sum(heads_of(r) * heads_of(0.5 * (kt_f + kt_b)) * r_k, axis=-1, keepdims=True) * heads_of(v)
    g = jax.nn.sigmoid(gd) @ g_up
    return (on + bonus.reshape(b, l, width)) * g


def _depthwise_conv(s, w, bias):
    t_len = s.shape[1]
    left = CONV_W // 2
    sp = jnp.pad(s, ((0, 0), (left, CONV_W - 1 - left), (0, 0)))
    out = sp[:, 0:t_len] * w[0]
    for j in range(1, CONV_W):
        out = out + sp[:, j:j + t_len] * w[j]
    return out + bias


def _rglru_mix(pb, n_ctx, conv_w, conv_b, wa, ba, wx, bx, lam):
    b, l, _ = pb.shape
    width = conv_b.shape[-1]
    xb = _per_segment(lambda s: _depthwise_conv(s, conv_w, conv_b), pb[..., :width], n_ctx)
    gate = jax.nn.gelu(pb[..., width:])
    xblk = xb.reshape(b, l, width // B_BLOCK_DIM, B_BLOCK_DIM)

    def direction(d):
        rg = jax.nn.sigmoid(jnp.einsum('blnc,ncd->blnd', xblk, wa[d]).reshape(b, l, width) + ba[d])
        ig = jax.nn.sigmoid(jnp.einsum('blnc,ncd->blnd', xblk, wx[d]).reshape(b, l, width) + bx[d])
        log_a = -LRU_C * rg * jax.nn.softplus(-lam[d])
        u = jnp.sqrt(-jnp.expm1(2.0 * log_a)) * ig * xb
        return jnp.exp(log_a), u

    a_f, u_f = direction(0)
    a_b, u_b = direction(1)

    def chains(z):
        return z.transpose(1, 0, 2).reshape(l, b * width // V7X_LANES, V7X_LANES)

    h_f, h_b = diag_scan(chains(a_f), chains(u_f), chains(a_b), chains(u_b), n_ctx)
    h = (h_f + h_b).reshape(l, b, width).transpose(1, 0, 2)
    return h * gate


def _rms(x):
    return x * lax.rsqrt(jnp.mean(x * x, axis=-1, keepdims=True) + RMS_EPS)


def _rope_1d(x, pos):
    half = x.shape[-1] // 2
    inv_freq = ROPE_BASE ** (-jnp.arange(half, dtype=F32) / half)
    ang = pos.astype(F32)[:, None] * inv_freq
    cos = jnp.cos(ang)[:, None, None, :]
    sin = jnp.sin(ang)[:, None, None, :]
    x1, x2 = x[..., :half], x[..., half:]
    return jnp.concatenate([x1 * cos - x2 * sin, x1 * sin + x2 * cos], axis=-1)


def _rope_2d(x, row_pos, col_pos):
    r = x.shape[-1] // 2
    return jnp.concatenate([_rope_1d(x[..., :r], row_pos), _rope_1d(x[..., r:], col_pos)], axis=-1)


def _diff_attention(qkv, n_ctx, lam_init, q_g, k_g, lq1, lk1, lq2, lk2, sub_g):
    b, l, width3 = qkv.shape
    width = width3 // 3
    heads = width // C_V_DIM
    n_lat = l - n_ctx
    n_rows = n_lat // GRID_W
    row_pos = jnp.repeat(jnp.arange(n_rows), GRID_W)
    col_pos = jnp.tile(jnp.arange(GRID_W), n_rows)
    q = _rms(qkv[..., :width].reshape(b, l, heads, 2, C_QK_DIM)) * q_g
    k = _rms(qkv[..., width:2 * width].reshape(b, l, heads, 2, C_QK_DIM)) * k_g
    v = qkv[..., 2 * width:]
    q = jnp.concatenate([q[:, :n_ctx], _rope_2d(q[:, n_ctx:], row_pos, col_pos)], axis=1)
    k = jnp.concatenate([k[:, :n_ctx], _rope_2d(k[:, n_ctx:], row_pos, col_pos)], axis=1)
    lam = jnp.exp(jnp.sum(lq1 * lk1)) - jnp.exp(jnp.sum(lq2 * lk2)) + lam_init
    return diff_attention_core(q.reshape(b, l, width), k.reshape(b, l, width).astype(BF16), v.astype(BF16),
                               lam.reshape(1), sub_g.reshape(1, C_V_DIM), n_ctx, lam_init)


def kernel(x, c, ctx, c_ctx, w_mod, b_mod, even_w_in, even_mu, even_w_out, rwkv_w0, rwkv_w_up, rwkv_a0, rwkv_a_up, rwkv_g_up, rwkv_k_k, rwkv_k_a, rwkv_r_k, rwkv_ln_w, rwkv_ln_b, lru_conv_w, lru_conv_b, lru_wa, lru_ba, lru_wx, lru_bx, lru_lam, attn_w_qkv, attn_q_g, attn_k_g, attn_lam_q1, attn_lam_k1, attn_lam_q2, attn_lam_k2, attn_sub_g, attn_w_o, peer_w_q, peer_keys, peer_u, peer_v):
    bsz, n_lat, d = x.shape
    n_ctx = ctx.shape[1]
    depth = w_mod.shape[0]
    a_proj = even_mu.shape[-1]
    s_lat = jax.nn.silu(c)
    s_ctx = jax.nn.silu(c_ctx)
    xs = jnp.concatenate([ctx, x], axis=1)
    for layer in range(depth):
        i = layer // 2
        m_l = s_lat @ w_mod[layer] + b_mod[layer]
        m_c = s_ctx @ w_mod[layer] + b_mod[layer]
        mods = jnp.stack([jnp.broadcast_to(m_c, m_l.shape), m_l], axis=1).reshape(bsz, 2, N_MOD, 1, d)
        shift1, scale1, gate1, shift2, scale2, gate2 = (mods[:, :, n] for n in range(N_MOD))
        if layer % 2 == 0:
            p = modulate_matmul(xs, shift1, scale1, even_w_in[i], n_ctx)
            pa, pb = p[..., :a_proj], p[..., a_proj:]
            pa = pa + even_mu[i] * (_per_segment(_centred_shift, pa, n_ctx) - pa)
            ya = _rwkv_mix(pa, n_ctx, rwkv_w0[i], rwkv_w_up[i], rwkv_a0[i], rwkv_a_up[i], rwkv_g_up[i],
                           rwkv_k_k[i], rwkv_k_a[i], rwkv_r_k[i], rwkv_ln_w[i], rwkv_ln_b[i])
            yb = _rglru_mix(pb, n_ctx, lru_conv_w[i], lru_conv_b[i], lru_wa[i], lru_ba[i], lru_wx[i],
                            lru_bx[i], lru_lam[i])
            xs = gated_out_proj(jnp.concatenate([ya, yb], axis=-1), xs, gate1, even_w_out[i], n_ctx)
        else:
            lam_init = 0.8 - 0.6 * math.exp(-0.3 * layer)
            qkv = modulate_matmul(xs, shift1, scale1, attn_w_qkv[i], n_ctx)
            o = _diff_attention(qkv, n_ctx, lam_init, attn_q_g[i], attn_k_g[i], attn_lam_q1[i],
                                attn_lam_k1[i], attn_lam_q2[i], attn_lam_k2[i], attn_sub_g[i])
            xs = gated_out_proj(o, xs, gate1, attn_w_o[i], n_ctx)
        h, isel, jsel, gsel = peer_route(xs, shift2, scale2, peer_w_q[layer], peer_keys[layer], n_ctx)
        xs = peer_experts(h, isel, jsel, gsel, peer_u[layer].astype(BF16), peer_v[layer].astype(BF16), xs,
                          gate2, n_ctx)
    return xs[:, n_ctx:]
```

```python
import functools
import math

import jax
import jax.numpy as jnp
from jax import lax
from jax.experimental import pallas as pl
from jax.experimental.pallas import tpu as pltpu

F32 = jnp.float32
BF16 = jnp.bfloat16
I32 = jnp.int32

V7X_LANES = 128
V7X_SUBLANES = 8
V7X_VMEM_LIMIT_BYTES = 56 * 1024 * 1024

GRID_W = 64
N_MOD = 6
A_HEAD_DIM = 64
DECAY_SCALE = math.exp(-0.5)
GN_EPS = 64e-5
DECAY_RANK = 64
ICLR_RANK = 64
GATE_RANK = 128
B_BLOCK_DIM = 64
CONV_W = 4
LRU_C = 8.0
C_QK_DIM = 64
C_V_DIM = 128
ROPE_BASE = 10000.0
N_KEYS = 128
PEER_HEADS = 8
PEER_TOPK = 16
RMS_EPS = 1e-6


def _cparams(sem):
    return pltpu.CompilerParams(dimension_semantics=sem, vmem_limit_bytes=V7X_VMEM_LIMIT_BYTES)


def _token_tile(n_ctx, n_lat, want):
    t = want
    while n_ctx % t or n_lat % t:
        t //= 2
    return t


def _mod_spec(d, ctx_tiles):
    return pl.BlockSpec((1, 1, 1, d), lambda b, j, *_: (b, jnp.where(j >= ctx_tiles, 1, 0), 0, 0))


def _rms_modulate(x, shift, scale):
    xn = x * lax.rsqrt(jnp.mean(x * x, axis=-1, keepdims=True) + RMS_EPS)
    return xn * (1.0 + scale) + shift


def _modmm_body(x_ref, sh_ref, sc_ref, w_ref, o_ref):
    z = _rms_modulate(x_ref[0], sh_ref[0, 0], sc_ref[0, 0])
    o_ref[0] = jnp.dot(z.astype(BF16), w_ref[...], preferred_element_type=F32)


def modulate_matmul(xs, shift, scale, w, n_ctx):
    b, l, d = xs.shape
    n = w.shape[1]
    tm = _token_tile(n_ctx, l - n_ctx, 256)
    ctx_tiles = n_ctx // tm
    return pl.pallas_call(
        _modmm_body,
        grid=(b, l // tm),
        in_specs=[
            pl.BlockSpec((1, tm, d), lambda i, j: (i, j, 0)),
            _mod_spec(d, ctx_tiles),
            _mod_spec(d, ctx_tiles),
            pl.BlockSpec((d, n), lambda i, j: (0, 0)),
        ],
        out_specs=pl.BlockSpec((1, tm, n), lambda i, j: (i, j, 0)),
        out_shape=jax.ShapeDtypeStruct((b, l, n), F32),
        compiler_params=_cparams(("parallel", "parallel")),
        name="modulate_matmul",
    )(xs, shift, scale, w.astype(BF16))


def _outproj_body(y_ref, x_ref, g_ref, w_ref, o_ref):
    acc = jnp.dot(y_ref[0].astype(BF16), w_ref[...], preferred_element_type=F32)
    o_ref[0] = x_ref[0] + g_ref[0, 0] * acc


def gated_out_proj(y, xs, gate, w, n_ctx):
    b, l, d = xs.shape
    k = y.shape[-1]
    tm = _token_tile(n_ctx, l - n_ctx, 256)
    ctx_tiles = n_ctx // tm
    return pl.pallas_call(
        _outproj_body,
        grid=(b, l // tm),
        in_specs=[
            pl.BlockSpec((1, tm, k), lambda i, j: (i, j, 0)),
            pl.BlockSpec((1, tm, d), lambda i, j: (i, j, 0)),
            _mod_spec(d, ctx_tiles),
            pl.BlockSpec((k, d), lambda i, j: (0, 0)),
        ],
        out_specs=pl.BlockSpec((1, tm, d), lambda i, j: (i, j, 0)),
        out_shape=jax.ShapeDtypeStruct((b, l, d), F32),
        compiler_params=_cparams(("parallel", "parallel")),
        name="gated_out_proj",
    )(y, xs, gate, w.astype(BF16))


RWKV_VBLK_A = 16
RWKV_VBLK_B = 8
RWKV_KPAR = 4
RWKV_CHAINS = (V7X_SUBLANES // RWKV_KPAR) * V7X_LANES
SCAN_TB = 8


def _reversed_block_map(n_blocks, ctx_blocks):
    def rev(i):
        return jnp.where(i < ctx_blocks, ctx_blocks - 1 - i, n_blocks - 1 - i + ctx_blocks)
    return rev


def _rwkv_scan_body(rf_ref, wf_ref, kf_ref, kkf_ref, akkf_ref, vf_ref,
                    rb_ref, wb_ref, kb_ref, kkb_ref, akkb_ref, vb_ref,
                    of_ref, ob_ref, tf_ref, tb_ref, sa_ref, *, tb, nk4, nv):
    @pl.when(pl.program_id(0) == 0)
    def _():
        tf_ref[...] = jnp.zeros_like(tf_ref)
        tb_ref[...] = jnp.zeros_like(tb_ref)

    def fold(acc):
        acc = acc + pltpu.roll(acc, V7X_SUBLANES // 2, axis=1)
        return acc + pltpu.roll(acc, V7X_SUBLANES // 4, axis=1)

    def one_direction(t, r_ref, w_ref, k_ref, kk_ref, akk_ref, v_ref, o_ref, t_ref):
        for vb in range(nv // RWKV_VBLK_A):
            vs = pl.ds(vb * RWKV_VBLK_A, RWKV_VBLK_A)

            def acc_sa(k4, acc):
                return acc + t_ref[k4, vs] * kk_ref[t, k4][None]

            acc = lax.fori_loop(0, nk4, acc_sa, jnp.zeros((RWKV_VBLK_A, V7X_SUBLANES, V7X_LANES), F32), unroll=4)
            sa_ref[vs] = fold(acc)
        for vb in range(nv // RWKV_VBLK_B):
            vs = pl.ds(vb * RWKV_VBLK_B, RWKV_VBLK_B)
            sa = sa_ref[vs]
            vv = v_ref[t, vs]

            def upd(k4, acc):
                tn = (t_ref[k4, vs] * w_ref[t, k4][None] - akk_ref[t, k4][None] * sa
                      + k_ref[t, k4][None] * vv)
                t_ref[k4, vs] = tn
                return acc + tn * r_ref[t, k4][None]

            acc = lax.fori_loop(0, nk4, upd, jnp.zeros((RWKV_VBLK_B, V7X_SUBLANES, V7X_LANES), F32), unroll=4)
            o_ref[t, vs] = fold(acc)

    def step(t, carry):
        one_direction(t, rf_ref, wf_ref, kf_ref, kkf_ref, akkf_ref, vf_ref, of_ref, tf_ref)
        one_direction(tb - 1 - t, rb_ref, wb_ref, kb_ref, kkb_ref, akkb_ref, vb_ref, ob_ref, tb_ref)
        return carry

    lax.fori_loop(0, tb, step, 0)


def rwkv_scan(r, w_f, w_b, k_f, k_b, kk, akk_f, akk_b, v, n_ctx):
    l, nk4 = r.shape[0], r.shape[1]
    nv = v.shape[1]
    tb = SCAN_TB
    rev = _reversed_block_map(l // tb, n_ctx // tb)
    kblk = (tb, nk4, V7X_SUBLANES, V7X_LANES)
    vblk = (tb, nv, V7X_SUBLANES, V7X_LANES)
    kf = pl.BlockSpec(kblk, lambda i: (i, 0, 0, 0))
    kb = pl.BlockSpec(kblk, lambda i: (rev(i), 0, 0, 0))
    vf = pl.BlockSpec(vblk, lambda i: (i, 0, 0, 0))
    vb = pl.BlockSpec(vblk, lambda i: (rev(i), 0, 0, 0))
    out = jax.ShapeDtypeStruct((l, nv, V7X_SUBLANES, V7X_LANES), F32)
    state = pltpu.VMEM((nk4, nv, V7X_SUBLANES, V7X_LANES), F32)
    return pl.pallas_call(
        functools.partial(_rwkv_scan_body, tb=tb, nk4=nk4, nv=nv),
        grid=(l // tb,),
        in_specs=[kf, kf, kf, kf, kf, vf, kb, kb, kb, kb, kb, vb],
        out_specs=[vf, vb],
        out_shape=[out, out],
        scratch_shapes=[state, state, pltpu.VMEM((nv, V7X_SUBLANES, V7X_LANES), F32)],
        compiler_params=_cparams(("arbitrary",)),
        name="rwkv_scan",
    )(r, w_f, k_f, kk, akk_f, v, r, w_b, k_b, kk, akk_b, v)


def _diag_scan_body(af_ref, uf_ref, ab_ref, ub_ref, of_ref, ob_ref, hf_ref, hb_ref, *, tb):
    @pl.when(pl.program_id(0) == 0)
    def _():
        hf_ref[...] = jnp.zeros_like(hf_ref)
        hb_ref[...] = jnp.zeros_like(hb_ref)

    def step(t, carry):
        hf, hb = carry
        hf = af_ref[:, t, :] * hf + uf_ref[:, t, :]
        of_ref[:, t, :] = hf
        tr = tb - 1 - t
        hb = ab_ref[:, tr, :] * hb + ub_ref[:, tr, :]
        ob_ref[:, tr, :] = hb
        return hf, hb

    hf, hb = lax.fori_loop(0, tb, step, (hf_ref[...], hb_ref[...]))
    hf_ref[...] = hf
    hb_ref[...] = hb


def diag_scan(a_f, u_f, a_b, u_b, n_ctx):
    b, l, c = a_f.shape
    tb = SCAN_TB
    rev = _reversed_block_map(l // tb, n_ctx // tb)
    fwd = pl.BlockSpec((b, tb, c), lambda i: (0, i, 0))
    bwd = pl.BlockSpec((b, tb, c), lambda i: (0, rev(i), 0))
    out = jax.ShapeDtypeStruct((b, l, c), F32)
    return pl.pallas_call(
        functools.partial(_diag_scan_body, tb=tb),
        grid=(l // tb,),
        in_specs=[fwd, fwd, bwd, bwd],
        out_specs=[fwd, bwd],
        out_shape=[out, out],
        scratch_shapes=[pltpu.VMEM((b, c), F32), pltpu.VMEM((b, c), F32)],
        compiler_params=_cparams(("arbitrary",)),
        name="diag_scan",
    )(a_f, u_f, a_b, u_b)


def _attn_body(lam_ref, q_ref, k_ref, v_ref, sg_ref, o_ref, *, ctx_tiles, n_ctx, scale, out_scale):
    j = pl.program_id(2)
    q = q_ref[0]
    lane = lax.broadcasted_iota(I32, q.shape, 1)
    qs = q * jnp.asarray(scale, BF16)
    q1 = jnp.where(lane < C_QK_DIM, qs, jnp.zeros_like(qs))
    q2 = jnp.where(lane >= C_QK_DIM, qs, jnp.zeros_like(qs))
    lam = lam_ref[0]
    nt = (((1,), (1,)), ((), ()))

    def attend(kb, vb):
        def softmax_times_v(qm):
            s = lax.dot_general(qm, kb, nt, preferred_element_type=F32)
            e = jnp.exp(s - jnp.max(s, axis=-1, keepdims=True))
            pv = jnp.dot(e.astype(BF16), vb, preferred_element_type=F32)
            return pv / jnp.sum(e, axis=-1, keepdims=True)

        o = softmax_times_v(q1) - lam * softmax_times_v(q2)
        o = o * lax.rsqrt(jnp.mean(o * o, axis=-1, keepdims=True) + RMS_EPS)
        o_ref[0] = o * sg_ref[...] * out_scale

    @pl.when(j < ctx_tiles)
    def _():
        attend(k_ref[0, :n_ctx], v_ref[0, :n_ctx])

    @pl.when(j >= ctx_tiles)
    def _():
        attend(k_ref[0], v_ref[0])


def diff_attention_core(q, k, v, lam, sub_g, n_ctx, lam_init):
    b, l, width = q.shape
    heads = width // C_V_DIM
    tq = _token_tile(n_ctx, l - n_ctx, 256)
    ctx_tiles = n_ctx // tq
    scale = C_QK_DIM ** -0.5
    assert math.frexp(scale)[0] == 0.5, "the kernel folds the score scale into q, exact only for a power of two"
    body = functools.partial(_attn_body, ctx_tiles=ctx_tiles, n_ctx=n_ctx, scale=scale, out_scale=1.0 - lam_init)
    return pl.pallas_call(
        body,
        grid=(b, heads, l // tq),
        in_specs=[
            pl.BlockSpec(memory_space=pltpu.SMEM),
            pl.BlockSpec((1, tq, C_V_DIM), lambda i, h, j: (i, j, h)),
            pl.BlockSpec((1, l, C_V_DIM), lambda i, h, j: (i, 0, h)),
            pl.BlockSpec((1, l, C_V_DIM), lambda i, h, j: (i, 0, h)),
            pl.BlockSpec((1, C_V_DIM), lambda i, h, j: (0, 0)),
        ],
        out_specs=pl.BlockSpec((1, tq, C_V_DIM), lambda i, h, j: (i, j, h)),
        out_shape=jax.ShapeDtypeStruct((b, l, width), F32),
        compiler_params=_cparams(("parallel", "parallel", "parallel")),
        name="diff_attention",
    )(lam, q, k, v, sub_g)


def _peer_candidates():
    return [(a, b) for a in range(PEER_TOPK) for b in range(PEER_TOPK) if (a + 1) * (b + 1) <= PEER_TOPK]


def _segment_rows(mod_ref, tile, tm, n_ctx):
    row = tile * tm + lax.broadcasted_iota(I32, (tm, 1), 0)
    return jnp.where(row < n_ctx, mod_ref[0, 0], mod_ref[0, 1])


def _peer_route_body(x_ref, sh_ref, sc_ref, wq_ref, keys_ref, h_ref, i_ref, j_ref, g_ref,
                     sv_ref, si_ref, cand_ref, best_ref, bi_ref, bj_ref, *, tm, n_ctx):
    tile = pl.program_id(1)
    h = _rms_modulate(x_ref[0], _segment_rows(sh_ref, tile, tm, n_ctx),
                      _segment_rows(sc_ref, tile, tm, n_ctx)).astype(BF16)
    h_ref[0] = h
    q = jnp.dot(h, wq_ref[...], preferred_element_type=F32).astype(BF16)
    nt = (((1,), (1,)), ((), ()))
    key_iota = lax.broadcasted_iota(I32, (N_KEYS, V7X_LANES), 0).astype(F32)
    neg = jnp.float32(-jnp.inf)

    for hd in range(PEER_HEADS):
        for col in range(tm // V7X_LANES):
            toks = slice(col * V7X_LANES, (col + 1) * V7X_LANES)
            scores = tuple(
                lax.dot_general(keys_ref[m], q[toks, (hd * 2 + m) * N_KEYS:(hd * 2 + m + 1) * N_KEYS], nt,
                                preferred_element_type=F32) for m in range(2))

            def extract(a, ss, hd=hd, toks=toks):
                out = []
                for m, s in enumerate(ss):
                    mx = jnp.max(s, axis=0, keepdims=True)
                    idx = jnp.min(jnp.where(s == mx, key_iota, N_KEYS), axis=0, keepdims=True)
                    sv_ref[m, a, pl.ds(hd, 1), toks] = mx
                    si_ref[m, a, pl.ds(hd, 1), toks] = idx
                    out.append(jnp.where(key_iota == idx, neg, s))
                return tuple(out)

            lax.fori_loop(0, PEER_TOPK, extract, scores)

    cands = _peer_candidates()
    for c, (a, b) in enumerate(cands):
        cand_ref[c] = sv_ref[0, a] + sv_ref[1, b]
    big = jnp.float32(PEER_TOPK * PEER_TOPK)

    def pick(k, carry):
        vals = [cand_ref[c] for c in range(len(cands))]
        mx = functools.reduce(jnp.maximum, vals)
        pos = functools.reduce(
            jnp.minimum,
            [jnp.where(vals[c] == mx, jnp.float32(a * PEER_TOPK + b), big) for c, (a, b) in enumerate(cands)])
        ii = jnp.zeros_like(pos)
        jj = jnp.zeros_like(pos)
        for c, (a, b) in enumerate(cands):
            hit = pos == jnp.float32(a * PEER_TOPK + b)
            cand_ref[c] = jnp.where(hit, neg, vals[c])
            ii = jnp.where(hit, si_ref[0, a], ii)
            jj = jnp.where(hit, si_ref[1, b], jj)
        best_ref[k] = mx
        bi_ref[k] = ii
        bj_ref[k] = jj
        return carry

    lax.fori_loop(0, PEER_TOPK, pick, 0)
    best = best_ref[...]
    e = jnp.exp(best - best[0][None])
    g = e / jnp.sum(e, axis=0, keepdims=True)
    g_ref[0] = g.reshape(PEER_TOPK * PEER_HEADS, tm)
    i_ref[0] = bi_ref[...].reshape(PEER_TOPK * PEER_HEADS, tm)
    j_ref[0] = bj_ref[...].reshape(PEER_TOPK * PEER_HEADS, tm)


PEER_TILE_TARGET = 512


def _peer_tile(l):
    t = (PEER_TILE_TARGET // V7X_LANES) * V7X_LANES
    while l % t:
        t -= V7X_LANES
    return t


def _mod_pair_spec(d):
    return pl.BlockSpec((1, 2, 1, d), lambda b, *_: (b, 0, 0, 0))


def peer_route(xs, shift, scale, w_q, keys, n_ctx):
    b, l, d = xs.shape
    tm = _peer_tile(l)
    nsel = PEER_TOPK * PEER_HEADS
    ncand = len(_peer_candidates())
    sel_spec = pl.BlockSpec((1, nsel, tm), lambda i, j: (i, 0, j))
    return pl.pallas_call(
        functools.partial(_peer_route_body, tm=tm, n_ctx=n_ctx),
        grid=(b, l // tm),
        in_specs=[
            pl.BlockSpec((1, tm, d), lambda i, j: (i, j, 0)),
            _mod_pair_spec(d),
            _mod_pair_spec(d),
            pl.BlockSpec(w_q.shape, lambda i, j: (0, 0)),
            pl.BlockSpec(keys.shape, lambda i, j: (0, 0, 0)),
        ],
        out_specs=[pl.BlockSpec((1, tm, d), lambda i, j: (i, j, 0)), sel_spec, sel_spec, sel_spec],
        out_shape=[
            jax.ShapeDtypeStruct((b, l, d), BF16),
            jax.ShapeDtypeStruct((b, nsel, l), F32),
            jax.ShapeDtypeStruct((b, nsel, l), F32),
            jax.ShapeDtypeStruct((b, nsel, l), F32),
        ],
        scratch_shapes=[
            pltpu.VMEM((2, PEER_TOPK, PEER_HEADS, tm), F32),
            pltpu.VMEM((2, PEER_TOPK, PEER_HEADS, tm), F32),
            pltpu.VMEM((ncand, PEER_HEADS, tm), F32),
            pltpu.VMEM((PEER_TOPK, PEER_HEADS, tm), F32),
            pltpu.VMEM((PEER_TOPK, PEER_HEADS, tm), F32),
            pltpu.VMEM((PEER_TOPK, PEER_HEADS, tm), F32),
        ],
        compiler_params=_cparams(("parallel", "parallel")),
        name="peer_route",
    )(xs, shift, scale, w_q.astype(BF16), keys.astype(BF16))


PEER_G_CHUNK = 32
PEER_PAIR = 2 * N_KEYS
PEER_EXPERT_BLOCK = 2048


def _peer_expert_body(h_ref, i_ref, j_ref, g_ref, u_ref, v_ref, x_ref, gate_ref, o_ref, gw_ref, acc_ref,
                      *, tm, eb, n_ctx):
    e = pl.program_id(2)
    nt = (((1,), (1,)), ((), ()))

    @pl.when(e == 0)
    def _():
        acc_ref[...] = jnp.zeros_like(acc_ref)
        isel = i_ref[0].T
        jsel = j_ref[0].T
        gsel = g_ref[0].T
        kio = lax.broadcasted_iota(I32, (PEER_G_CHUNK, N_KEYS, isel.shape[1]), 1).astype(F32)
        for c in range(tm // PEER_G_CHUNK):
            rows = slice(c * PEER_G_CHUNK, (c + 1) * PEER_G_CHUNK)
            oh_i = jnp.where(isel[rows][:, None, :] == kio, 1.0, 0.0).astype(BF16)
            oh_j = jnp.where(jsel[rows][:, None, :] == kio, gsel[rows][:, None, :], 0.0).astype(BF16)
            g3 = jnp.einsum("pis,pjs->pij", oh_i, oh_j, preferred_element_type=F32)
            gw_ref[:, rows, :] = jnp.swapaxes(g3, 0, 1).astype(BF16)

    h = h_ref[0]
    for ip in range(eb // PEER_PAIR):
        rows = slice(ip * PEER_PAIR, (ip + 1) * PEER_PAIR)
        act = jax.nn.gelu(lax.dot_general(h, u_ref[rows, :], nt, preferred_element_type=F32))
        i0 = e * (eb // N_KEYS) + ip * 2
        gw = jnp.concatenate([gw_ref[i0], gw_ref[i0 + 1]], axis=-1)
        acc_ref[...] += jnp.dot(act.astype(BF16) * gw, v_ref[rows, :], preferred_element_type=F32)

    @pl.when(e == pl.num_programs(2) - 1)
    def _():
        gate = _segment_rows(gate_ref, pl.program_id(1), tm, n_ctx)
        o_ref[0] = x_ref[0] + gate * acc_ref[...]


def peer_experts(h, isel, jsel, gsel, u_tab, v_tab, xs, gate, n_ctx):
    b, l, d = xs.shape
    ne = u_tab.shape[0]
    tm = _peer_tile(l)
    eb = PEER_EXPERT_BLOCK
    nsel = isel.shape[1]
    sel_spec = pl.BlockSpec((1, nsel, tm), lambda i, j, e: (i, 0, j))
    tok_spec = pl.BlockSpec((1, tm, d), lambda i, j, e: (i, j, 0))
    return pl.pallas_call(
        functools.partial(_peer_expert_body, tm=tm, eb=eb, n_ctx=n_ctx),
        grid=(b, l // tm, ne // eb),
        in_specs=[
            tok_spec, sel_spec, sel_spec, sel_spec,
            pl.BlockSpec((eb, d), lambda i, j, e: (e, 0)),
            pl.BlockSpec((eb, d), lambda i, j, e: (e, 0)),
            tok_spec,
            _mod_pair_spec(d),
        ],
        out_specs=tok_spec,
        out_shape=jax.ShapeDtypeStruct((b, l, d), F32),
        scratch_shapes=[
            pltpu.VMEM((N_KEYS, tm, N_KEYS), BF16),
            pltpu.VMEM((tm, d), F32),
        ],
        compiler_params=_cparams(("parallel", "parallel", "arbitrary")),
        name="peer_experts",
    )(h, isel, jsel, gsel, u_tab, v_tab, xs, gate)


def _group_ones(width, group):
    g = jnp.arange(width) // group
    return (g[:, None] == g[None, :]).astype(F32)


def _group_sum(x, ones_ref):
    return jnp.dot(x, ones_ref[...], precision=lax.Precision.HIGHEST, preferred_element_type=F32)


def _pad_rank_rows(w, lo, total):
    return jnp.pad(w, ((lo, total - lo - w.shape[0]), (0, 0)))


HALO_ROWS = V7X_SUBLANES


def _even_prep_body(pm_ref, pp_ref, pn_ref, mu_ref, rvec_ref, wup_ref, aup_ref, gup_ref, ones_ref,
                    cw_ref, wa_ref, wx_ref, lvec_ref,
                    r_ref, decf_ref, decb_ref, ktf_ref, ktb_ref, kk_ref, akkf_ref, akkb_ref, v_ref,
                    bonus_ref, ga_ref, af_ref, uf_ref, ab_ref, ub_ref, gb_ref,
                    *, tm, ctx_tiles, n_tiles, width, a_proj):
    j = pl.program_id(1)
    first = jnp.logical_or(j == 0, j == ctx_tiles)
    last = jnp.logical_or(j == ctx_tiles - 1, j == n_tiles - 1)
    p = pm_ref[0]
    prev = jnp.where(first, 0.0, pp_ref[0])
    nxt = jnp.where(last, 0.0, pn_ref[0])
    row = lax.broadcasted_iota(I32, (tm, 1), 0)

    def shifted(lo, hi, off):
        x = p[:, lo:hi]
        if off < 0:
            y = pltpu.roll(x, -off, axis=0)
            for s in range(-off):
                y = jnp.where(row == s, prev[HALO_ROWS + off + s:HALO_ROWS + off + s + 1, lo:hi], y)
        else:
            y = pltpu.roll(x, tm - off, axis=0)
            for s in range(off):
                y = jnp.where(row == tm - off + s, nxt[s:s + 1, lo:hi], y)
        return y

    pa = p[:, :a_proj]
    pa = pa + mu_ref[...] * (0.5 * (shifted(0, a_proj, -1) + shifted(0, a_proj, 1)) - pa)
    r = pa[:, :width]
    k = pa[:, width:2 * width]
    v = pa[:, 2 * width:3 * width]
    c0 = 3 * width
    wd = jnp.tanh(pa[:, c0:c0 + V7X_LANES]).astype(BF16)
    ad = pa[:, c0 + V7X_LANES:c0 + 2 * V7X_LANES].astype(BF16)
    gd = jax.nn.sigmoid(pa[:, c0 + 2 * V7X_LANES:c0 + 3 * V7X_LANES]).astype(BF16)
    w0_f, w0_b, a0_f, a0_b, k_k, k_a, r_k = (rvec_ref[n:n + 1] for n in range(7))

    kk = k * k_k
    kk = kk / jnp.maximum(jnp.sqrt(_group_sum(kk * kk, ones_ref)), 1e-12)
    kts = []
    for w0, a0, d, dec_ref, kt_ref, akk_ref in ((w0_f, a0_f, 0, decf_ref, ktf_ref, akkf_ref),
                                                (w0_b, a0_b, 1, decb_ref, ktb_ref, akkb_ref)):
        lora_w = jnp.dot(wd, wup_ref[d], preferred_element_type=F32)
        dec_ref[0] = jnp.exp(-DECAY_SCALE * jax.nn.sigmoid(w0 + lora_w))
        a = jax.nn.sigmoid(a0 + jnp.dot(ad, aup_ref[d], preferred_element_type=F32))
        kt = k * (1.0 + (a - 1.0) * k_a)
        kt_ref[0] = kt
        akk_ref[0] = a * kk
        kts.append(kt)
    r_ref[0] = r
    kk_ref[0] = kk
    v_ref[0] = v
    bonus_ref[0] = _group_sum(r * (0.5 * (kts[0] + kts[1])) * r_k, ones_ref) * v
    ga_ref[0] = jnp.dot(gd, gup_ref[...], preferred_element_type=F32)

    b0 = a_proj
    conv_b, ba_f, ba_b, bx_f, bx_b, sp_f, sp_b = (lvec_ref[n:n + 1] for n in range(7))
    xb = (cw_ref[0:1] * shifted(b0, b0 + width, -2) + cw_ref[1:2] * shifted(b0, b0 + width, -1)
          + cw_ref[2:3] * p[:, b0:b0 + width] + cw_ref[3:4] * shifted(b0, b0 + width, 1) + conv_b)
    gb_ref[0] = jax.nn.gelu(p[:, b0 + width:b0 + 2 * width])
    xbb = xb.astype(BF16)
    for d, ba, bx, sp, a_ref, u_ref in ((0, ba_f, bx_f, sp_f, af_ref, uf_ref), (1, ba_b, bx_b, sp_b, ab_ref, ub_ref)):
        rg = jax.nn.sigmoid(jnp.dot(xbb, wa_ref[d], preferred_element_type=F32) + ba)
        ig = jax.nn.sigmoid(jnp.dot(xbb, wx_ref[d], preferred_element_type=F32) + bx)
        log_a = -LRU_C * rg * sp
        a_ref[0] = jnp.exp(log_a)
        th = jnp.tanh(log_a)
        u_ref[0] = jnp.sqrt(-2.0 * th / (1.0 - th)) * ig * xb


def even_prep(p, n_ctx, mu, w0, w_up, a0, a_up, g_up, k_k, k_a, r_k, conv_w, conv_b, wa, ba, wx, bx, lam):
    b, l, n_proj = p.shape
    width = w0.shape[-1]
    a_proj = mu.shape[-1]
    assert CONV_W == 4 and 2 * DECAY_RANK == V7X_LANES and 2 * ICLR_RANK == V7X_LANES and GATE_RANK == V7X_LANES
    assert a_proj == 3 * width + 3 * V7X_LANES and n_proj == a_proj + 2 * width
    tm = _token_tile(n_ctx, l - n_ctx, 256)
    n_tiles, ctx_tiles = l // tm, n_ctx // tm
    halo_per_tile = tm // HALO_ROWS
    last_halo = l // HALO_ROWS - 1
    rvec = jnp.concatenate([w0, a0, k_k[None], k_a[None], r_k.reshape(1, width), jnp.zeros((1, width), F32)], axis=0)
    lvec = jnp.concatenate([conv_b[None], ba, bx, jax.nn.softplus(-lam), jnp.zeros((1, width), F32)], axis=0)
    wup = jnp.stack([_pad_rank_rows(w_up[0], 0, V7X_LANES), _pad_rank_rows(w_up[1], DECAY_RANK, V7X_LANES)])
    aup = jnp.stack([_pad_rank_rows(a_up[0], 0, V7X_LANES), _pad_rank_rows(a_up[1], ICLR_RANK, V7X_LANES)])

    def block_diag(w):
        return jnp.stack([jax.scipy.linalg.block_diag(*w[d]) for d in range(2)]).astype(BF16)

    tok = pl.BlockSpec((1, tm, width), lambda i, j: (i, j, 0))

    def full(a):
        return pl.BlockSpec(a.shape, lambda i, j, nd=a.ndim: (0,) * nd)

    consts = [mu[None], rvec, wup.astype(BF16), aup.astype(BF16), g_up.astype(BF16),
              _group_ones(width, A_HEAD_DIM), conv_w, block_diag(wa), block_diag(wx), lvec]
    return pl.pallas_call(
        functools.partial(_even_prep_body, tm=tm, ctx_tiles=ctx_tiles, n_tiles=n_tiles, width=width, a_proj=a_proj),
        grid=(b, n_tiles),
        in_specs=[
            pl.BlockSpec((1, tm, n_proj), lambda i, j: (i, j, 0)),
            pl.BlockSpec((1, HALO_ROWS, n_proj), lambda i, j: (i, jnp.maximum(j * halo_per_tile - 1, 0), 0)),
            pl.BlockSpec((1, HALO_ROWS, n_proj), lambda i, j: (i, jnp.minimum((j + 1) * halo_per_tile, last_halo), 0)),
        ] + [full(a) for a in consts],
        out_specs=[tok] * 16,
        out_shape=[jax.ShapeDtypeStruct((b, l, width), F32)] * 16,
        compiler_params=_cparams(("parallel", "parallel")),
        name="even_prep",
    )(p, p, p, *consts)


def _even_post_body(of_ref, ob_ref, bonus_ref, ga_ref, hf_ref, hb_ref, gb_ref, ln_ref, ones_ref,
                    x_ref, g_ref, w_ref, o_ref, *, width):
    o = of_ref[0] + ob_ref[0]
    inv_n = 1.0 / A_HEAD_DIM
    cen = o - _group_sum(o, ones_ref) * inv_n
    var = _group_sum(cen * cen, ones_ref) * inv_n
    on = cen * lax.rsqrt(var + GN_EPS) * ln_ref[0:1] + ln_ref[1:2]
    ya = ((on + bonus_ref[0]) * ga_ref[0]).astype(BF16)
    yb = ((hf_ref[0] + hb_ref[0]) * gb_ref[0]).astype(BF16)
    acc = (jnp.dot(ya, w_ref[:width], preferred_element_type=F32)
           + jnp.dot(yb, w_ref[width:], preferred_element_type=F32))
    o_ref[0] = x_ref[0] + g_ref[0, 0] * acc


def even_post(o_f, o_b, bonus, gate_a, h_f, h_b, gate_b, ln_w, ln_b, xs, gate, w_out, n_ctx):
    b, l, d = xs.shape
    width = o_f.shape[-1]
    tm = _token_tile(n_ctx, l - n_ctx, 256)
    ctx_tiles = n_ctx // tm
    tok = pl.BlockSpec((1, tm, width), lambda i, j: (i, j, 0))
    ln = jnp.stack([ln_w, ln_b])
    ones = _group_ones(width, A_HEAD_DIM)
    return pl.pallas_call(
        functools.partial(_even_post_body, width=width),
        grid=(b, l // tm),
        in_specs=[tok] * 7 + [
            pl.BlockSpec(ln.shape, lambda i, j: (0, 0)),
            pl.BlockSpec(ones.shape, lambda i, j: (0, 0)),
            pl.BlockSpec((1, tm, d), lambda i, j: (i, j, 0)),
            _mod_spec(d, ctx_tiles),
            pl.BlockSpec(w_out.shape, lambda i, j: (0, 0)),
        ],
        out_specs=pl.BlockSpec((1, tm, d), lambda i, j: (i, j, 0)),
        out_shape=jax.ShapeDtypeStruct((b, l, d), F32),
        compiler_params=_cparams(("parallel", "parallel")),
        name="even_post",
    )(o_f, o_b, bonus, gate_a, h_f, h_b, gate_b, ln, ones, xs, gate, w_out.astype(BF16))


def _rwkv_chain_layout(z, heads):
    b, l, width = z.shape
    n = width // heads
    z = z.reshape(b, l, heads, n).transpose(1, 3, 0, 2).reshape(l, n, b * heads)
    z = jnp.pad(z, ((0, 0), (0, 0), (0, RWKV_CHAINS - b * heads)))
    return z.reshape(l, n, V7X_SUBLANES // RWKV_KPAR, V7X_LANES)


def _rwkv_keyed(z, heads):
    z = _rwkv_chain_layout(z, heads)
    l, n = z.shape[:2]
    return z.reshape(l, n // RWKV_KPAR, V7X_SUBLANES, V7X_LANES)


def _rwkv_valued(z, heads):
    z = _rwkv_chain_layout(z, heads)
    return jnp.concatenate([z] * RWKV_KPAR, axis=2)


def _rwkv_unchain(o, b, heads):
    l, n = o.shape[:2]
    o = o[:, :, :V7X_SUBLANES // RWKV_KPAR].reshape(l, n, RWKV_CHAINS)[:, :, :b * heads]
    return o.reshape(l, n, b, heads).transpose(2, 0, 3, 1).reshape(b, l, heads * n)


def _even_mixer(xs, shift, scale, gate, n_ctx, w_in, mu, w_out, w0, w_up, a0, a_up, g_up, k_k, k_a, r_k, ln_w, ln_b,
                conv_w, conv_b, wa, ba, wx, bx, lam):
    b = xs.shape[0]
    heads = w0.shape[-1] // A_HEAD_DIM
    assert b * heads <= RWKV_CHAINS
    p = modulate_matmul(xs, shift, scale, w_in, n_ctx)
    (r, dec_f, dec_b, kt_f, kt_b, kk, akk_f, akk_b, v, bonus, gate_a, a_f, u_f, a_b, u_b, gate_b) = even_prep(
        p, n_ctx, mu, w0, w_up, a0, a_up, g_up, k_k, k_a, r_k, conv_w, conv_b, wa, ba, wx, bx, lam)
    o_f, o_b = rwkv_scan(
        _rwkv_keyed(r, heads), _rwkv_keyed(dec_f, heads), _rwkv_keyed(dec_b, heads),
        _rwkv_keyed(kt_f, heads), _rwkv_keyed(kt_b, heads), _rwkv_keyed(kk, heads),
        _rwkv_keyed(akk_f, heads), _rwkv_keyed(akk_b, heads), _rwkv_valued(v, heads), n_ctx)
    h_f, h_b = diag_scan(a_f, u_f, a_b, u_b, n_ctx)
    return even_post(_rwkv_unchain(o_f, b, heads), _rwkv_unchain(o_b, b, heads), bonus, gate_a, h_f, h_b, gate_b,
                     ln_w, ln_b, xs, gate, w_out, n_ctx)


ROPE_HALF = C_QK_DIM // 4


def _qkv_body(x_ref, sh_ref, sc_ref, w_ref, ones_ref, g_ref, cos_ref, sin_ref, q_ref, k_ref, v_ref, *, width):
    z = _rms_modulate(x_ref[0], sh_ref[0, 0], sc_ref[0, 0])
    qkv = jnp.dot(z.astype(BF16), w_ref[...], preferred_element_type=F32)
    cos = cos_ref[...]
    sin = sin_ref[...]
    lane = lax.broadcasted_iota(I32, cos.shape, 1)
    first_half = (lane % (2 * ROPE_HALF)) < ROPE_HALF
    for part, out_ref in ((0, q_ref), (1, k_ref)):
        gain = g_ref[part:part + 1]
        for c in range(width // V7X_LANES):
            lo = part * width + c * V7X_LANES
            t = qkv[:, lo:lo + V7X_LANES]
            ms = _group_sum(t * t, ones_ref) * (1.0 / C_QK_DIM)
            t = t * lax.rsqrt(ms + RMS_EPS) * gain
            partner = jnp.where(first_half, pltpu.roll(t, V7X_LANES - ROPE_HALF, axis=1),
                                pltpu.roll(t, ROPE_HALF, axis=1))
            out_ref[0, :, c * V7X_LANES:(c + 1) * V7X_LANES] = (t * cos + partner * sin).astype(BF16)
    v_ref[0] = qkv[:, 2 * width:].astype(BF16)


def _rope_tables(n_ctx, n_lat):
    n_rows = n_lat // GRID_W
    row_pos = jnp.repeat(jnp.arange(n_rows), GRID_W).astype(F32)
    col_pos = jnp.tile(jnp.arange(GRID_W), n_rows).astype(F32)
    inv_freq = ROPE_BASE ** (-jnp.arange(ROPE_HALF, dtype=F32) / ROPE_HALF)

    def one(pos):
        ang = pos[:, None] * inv_freq
        c, s = jnp.cos(ang), jnp.sin(ang)
        return jnp.concatenate([c, c], axis=-1), jnp.concatenate([-s, s], axis=-1)

    (cr, sr), (cc, sc) = one(row_pos), one(col_pos)
    cos = jnp.concatenate([cr, cc], axis=-1)
    sin = jnp.concatenate([sr, sc], axis=-1)
    cos = jnp.concatenate([jnp.ones((n_ctx, C_QK_DIM), F32), cos], axis=0)
    sin = jnp.concatenate([jnp.zeros((n_ctx, C_QK_DIM), F32), sin], axis=0)
    reps = V7X_LANES // C_QK_DIM
    return jnp.tile(cos, (1, reps)), jnp.tile(sin, (1, reps))


def qkv_project(xs, shift, scale, w_qkv, q_g, k_g, n_ctx):
    b, l, d = xs.shape
    width = w_qkv.shape[1] // 3
    tm = _token_tile(n_ctx, l - n_ctx, 256)
    ctx_tiles = n_ctx // tm
    cos, sin = _rope_tables(n_ctx, l - n_ctx)
    gains = jnp.stack([jnp.tile(q_g, V7X_LANES // C_QK_DIM), jnp.tile(k_g, V7X_LANES // C_QK_DIM)])
    ones = _group_ones(V7X_LANES, C_QK_DIM)
    out = pl.BlockSpec((1, tm, width), lambda i, j: (i, j, 0))
    return pl.pallas_call(
        functools.partial(_qkv_body, width=width),
        grid=(b, l // tm),
        in_specs=[
            pl.BlockSpec((1, tm, d), lambda i, j: (i, j, 0)),
            _mod_spec(d, ctx_tiles),
            _mod_spec(d, ctx_tiles),
            pl.BlockSpec(w_qkv.shape, lambda i, j: (0, 0)),
            pl.BlockSpec(ones.shape, lambda i, j: (0, 0)),
            pl.BlockSpec(gains.shape, lambda i, j: (0, 0)),
            pl.BlockSpec((tm, V7X_LANES), lambda i, j: (j, 0)),
            pl.BlockSpec((tm, V7X_LANES), lambda i, j: (j, 0)),
        ],
        out_specs=[out, out, out],
        out_shape=[jax.ShapeDtypeStruct((b, l, width), BF16)] * 3,
        compiler_params=_cparams(("parallel", "parallel")),
        name="qkv_project",
    )(xs, shift, scale, w_qkv.astype(BF16), ones, gains, cos, sin)


def kernel(x, c, ctx, c_ctx, w_mod, b_mod, even_w_in, even_mu, even_w_out, rwkv_w0, rwkv_w_up, rwkv_a0, rwkv_a_up, rwkv_g_up, rwkv_k_k, rwkv_k_a, rwkv_r_k, rwkv_ln_w, rwkv_ln_b, lru_conv_w, lru_conv_b, lru_wa, lru_ba, lru_wx, lru_bx, lru_lam, attn_w_qkv, attn_q_g, attn_k_g, attn_lam_q1, attn_lam_k1, attn_lam_q2, attn_lam_k2, attn_sub_g, attn_w_o, peer_w_q, peer_keys, peer_u, peer_v):
    bsz, n_lat, d = x.shape
    n_ctx = ctx.shape[1]
    depth = w_mod.shape[0]
    a_proj = even_mu.shape[-1]
    s_lat = jax.nn.silu(c)
    s_ctx = jax.nn.silu(c_ctx)
    xs = jnp.concatenate([ctx, x], axis=1)
    for layer in range(depth):
        i = layer // 2
        m_l = s_lat @ w_mod[layer] + b_mod[layer]
        m_c = s_ctx @ w_mod[layer] + b_mod[layer]
        mods = jnp.stack([jnp.broadcast_to(m_c, m_l.shape), m_l], axis=1).reshape(bsz, 2, N_MOD, 1, d)
        shift1, scale1, gate1, shift2, scale2, gate2 = (mods[:, :, n] for n in range(N_MOD))
        if layer % 2 == 0:
            xs = _even_mixer(xs, shift1, scale1, gate1, n_ctx, even_w_in[i], even_mu[i], even_w_out[i],
                             rwkv_w0[i], rwkv_w_up[i], rwkv_a0[i], rwkv_a_up[i], rwkv_g_up[i], rwkv_k_k[i],
                             rwkv_k_a[i], rwkv_r_k[i], rwkv_ln_w[i], rwkv_ln_b[i], lru_conv_w[i], lru_conv_b[i],
                             lru_wa[i], lru_ba[i], lru_wx[i], lru_bx[i], lru_lam[i])
        else:
            lam_init = 0.8 - 0.6 * math.exp(-0.3 * layer)
            q, k, v = qkv_project(xs, shift1, scale1, attn_w_qkv[i], attn_q_g[i], attn_k_g[i], n_ctx)
            lam = (jnp.exp(jnp.sum(attn_lam_q1[i] * attn_lam_k1[i]))
                   - jnp.exp(jnp.sum(attn_lam_q2[i] * attn_lam_k2[i])) + lam_init)
            o = diff_attention_core(q, k, v, lam.reshape(1), attn_sub_g[i].reshape(1, C_V_DIM), n_ctx, lam_init)
            xs = gated_out_proj(o, xs, gate1, attn_w_o[i], n_ctx)
        h, isel, jsel, gsel = peer_route(xs, shift2, scale2, peer_w_q[layer], peer_keys[layer], n_ctx)
        xs = peer_experts(h, isel, jsel, gsel, peer_u[layer].astype(BF16), peer_v[layer].astype(BF16), xs,
                          gate2, n_ctx)
    return xs[:, n_ctx:]
```

```python
import functools
import math

import jax
import jax.numpy as jnp
from jax import lax
from jax.experimental import pallas as pl
from jax.experimental.pallas import tpu as pltpu

F32 = jnp.float32
BF16 = jnp.bfloat16
I32 = jnp.int32

V7X_LANES = 128
V7X_SUBLANES = 8
V7X_VMEM_LIMIT_BYTES = 56 * 1024 * 1024

GRID_W = 64
N_MOD = 6
A_HEAD_DIM = 64
DECAY_SCALE = math.exp(-0.5)
GN_EPS = 64e-5
DECAY_RANK = 64
ICLR_RANK = 64
GATE_RANK = 128
B_BLOCK_DIM = 64
CONV_W = 4
LRU_C = 8.0
C_QK_DIM = 64
C_V_DIM = 128
ROPE_BASE = 10000.0
N_KEYS = 128
PEER_HEADS = 8
PEER_TOPK = 16
RMS_EPS = 1e-6


def _cparams(sem):
    return pltpu.CompilerParams(dimension_semantics=sem, vmem_limit_bytes=V7X_VMEM_LIMIT_BYTES)


def _token_tile(n_ctx, n_lat, want):
    t = want
    while n_ctx % t or n_lat % t:
        t //= 2
    return t


def _mod_spec(d, ctx_tiles):
    return pl.BlockSpec((1, 1, 1, d), lambda b, j, *_: (b, jnp.where(j >= ctx_tiles, 1, 0), 0, 0))


def _rms_modulate(x, shift, scale):
    xn = x * lax.rsqrt(jnp.mean(x * x, axis=-1, keepdims=True) + RMS_EPS)
    return xn * (1.0 + scale) + shift


def _modmm_body(x_ref, sh_ref, sc_ref, w_ref, o_ref):
    z = _rms_modulate(x_ref[0], sh_ref[0, 0], sc_ref[0, 0])
    o_ref[0] = jnp.dot(z.astype(BF16), w_ref[...], preferred_element_type=F32)


def modulate_matmul(xs, shift, scale, w, n_ctx):
    b, l, d = xs.shape
    n = w.shape[1]
    tm = _token_tile(n_ctx, l - n_ctx, 256)
    ctx_tiles = n_ctx // tm
    return pl.pallas_call(
        _modmm_body,
        grid=(b, l // tm),
        in_specs=[
            pl.BlockSpec((1, tm, d), lambda i, j: (i, j, 0)),
            _mod_spec(d, ctx_tiles),
            _mod_spec(d, ctx_tiles),
            pl.BlockSpec((d, n), lambda i, j: (0, 0)),
        ],
        out_specs=pl.BlockSpec((1, tm, n), lambda i, j: (i, j, 0)),
        out_shape=jax.ShapeDtypeStruct((b, l, n), F32),
        compiler_params=_cparams(("parallel", "parallel")),
        name="modulate_matmul",
    )(xs, shift, scale, w.astype(BF16))


def _outproj_body(y_ref, x_ref, g_ref, w_ref, o_ref):
    acc = jnp.dot(y_ref[0].astype(BF16), w_ref[...], preferred_element_type=F32)
    o_ref[0] = x_ref[0] + g_ref[0, 0] * acc


def gated_out_proj(y, xs, gate, w, n_ctx):
    b, l, d = xs.shape
    k = y.shape[-1]
    tm = _token_tile(n_ctx, l - n_ctx, 256)
    ctx_tiles = n_ctx // tm
    return pl.pallas_call(
        _outproj_body,
        grid=(b, l // tm),
        in_specs=[
            pl.BlockSpec((1, tm, k), lambda i, j: (i, j, 0)),
            pl.BlockSpec((1, tm, d), lambda i, j: (i, j, 0)),
            _mod_spec(d, ctx_tiles),
            pl.BlockSpec((k, d), lambda i, j: (0, 0)),
        ],
        out_specs=pl.BlockSpec((1, tm, d), lambda i, j: (i, j, 0)),
        out_shape=jax.ShapeDtypeStruct((b, l, d), F32),
        compiler_params=_cparams(("parallel", "parallel")),
        name="gated_out_proj",
    )(y, xs, gate, w.astype(BF16))


RWKV_VBLK_A = 16
RWKV_VBLK_B = 8
RWKV_KPAR = 4
RWKV_CHAINS = (V7X_SUBLANES // RWKV_KPAR) * V7X_LANES
SCAN_TB = 8


def _reversed_block_map(n_blocks, ctx_blocks):
    def rev(i):
        return jnp.where(i < ctx_blocks, ctx_blocks - 1 - i, n_blocks - 1 - i + ctx_blocks)
    return rev


def _rwkv_scan_body(rf_ref, wf_ref, kf_ref, kkf_ref, akkf_ref, vf_ref,
                    rb_ref, wb_ref, kb_ref, kkb_ref, akkb_ref, vb_ref,
                    of_ref, ob_ref, tf_ref, tb_ref, sa_ref, *, tb, nk4, nv):
    @pl.when(pl.program_id(0) == 0)
    def _():
        tf_ref[...] = jnp.zeros_like(tf_ref)
        tb_ref[...] = jnp.zeros_like(tb_ref)

    def fold(acc):
        acc = acc + pltpu.roll(acc, V7X_SUBLANES // 2, axis=1)
        return acc + pltpu.roll(acc, V7X_SUBLANES // 4, axis=1)

    def one_direction(t, r_ref, w_ref, k_ref, kk_ref, akk_ref, v_ref, o_ref, t_ref):
        for vb in range(nv // RWKV_VBLK_A):
            vs = pl.ds(vb * RWKV_VBLK_A, RWKV_VBLK_A)

            def acc_sa(k4, acc):
                return acc + t_ref[k4, vs] * kk_ref[t, k4][None]

            acc = lax.fori_loop(0, nk4, acc_sa, jnp.zeros((RWKV_VBLK_A, V7X_SUBLANES, V7X_LANES), F32), unroll=4)
            sa_ref[vs] = fold(acc)
        for vb in range(nv // RWKV_VBLK_B):
            vs = pl.ds(vb * RWKV_VBLK_B, RWKV_VBLK_B)
            sa = sa_ref[vs]
            vv = v_ref[t, vs]

            def upd(k4, acc):
                tn = (t_ref[k4, vs] * w_ref[t, k4][None] - akk_ref[t, k4][None] * sa
                      + k_ref[t, k4][None] * vv)
                t_ref[k4, vs] = tn
                return acc + tn * r_ref[t, k4][None]

            acc = lax.fori_loop(0, nk4, upd, jnp.zeros((RWKV_VBLK_B, V7X_SUBLANES, V7X_LANES), F32), unroll=4)
            o_ref[t, vs] = fold(acc)

    def step(t, carry):
        one_direction(t, rf_ref, wf_ref, kf_ref, kkf_ref, akkf_ref, vf_ref, of_ref, tf_ref)
        one_direction(tb - 1 - t, rb_ref, wb_ref, kb_ref, kkb_ref, akkb_ref, vb_ref, ob_ref, tb_ref)
        return carry

    lax.fori_loop(0, tb, step, 0)


def rwkv_scan(r, w_f, w_b, k_f, k_b, kk, akk_f, akk_b, v, n_ctx):
    l, nk4 = r.shape[0], r.shape[1]
    nv = v.shape[1]
    tb = SCAN_TB
    rev = _reversed_block_map(l // tb, n_ctx // tb)
    kblk = (tb, nk4, V7X_SUBLANES, V7X_LANES)
    vblk = (tb, nv, V7X_SUBLANES, V7X_LANES)
    kf = pl.BlockSpec(kblk, lambda i: (i, 0, 0, 0))
    kb = pl.BlockSpec(kblk, lambda i: (rev(i), 0, 0, 0))
    vf = pl.BlockSpec(vblk, lambda i: (i, 0, 0, 0))
    vb = pl.BlockSpec(vblk, lambda i: (rev(i), 0, 0, 0))
    out = jax.ShapeDtypeStruct((l, nv, V7X_SUBLANES, V7X_LANES), F32)
    state = pltpu.VMEM((nk4, nv, V7X_SUBLANES, V7X_LANES), F32)
    return pl.pallas_call(
        functools.partial(_rwkv_scan_body, tb=tb, nk4=nk4, nv=nv),
        grid=(l // tb,),
        in_specs=[kf, kf, kf, kf, kf, vf, kb, kb, kb, kb, kb, vb],
        out_specs=[vf, vb],
        out_shape=[out, out],
        scratch_shapes=[state, state, pltpu.VMEM((nv, V7X_SUBLANES, V7X_LANES), F32)],
        compiler_params=_cparams(("arbitrary",)),
        name="rwkv_scan",
    )(r, w_f, k_f, kk, akk_f, v, r, w_b, k_b, kk, akk_b, v)


def _diag_scan_body(af_ref, uf_ref, ab_ref, ub_ref, of_ref, ob_ref, hf_ref, hb_ref, *, tb):
    @pl.when(pl.program_id(0) == 0)
    def _():
        hf_ref[...] = jnp.zeros_like(hf_ref)
        hb_ref[...] = jnp.zeros_like(hb_ref)

    def step(t, carry):
        hf, hb = carry
        hf = af_ref[:, t, :] * hf + uf_ref[:, t, :]
        of_ref[:, t, :] = hf
        tr = tb - 1 - t
        hb = ab_ref[:, tr, :] * hb + ub_ref[:, tr, :]
        ob_ref[:, tr, :] = hb
        return hf, hb

    hf, hb = lax.fori_loop(0, tb, step, (hf_ref[...], hb_ref[...]))
    hf_ref[...] = hf
    hb_ref[...] = hb


def diag_scan(a_f, u_f, a_b, u_b, n_ctx):
    b, l, c = a_f.shape
    tb = SCAN_TB
    rev = _reversed_block_map(l // tb, n_ctx // tb)
    fwd = pl.BlockSpec((b, tb, c), lambda i: (0, i, 0))
    bwd = pl.BlockSpec((b, tb, c), lambda i: (0, rev(i), 0))
    out = jax.ShapeDtypeStruct((b, l, c), F32)
    return pl.pallas_call(
        functools.partial(_diag_scan_body, tb=tb),
        grid=(l // tb,),
        in_specs=[fwd, fwd, bwd, bwd],
        out_specs=[fwd, bwd],
        out_shape=[out, out],
        scratch_shapes=[pltpu.VMEM((b, c), F32), pltpu.VMEM((b, c), F32)],
        compiler_params=_cparams(("arbitrary",)),
        name="diag_scan",
    )(a_f, u_f, a_b, u_b)


def _attn_body(lam_ref, q_ref, k_ref, v_ref, sg_ref, o_ref, *, ctx_tiles, n_ctx, scale, out_scale):
    j = pl.program_id(2)
    q = q_ref[0]
    lane = lax.broadcasted_iota(I32, q.shape, 1)
    qs = q * jnp.asarray(scale, BF16)
    q1 = jnp.where(lane < C_QK_DIM, qs, jnp.zeros_like(qs))
    q2 = jnp.where(lane >= C_QK_DIM, qs, jnp.zeros_like(qs))
    lam = lam_ref[0]

    def attend(kt, vb):
        def softmax_times_v(qm):
            s = jnp.dot(qm, kt, preferred_element_type=F32)
            e = jnp.exp(s - jnp.max(s, axis=-1, keepdims=True))
            pv = jnp.dot(e.astype(BF16), vb, preferred_element_type=F32)
            return pv / jnp.sum(e, axis=-1, keepdims=True)

        o = softmax_times_v(q1) - lam * softmax_times_v(q2)
        o = o * lax.rsqrt(jnp.mean(o * o, axis=-1, keepdims=True) + RMS_EPS)
        o_ref[0] = o * sg_ref[...] * out_scale

    @pl.when(j < ctx_tiles)
    def _():
        attend(k_ref[0, :, :n_ctx], v_ref[0, :n_ctx])

    @pl.when(j >= ctx_tiles)
    def _():
        attend(k_ref[0], v_ref[0])


def diff_attention_core(q, k_t, v, lam, sub_g, n_ctx, lam_init):
    b, l, width = q.shape
    heads = width // C_V_DIM
    assert n_ctx % V7X_LANES == 0
    tq = _token_tile(n_ctx, l - n_ctx, 256)
    ctx_tiles = n_ctx // tq
    scale = C_QK_DIM ** -0.5
    assert math.frexp(scale)[0] == 0.5, "the kernel folds the score scale into q, exact only for a power of two"
    body = functools.partial(_attn_body, ctx_tiles=ctx_tiles, n_ctx=n_ctx, scale=scale, out_scale=1.0 - lam_init)
    return pl.pallas_call(
        body,
        grid=(b, heads, l // tq),
        in_specs=[
            pl.BlockSpec(memory_space=pltpu.SMEM),
            pl.BlockSpec((1, tq, C_V_DIM), lambda i, h, j: (i, j, h)),
            pl.BlockSpec((1, C_V_DIM, l), lambda i, h, j: (i, h, 0)),
            pl.BlockSpec((1, l, C_V_DIM), lambda i, h, j: (i, 0, h)),
            pl.BlockSpec((1, C_V_DIM), lambda i, h, j: (0, 0)),
        ],
        out_specs=pl.BlockSpec((1, tq, C_V_DIM), lambda i, h, j: (i, j, h)),
        out_shape=jax.ShapeDtypeStruct((b, l, width), F32),
        compiler_params=_cparams(("parallel", "parallel", "parallel")),
        name="diff_attention",
    )(lam, q, k_t, v, sub_g)


def _peer_candidates():
    return [(a, b) for a in range(PEER_TOPK) for b in range(PEER_TOPK) if (a + 1) * (b + 1) <= PEER_TOPK]


def _segment_rows(mod_ref, tile, tm, n_ctx):
    row = tile * tm + lax.broadcasted_iota(I32, (tm, 1), 0)
    return jnp.where(row < n_ctx, mod_ref[0, 0], mod_ref[0, 1])


def _peer_route_body(x_ref, sh_ref, sc_ref, wq_ref, keys_ref, h_ref, i_ref, j_ref, g_ref,
                     sv_ref, si_ref, cand_ref, best_ref, bi_ref, bj_ref, s_ref, *, tm, n_ctx):
    tile = pl.program_id(1)
    h = _rms_modulate(x_ref[0], _segment_rows(sh_ref, tile, tm, n_ctx),
                      _segment_rows(sc_ref, tile, tm, n_ctx)).astype(BF16)
    h_ref[0] = h
    q = jnp.dot(h, wq_ref[...], preferred_element_type=F32).astype(BF16)
    nt = (((1,), (1,)), ((), ()))
    neg = jnp.float32(-jnp.inf)

    def best_of(nodes):
        while len(nodes) > 1:
            nxt = []
            for (va, ia), (vb, ib) in zip(nodes[0::2], nodes[1::2]):
                left = va >= vb
                nxt.append((jnp.maximum(va, vb), jnp.where(left, ia, ib)))
            nodes = nxt
        return nodes[0]

    for m in range(2):
        for col in range(tm // V7X_LANES):
            toks = slice(col * V7X_LANES, (col + 1) * V7X_LANES)
            per_head = [
                lax.dot_general(keys_ref[m], q[toks, (hd * 2 + m) * N_KEYS:(hd * 2 + m + 1) * N_KEYS], nt,
                                preferred_element_type=F32) for hd in range(PEER_HEADS)]
            s_ref[...] = jnp.swapaxes(jnp.stack(per_head, axis=0), 0, 1)

            def extract(a, carry, m=m, toks=toks):
                vals = [s_ref[n] for n in range(N_KEYS)]
                mx, idx = best_of([(vals[n], jnp.float32(n)) for n in range(N_KEYS)])
                sv_ref[m, a, :, toks] = mx
                si_ref[m, a, :, toks] = idx
                for n in range(N_KEYS):
                    s_ref[n] = jnp.where(idx == jnp.float32(n), neg, vals[n])
                return carry

            lax.fori_loop(0, PEER_TOPK, extract, 0)

    cands = _peer_candidates()
    for c, (a, b) in enumerate(cands):
        cand_ref[c] = sv_ref[0, a] + sv_ref[1, b]
    big = jnp.float32(PEER_TOPK * PEER_TOPK)

    def pick(k, carry):
        vals = [cand_ref[c] for c in range(len(cands))]
        mx = functools.reduce(jnp.maximum, vals)
        pos = functools.reduce(
            jnp.minimum,
            [jnp.where(vals[c] == mx, jnp.float32(a * PEER_TOPK + b), big) for c, (a, b) in enumerate(cands)])
        ii = jnp.zeros_like(pos)
        jj = jnp.zeros_like(pos)
        for c, (a, b) in enumerate(cands):
            hit = pos == jnp.float32(a * PEER_TOPK + b)
            cand_ref[c] = jnp.where(hit, neg, vals[c])
            ii = jnp.where(hit, si_ref[0, a], ii)
            jj = jnp.where(hit, si_ref[1, b], jj)
        best_ref[k] = mx
        bi_ref[k] = ii
        bj_ref[k] = jj
        return carry

    lax.fori_loop(0, PEER_TOPK, pick, 0)
    best = best_ref[...]
    e = jnp.exp(best - best[0][None])
    g = e / jnp.sum(e, axis=0, keepdims=True)
    g_ref[0] = g.reshape(PEER_TOPK * PEER_HEADS, tm)
    i_ref[0] = bi_ref[...].reshape(PEER_TOPK * PEER_HEADS, tm)
    j_ref[0] = bj_ref[...].reshape(PEER_TOPK * PEER_HEADS, tm)


PEER_TILE_TARGET = 512


def _peer_tile(l):
    t = (PEER_TILE_TARGET // V7X_LANES) * V7X_LANES
    while l % t:
        t -= V7X_LANES
    return t


def _mod_pair_spec(d):
    return pl.BlockSpec((1, 2, 1, d), lambda b, *_: (b, 0, 0, 0))


def peer_route(xs, shift, scale, w_q, keys, n_ctx):
    b, l, d = xs.shape
    tm = _peer_tile(l)
    nsel = PEER_TOPK * PEER_HEADS
    ncand = len(_peer_candidates())
    sel_spec = pl.BlockSpec((1, nsel, tm), lambda i, j: (i, 0, j))
    return pl.pallas_call(
        functools.partial(_peer_route_body, tm=tm, n_ctx=n_ctx),
        grid=(b, l // tm),
        in_specs=[
            pl.BlockSpec((1, tm, d), lambda i, j: (i, j, 0)),
            _mod_pair_spec(d),
            _mod_pair_spec(d),
            pl.BlockSpec(w_q.shape, lambda i, j: (0, 0)),
            pl.BlockSpec(keys.shape, lambda i, j: (0, 0, 0)),
        ],
        out_specs=[pl.BlockSpec((1, tm, d), lambda i, j: (i, j, 0)), sel_spec, sel_spec, sel_spec],
        out_shape=[
            jax.ShapeDtypeStruct((b, l, d), BF16),
            jax.ShapeDtypeStruct((b, nsel, l), F32),
            jax.ShapeDtypeStruct((b, nsel, l), F32),
            jax.ShapeDtypeStruct((b, nsel, l), F32),
        ],
        scratch_shapes=[
            pltpu.VMEM((2, PEER_TOPK, PEER_HEADS, tm), F32),
            pltpu.VMEM((2, PEER_TOPK, PEER_HEADS, tm), F32),
            pltpu.VMEM((ncand, PEER_HEADS, tm), F32),
            pltpu.VMEM((PEER_TOPK, PEER_HEADS, tm), F32),
            pltpu.VMEM((PEER_TOPK, PEER_HEADS, tm), F32),
            pltpu.VMEM((PEER_TOPK, PEER_HEADS, tm), F32),
            pltpu.VMEM((N_KEYS, PEER_HEADS, V7X_LANES), F32),
        ],
        compiler_params=_cparams(("parallel", "parallel")),
        name="peer_route",
    )(xs, shift, scale, w_q.astype(BF16), keys.astype(BF16))


PEER_G_CHUNK = 32
PEER_PAIR = 2 * N_KEYS
PEER_EXPERT_BLOCK = 2048


def _peer_expert_body(h_ref, i_ref, j_ref, g_ref, u_ref, v_ref, x_ref, gate_ref, o_ref, gw_ref, acc_ref,
                      *, tm, eb, n_ctx):
    e = pl.program_id(2)

    @pl.when(e == 0)
    def _():
        acc_ref[...] = jnp.zeros_like(acc_ref)
        isel = i_ref[0].T
        jsel = j_ref[0].T
        gsel = g_ref[0].T
        kio = lax.broadcasted_iota(I32, (PEER_G_CHUNK, N_KEYS, isel.shape[1]), 1).astype(F32)
        for c in range(tm // PEER_G_CHUNK):
            rows = slice(c * PEER_G_CHUNK, (c + 1) * PEER_G_CHUNK)
            oh_i = jnp.where(isel[rows][:, None, :] == kio, 1.0, 0.0).astype(BF16)
            oh_j = jnp.where(jsel[rows][:, None, :] == kio, gsel[rows][:, None, :], 0.0).astype(BF16)
            g3 = jnp.einsum("pis,pjs->pij", oh_i, oh_j, preferred_element_type=F32)
            gw_ref[:, rows, :] = jnp.swapaxes(g3, 0, 1).astype(BF16)

    h = h_ref[0]
    for ip in range(eb // PEER_PAIR):
        rows = slice(ip * PEER_PAIR, (ip + 1) * PEER_PAIR)
        act = jax.nn.gelu(jnp.dot(h, u_ref[:, rows], preferred_element_type=F32))
        i0 = e * (eb // N_KEYS) + ip * 2
        gw = jnp.concatenate([gw_ref[i0], gw_ref[i0 + 1]], axis=-1)
        acc_ref[...] += jnp.dot(act.astype(BF16) * gw, v_ref[rows, :], preferred_element_type=F32)

    @pl.when(e == pl.num_programs(2) - 1)
    def _():
        gate = _segment_rows(gate_ref, pl.program_id(1), tm, n_ctx)
        o_ref[0] = x_ref[0] + gate * acc_ref[...]


def peer_experts(h, isel, jsel, gsel, u_t, v_tab, xs, gate, n_ctx):
    b, l, d = xs.shape
    ne = v_tab.shape[0]
    tm = _peer_tile(l)
    eb = PEER_EXPERT_BLOCK
    nsel = isel.shape[1]
    sel_spec = pl.BlockSpec((1, nsel, tm), lambda i, j, e: (i, 0, j))
    tok_spec = pl.BlockSpec((1, tm, d), lambda i, j, e: (i, j, 0))
    return pl.pallas_call(
        functools.partial(_peer_expert_body, tm=tm, eb=eb, n_ctx=n_ctx),
        grid=(b, l // tm, ne // eb),
        in_specs=[
            tok_spec, sel_spec, sel_spec, sel_spec,
            pl.BlockSpec((d, eb), lambda i, j, e: (0, e)),
            pl.BlockSpec((eb, d), lambda i, j, e: (e, 0)),
            tok_spec,
            _mod_pair_spec(d),
        ],
        out_specs=tok_spec,
        out_shape=jax.ShapeDtypeStruct((b, l, d), F32),
        scratch_shapes=[
            pltpu.VMEM((N_KEYS, tm, N_KEYS), BF16),
            pltpu.VMEM((tm, d), F32),
        ],
        compiler_params=_cparams(("parallel", "parallel", "arbitrary")),
        name="peer_experts",
    )(h, isel, jsel, gsel, u_t, v_tab, xs, gate)


def _group_ones(width, group):
    g = jnp.arange(width) // group
    return (g[:, None] == g[None, :]).astype(F32)


def _group_sum(x, ones_ref):
    return jnp.dot(x, ones_ref[...], precision=lax.Precision.HIGHEST, preferred_element_type=F32)


def _pad_rank_rows(w, lo, total):
    return jnp.pad(w, ((lo, total - lo - w.shape[0]), (0, 0)))


HALO_ROWS = V7X_SUBLANES


def _even_prep_body(pm_ref, pp_ref, pn_ref, mu_ref, rvec_ref, wup_ref, aup_ref, gup_ref, ones_ref,
                    cw_ref, wa_ref, wx_ref, lvec_ref,
                    r_ref, decf_ref, decb_ref, ktf_ref, ktb_ref, kk_ref, akkf_ref, akkb_ref, v_ref,
                    bonus_ref, ga_ref, af_ref, uf_ref, ab_ref, ub_ref, gb_ref,
                    *, tm, ctx_tiles, n_tiles, width, a_proj):
    j = pl.program_id(1)
    first = jnp.logical_or(j == 0, j == ctx_tiles)
    last = jnp.logical_or(j == ctx_tiles - 1, j == n_tiles - 1)
    p = pm_ref[0]
    prev = jnp.where(first, 0.0, pp_ref[0])
    nxt = jnp.where(last, 0.0, pn_ref[0])
    row = lax.broadcasted_iota(I32, (tm, 1), 0)

    def shifted(lo, hi, off):
        x = p[:, lo:hi]
        if off < 0:
            y = pltpu.roll(x, -off, axis=0)
            for s in range(-off):
                y = jnp.where(row == s, prev[HALO_ROWS + off + s:HALO_ROWS + off + s + 1, lo:hi], y)
        else:
            y = pltpu.roll(x, tm - off, axis=0)
            for s in range(off):
                y = jnp.where(row == tm - off + s, nxt[s:s + 1, lo:hi], y)
        return y

    pa = p[:, :a_proj]
    pa = pa + mu_ref[...] * (0.5 * (shifted(0, a_proj, -1) + shifted(0, a_proj, 1)) - pa)
    r = pa[:, :width]
    k = pa[:, width:2 * width]
    v = pa[:, 2 * width:3 * width]
    c0 = 3 * width
    wd = jnp.tanh(pa[:, c0:c0 + V7X_LANES]).astype(BF16)
    ad = pa[:, c0 + V7X_LANES:c0 + 2 * V7X_LANES].astype(BF16)
    gd = jax.nn.sigmoid(pa[:, c0 + 2 * V7X_LANES:c0 + 3 * V7X_LANES]).astype(BF16)
    w0_f, w0_b, a0_f, a0_b, k_k, k_a, r_k = (rvec_ref[n:n + 1] for n in range(7))

    kk = k * k_k
    kk = kk / jnp.maximum(jnp.sqrt(_group_sum(kk * kk, ones_ref)), 1e-12)
    kts = []
    for w0, a0, d, dec_ref, kt_ref, akk_ref in ((w0_f, a0_f, 0, decf_ref, ktf_ref, akkf_ref),
                                                (w0_b, a0_b, 1, decb_ref, ktb_ref, akkb_ref)):
        lora_w = jnp.dot(wd, wup_ref[d], preferred_element_type=F32)
        dec_ref[0] = jnp.exp(-DECAY_SCALE * jax.nn.sigmoid(w0 + lora_w))
        a = jax.nn.sigmoid(a0 + jnp.dot(ad, aup_ref[d], preferred_element_type=F32))
        kt = k * (1.0 + (a - 1.0) * k_a)
        kt_ref[0] = kt
        akk_ref[0] = a * kk
        kts.append(kt)
    r_ref[0] = r
    kk_ref[0] = kk
    v_ref[0] = v
    bonus_ref[0] = _group_sum(r * (0.5 * (kts[0] + kts[1])) * r_k, ones_ref) * v
    ga_ref[0] = jnp.dot(gd, gup_ref[...], preferred_element_type=F32)

    b0 = a_proj
    conv_b, ba_f, ba_b, bx_f, bx_b, sp_f, sp_b = (lvec_ref[n:n + 1] for n in range(7))
    xb = (cw_ref[0:1] * shifted(b0, b0 + width, -2) + cw_ref[1:2] * shifted(b0, b0 + width, -1)
          + cw_ref[2:3] * p[:, b0:b0 + width] + cw_ref[3:4] * shifted(b0, b0 + width, 1) + conv_b)
    gb_ref[0] = jax.nn.gelu(p[:, b0 + width:b0 + 2 * width])
    xbb = xb.astype(BF16)
    for d, ba, bx, sp, a_ref, u_ref in ((0, ba_f, bx_f, sp_f, af_ref, uf_ref), (1, ba_b, bx_b, sp_b, ab_ref, ub_ref)):
        rg = jax.nn.sigmoid(jnp.dot(xbb, wa_ref[d], preferred_element_type=F32) + ba)
        ig = jax.nn.sigmoid(jnp.dot(xbb, wx_ref[d], preferred_element_type=F32) + bx)
        log_a = -LRU_C * rg * sp
        a_ref[0] = jnp.exp(log_a)
        th = jnp.tanh(log_a)
        u_ref[0] = jnp.sqrt(-2.0 * th / (1.0 - th)) * ig * xb


def even_prep(p, n_ctx, mu, w0, w_up, a0, a_up, g_up, k_k, k_a, r_k, conv_w, conv_b, wa, ba, wx, bx, lam):
    b, l, n_proj = p.shape
    width = w0.shape[-1]
    a_proj = mu.shape[-1]
    assert CONV_W == 4 and 2 * DECAY_RANK == V7X_LANES and 2 * ICLR_RANK == V7X_LANES and GATE_RANK == V7X_LANES
    assert a_proj == 3 * width + 3 * V7X_LANES and n_proj == a_proj + 2 * width
    tm = _token_tile(n_ctx, l - n_ctx, 256)
    n_tiles, ctx_tiles = l // tm, n_ctx // tm
    halo_per_tile = tm // HALO_ROWS
    last_halo = l // HALO_ROWS - 1
    rvec = jnp.concatenate([w0, a0, k_k[None], k_a[None], r_k.reshape(1, width), jnp.zeros((1, width), F32)], axis=0)
    lvec = jnp.concatenate([conv_b[None], ba, bx, jax.nn.softplus(-lam), jnp.zeros((1, width), F32)], axis=0)
    wup = jnp.stack([_pad_rank_rows(w_up[0], 0, V7X_LANES), _pad_rank_rows(w_up[1], DECAY_RANK, V7X_LANES)])
    aup = jnp.stack([_pad_rank_rows(a_up[0], 0, V7X_LANES), _pad_rank_rows(a_up[1], ICLR_RANK, V7X_LANES)])

    def block_diag(w):
        return jnp.stack([jax.scipy.linalg.block_diag(*w[d]) for d in range(2)]).astype(BF16)

    tok = pl.BlockSpec((1, tm, width), lambda i, j: (i, j, 0))

    def full(a):
        return pl.BlockSpec(a.shape, lambda i, j, nd=a.ndim: (0,) * nd)

    consts = [mu[None], rvec, wup.astype(BF16), aup.astype(BF16), g_up.astype(BF16),
              _group_ones(width, A_HEAD_DIM), conv_w, block_diag(wa), block_diag(wx), lvec]
    return pl.pallas_call(
        functools.partial(_even_prep_body, tm=tm, ctx_tiles=ctx_tiles, n_tiles=n_tiles, width=width, a_proj=a_proj),
        grid=(b, n_tiles),
        in_specs=[
            pl.BlockSpec((1, tm, n_proj), lambda i, j: (i, j, 0)),
            pl.BlockSpec((1, HALO_ROWS, n_proj), lambda i, j: (i, jnp.maximum(j * halo_per_tile - 1, 0), 0)),
            pl.BlockSpec((1, HALO_ROWS, n_proj), lambda i, j: (i, jnp.minimum((j + 1) * halo_per_tile, last_halo), 0)),
        ] + [full(a) for a in consts],
        out_specs=[tok] * 16,
        out_shape=[jax.ShapeDtypeStruct((b, l, width), F32)] * 16,
        compiler_params=_cparams(("parallel", "parallel")),
        name="even_prep",
    )(p, p, p, *consts)


def _even_post_body(of_ref, ob_ref, bonus_ref, ga_ref, hf_ref, hb_ref, gb_ref, ln_ref, ones_ref,
                    x_ref, g_ref, w_ref, o_ref, *, width):
    o = of_ref[0] + ob_ref[0]
    inv_n = 1.0 / A_HEAD_DIM
    cen = o - _group_sum(o, ones_ref) * inv_n
    var = _group_sum(cen * cen, ones_ref) * inv_n
    on = cen * lax.rsqrt(var + GN_EPS) * ln_ref[0:1] + ln_ref[1:2]
    ya = ((on + bonus_ref[0]) * ga_ref[0]).astype(BF16)
    yb = ((hf_ref[0] + hb_ref[0]) * gb_ref[0]).astype(BF16)
    acc = (jnp.dot(ya, w_ref[:width], preferred_element_type=F32)
           + jnp.dot(yb, w_ref[width:], preferred_element_type=F32))
    o_ref[0] = x_ref[0] + g_ref[0, 0] * acc


def even_post(o_f, o_b, bonus, gate_a, h_f, h_b, gate_b, ln_w, ln_b, xs, gate, w_out, n_ctx):
    b, l, d = xs.shape
    width = o_f.shape[-1]
    tm = _token_tile(n_ctx, l - n_ctx, 256)
    ctx_tiles = n_ctx // tm
    tok = pl.BlockSpec((1, tm, width), lambda i, j: (i, j, 0))
    ln = jnp.stack([ln_w, ln_b])
    ones = _group_ones(width, A_HEAD_DIM)
    return pl.pallas_call(
        functools.partial(_even_post_body, width=width),
        grid=(b, l // tm),
        in_specs=[tok] * 7 + [
            pl.BlockSpec(ln.shape, lambda i, j: (0, 0)),
            pl.BlockSpec(ones.shape, lambda i, j: (0, 0)),
            pl.BlockSpec((1, tm, d), lambda i, j: (i, j, 0)),
            _mod_spec(d, ctx_tiles),
            pl.BlockSpec(w_out.shape, lambda i, j: (0, 0)),
        ],
        out_specs=pl.BlockSpec((1, tm, d), lambda i, j: (i, j, 0)),
        out_shape=jax.ShapeDtypeStruct((b, l, d), F32),
        compiler_params=_cparams(("parallel", "parallel")),
        name="even_post",
    )(o_f, o_b, bonus, gate_a, h_f, h_b, gate_b, ln, ones, xs, gate, w_out.astype(BF16))


def _rwkv_chain_layout(z, heads):
    b, l, width = z.shape
    n = width // heads
    z = z.reshape(b, l, heads, n).transpose(1, 3, 0, 2).reshape(l, n, b * heads)
    z = jnp.pad(z, ((0, 0), (0, 0), (0, RWKV_CHAINS - b * heads)))
    return z.reshape(l, n, V7X_SUBLANES // RWKV_KPAR, V7X_LANES)


def _rwkv_keyed(z, heads):
    z = _rwkv_chain_layout(z, heads)
    l, n = z.shape[:2]
    return z.reshape(l, n // RWKV_KPAR, V7X_SUBLANES, V7X_LANES)


def _rwkv_valued(z, heads):
    z = _rwkv_chain_layout(z, heads)
    return jnp.concatenate([z] * RWKV_KPAR, axis=2)


def _rwkv_unchain(o, b, heads):
    l, n = o.shape[:2]
    o = o[:, :, :V7X_SUBLANES // RWKV_KPAR].reshape(l, n, RWKV_CHAINS)[:, :, :b * heads]
    return o.reshape(l, n, b, heads).transpose(2, 0, 3, 1).reshape(b, l, heads * n)


def _even_mixer(xs, shift, scale, gate, n_ctx, w_in, mu, w_out, w0, w_up, a0, a_up, g_up, k_k, k_a, r_k, ln_w, ln_b,
                conv_w, conv_b, wa, ba, wx, bx, lam):
    b = xs.shape[0]
    heads = w0.shape[-1] // A_HEAD_DIM
    assert b * heads <= RWKV_CHAINS
    p = modulate_matmul(xs, shift, scale, w_in, n_ctx)
    (r, dec_f, dec_b, kt_f, kt_b, kk, akk_f, akk_b, v, bonus, gate_a, a_f, u_f, a_b, u_b, gate_b) = even_prep(
        p, n_ctx, mu, w0, w_up, a0, a_up, g_up, k_k, k_a, r_k, conv_w, conv_b, wa, ba, wx, bx, lam)
    o_f, o_b = rwkv_scan(
        _rwkv_keyed(r, heads), _rwkv_keyed(dec_f, heads), _rwkv_keyed(dec_b, heads),
        _rwkv_keyed(kt_f, heads), _rwkv_keyed(kt_b, heads), _rwkv_keyed(kk, heads),
        _rwkv_keyed(akk_f, heads), _rwkv_keyed(akk_b, heads), _rwkv_valued(v, heads), n_ctx)
    h_f, h_b = diag_scan(a_f, u_f, a_b, u_b, n_ctx)
    return even_post(_rwkv_unchain(o_f, b, heads), _rwkv_unchain(o_b, b, heads), bonus, gate_a, h_f, h_b, gate_b,
                     ln_w, ln_b, xs, gate, w_out, n_ctx)


ROPE_HALF = C_QK_DIM // 4


def _qkv_body(x_ref, sh_ref, sc_ref, w_ref, ones_ref, g_ref, cos_ref, sin_ref, q_ref, k_ref, v_ref, *, width):
    z = _rms_modulate(x_ref[0], sh_ref[0, 0], sc_ref[0, 0])
    qkv = jnp.dot(z.astype(BF16), w_ref[...], preferred_element_type=F32)
    cos = cos_ref[...]
    sin = sin_ref[...]
    lane = lax.broadcasted_iota(I32, cos.shape, 1)
    first_half = (lane % (2 * ROPE_HALF)) < ROPE_HALF
    for part, out_ref in ((0, q_ref), (1, k_ref)):
        gain = g_ref[part:part + 1]
        for c in range(width // V7X_LANES):
            lo = part * width + c * V7X_LANES
            t = qkv[:, lo:lo + V7X_LANES]
            ms = _group_sum(t * t, ones_ref) * (1.0 / C_QK_DIM)
            t = t * lax.rsqrt(ms + RMS_EPS) * gain
            partner = jnp.where(first_half, pltpu.roll(t, V7X_LANES - ROPE_HALF, axis=1),
                                pltpu.roll(t, ROPE_HALF, axis=1))
            t = t * cos + partner * sin
            if part == 0:
                out_ref[0, :, c * V7X_LANES:(c + 1) * V7X_LANES] = t.astype(BF16)
            else:
                out_ref[0, c * V7X_LANES:(c + 1) * V7X_LANES, :] = t.T.astype(BF16)
    v_ref[0] = qkv[:, 2 * width:].astype(BF16)


def _rope_tables(n_ctx, n_lat):
    n_rows = n_lat // GRID_W
    row_pos = jnp.repeat(jnp.arange(n_rows), GRID_W).astype(F32)
    col_pos = jnp.tile(jnp.arange(GRID_W), n_rows).astype(F32)
    inv_freq = ROPE_BASE ** (-jnp.arange(ROPE_HALF, dtype=F32) / ROPE_HALF)

    def one(pos):
        ang = pos[:, None] * inv_freq
        c, s = jnp.cos(ang), jnp.sin(ang)
        return jnp.concatenate([c, c], axis=-1), jnp.concatenate([-s, s], axis=-1)

    (cr, sr), (cc, sc) = one(row_pos), one(col_pos)
    cos = jnp.concatenate([cr, cc], axis=-1)
    sin = jnp.concatenate([sr, sc], axis=-1)
    cos = jnp.concatenate([jnp.ones((n_ctx, C_QK_DIM), F32), cos], axis=0)
    sin = jnp.concatenate([jnp.zeros((n_ctx, C_QK_DIM), F32), sin], axis=0)
    reps = V7X_LANES // C_QK_DIM
    return jnp.tile(cos, (1, reps)), jnp.tile(sin, (1, reps))


def qkv_project(xs, shift, scale, w_qkv, q_g, k_g, n_ctx):
    b, l, d = xs.shape
    width = w_qkv.shape[1] // 3
    tm = _token_tile(n_ctx, l - n_ctx, 256)
    ctx_tiles = n_ctx // tm
    cos, sin = _rope_tables(n_ctx, l - n_ctx)
    gains = jnp.stack([jnp.tile(q_g, V7X_LANES // C_QK_DIM), jnp.tile(k_g, V7X_LANES // C_QK_DIM)])
    ones = _group_ones(V7X_LANES, C_QK_DIM)
    out = pl.BlockSpec((1, tm, width), lambda i, j: (i, j, 0))
    return pl.pallas_call(
        functools.partial(_qkv_body, width=width),
        grid=(b, l // tm),
        in_specs=[
            pl.BlockSpec((1, tm, d), lambda i, j: (i, j, 0)),
            _mod_spec(d, ctx_tiles),
            _mod_spec(d, ctx_tiles),
            pl.BlockSpec(w_qkv.shape, lambda i, j: (0, 0)),
            pl.BlockSpec(ones.shape, lambda i, j: (0, 0)),
            pl.BlockSpec(gains.shape, lambda i, j: (0, 0)),
            pl.BlockSpec((tm, V7X_LANES), lambda i, j: (j, 0)),
            pl.BlockSpec((tm, V7X_LANES), lambda i, j: (j, 0)),
        ],
        out_specs=[out, pl.BlockSpec((1, width, tm), lambda i, j: (i, 0, j)), out],
        out_shape=[jax.ShapeDtypeStruct((b, l, width), BF16), jax.ShapeDtypeStruct((b, width, l), BF16),
                   jax.ShapeDtypeStruct((b, l, width), BF16)],
        compiler_params=_cparams(("parallel", "parallel")),
        name="qkv_project",
    )(xs, shift, scale, w_qkv.astype(BF16), ones, gains, cos, sin)


def kernel(x, c, ctx, c_ctx, w_mod, b_mod, even_w_in, even_mu, even_w_out, rwkv_w0, rwkv_w_up, rwkv_a0, rwkv_a_up, rwkv_g_up, rwkv_k_k, rwkv_k_a, rwkv_r_k, rwkv_ln_w, rwkv_ln_b, lru_conv_w, lru_conv_b, lru_wa, lru_ba, lru_wx, lru_bx, lru_lam, attn_w_qkv, attn_q_g, attn_k_g, attn_lam_q1, attn_lam_k1, attn_lam_q2, attn_lam_k2, attn_sub_g, attn_w_o, peer_w_q, peer_keys, peer_u, peer_v):
    bsz, n_lat, d = x.shape
    n_ctx = ctx.shape[1]
    depth = w_mod.shape[0]
    a_proj = even_mu.shape[-1]
    s_lat = jax.nn.silu(c)
    s_ctx = jax.nn.silu(c_ctx)
    xs = jnp.concatenate([ctx, x], axis=1)
    for layer in range(depth):
        i = layer // 2
        m_l = s_lat @ w_mod[layer] + b_mod[layer]
        m_c = s_ctx @ w_mod[layer] + b_mod[layer]
        mods = jnp.stack([jnp.broadcast_to(m_c, m_l.shape), m_l], axis=1).reshape(bsz, 2, N_MOD, 1, d)
        shift1, scale1, gate1, shift2, scale2, gate2 = (mods[:, :, n] for n in range(N_MOD))
        if layer % 2 == 0:
            xs = _even_mixer(xs, shift1, scale1, gate1, n_ctx, even_w_in[i], even_mu[i], even_w_out[i],
                             rwkv_w0[i], rwkv_w_up[i], rwkv_a0[i], rwkv_a_up[i], rwkv_g_up[i], rwkv_k_k[i],
                             rwkv_k_a[i], rwkv_r_k[i], rwkv_ln_w[i], rwkv_ln_b[i], lru_conv_w[i], lru_conv_b[i],
                             lru_wa[i], lru_ba[i], lru_wx[i], lru_bx[i], lru_lam[i])
        else:
            lam_init = 0.8 - 0.6 * math.exp(-0.3 * layer)
            q, k, v = qkv_project(xs, shift1, scale1, attn_w_qkv[i], attn_q_g[i], attn_k_g[i], n_ctx)
            lam = (jnp.exp(jnp.sum(attn_lam_q1[i] * attn_lam_k1[i]))
                   - jnp.exp(jnp.sum(attn_lam_q2[i] * attn_lam_k2[i])) + lam_init)
            o = diff_attention_core(q, k, v, lam.reshape(1), attn_sub_g[i].reshape(1, C_V_DIM), n_ctx, lam_init)
            xs = gated_out_proj(o, xs, gate1, attn_w_o[i], n_ctx)
        h, isel, jsel, gsel = peer_route(xs, shift2, scale2, peer_w_q[layer], peer_keys[layer], n_ctx)
        xs = peer_experts(h, isel, jsel, gsel, peer_u[layer].astype(BF16).T, peer_v[layer].astype(BF16), xs,
                          gate2, n_ctx)
    return xs[:, n_ctx:]
```

```python
import functools
import math

import jax
import jax.numpy as jnp
from jax import lax
from jax.experimental import pallas as pl
from jax.experimental.pallas import tpu as pltpu

F32 = jnp.float32
BF16 = jnp.bfloat16
I32 = jnp.int32

V7X_LANES = 128
V7X_SUBLANES = 8
V7X_VMEM_LIMIT_BYTES = 56 * 1024 * 1024

GRID_W = 64
N_MOD = 6
A_HEAD_DIM = 64
DECAY_SCALE = math.exp(-0.5)
GN_EPS = 64e-5
DECAY_RANK = 64
ICLR_RANK = 64
GATE_RANK = 128
B_BLOCK_DIM = 64
CONV_W = 4
LRU_C = 8.0
C_QK_DIM = 64
C_V_DIM = 128
ROPE_BASE = 10000.0
N_KEYS = 128
PEER_HEADS = 8
PEER_TOPK = 16
RMS_EPS = 1e-6


def _cparams(sem):
    return pltpu.CompilerParams(dimension_semantics=sem, vmem_limit_bytes=V7X_VMEM_LIMIT_BYTES)


def _token_tile(n_ctx, n_lat, want):
    t = want
    while n_ctx % t or n_lat % t:
        t //= 2
    return t


def _mod_spec(d, ctx_tiles):
    return pl.BlockSpec((1, 1, 1, d), lambda b, j, *_: (b, jnp.where(j >= ctx_tiles, 1, 0), 0, 0))


def _rms_modulate(x, shift, scale):
    xn = x * lax.rsqrt(jnp.mean(x * x, axis=-1, keepdims=True) + RMS_EPS)
    return xn * (1.0 + scale) + shift


def _modmm_body(x_ref, sh_ref, sc_ref, w_ref, o_ref):
    z = _rms_modulate(x_ref[0], sh_ref[0, 0], sc_ref[0, 0])
    o_ref[0] = jnp.dot(z.astype(BF16), w_ref[...], preferred_element_type=F32)


def modulate_matmul(xs, shift, scale, w, n_ctx):
    b, l, d = xs.shape
    n = w.shape[1]
    tm = _token_tile(n_ctx, l - n_ctx, 256)
    ctx_tiles = n_ctx // tm
    return pl.pallas_call(
        _modmm_body,
        grid=(b, l // tm),
        in_specs=[
            pl.BlockSpec((1, tm, d), lambda i, j: (i, j, 0)),
            _mod_spec(d, ctx_tiles),
            _mod_spec(d, ctx_tiles),
            pl.BlockSpec((d, n), lambda i, j: (0, 0)),
        ],
        out_specs=pl.BlockSpec((1, tm, n), lambda i, j: (i, j, 0)),
        out_shape=jax.ShapeDtypeStruct((b, l, n), F32),
        compiler_params=_cparams(("parallel", "parallel")),
        name="modulate_matmul",
    )(xs, shift, scale, w.astype(BF16))


def _outproj_body(y_ref, x_ref, g_ref, w_ref, o_ref):
    acc = jnp.dot(y_ref[0].astype(BF16), w_ref[...], preferred_element_type=F32)
    o_ref[0] = x_ref[0] + g_ref[0, 0] * acc


def gated_out_proj(y, xs, gate, w, n_ctx):
    b, l, d = xs.shape
    k = y.shape[-1]
    tm = _token_tile(n_ctx, l - n_ctx, 256)
    ctx_tiles = n_ctx // tm
    return pl.pallas_call(
        _outproj_body,
        grid=(b, l // tm),
        in_specs=[
            pl.BlockSpec((1, tm, k), lambda i, j: (i, j, 0)),
            pl.BlockSpec((1, tm, d), lambda i, j: (i, j, 0)),
            _mod_spec(d, ctx_tiles),
            pl.BlockSpec((k, d), lambda i, j: (0, 0)),
        ],
        out_specs=pl.BlockSpec((1, tm, d), lambda i, j: (i, j, 0)),
        out_shape=jax.ShapeDtypeStruct((b, l, d), F32),
        compiler_params=_cparams(("parallel", "parallel")),
        name="gated_out_proj",
    )(y, xs, gate, w.astype(BF16))


RWKV_VBLK_A = 16
RWKV_VBLK_B = 8
RWKV_KPAR = 4
RWKV_CHAINS = (V7X_SUBLANES // RWKV_KPAR) * V7X_LANES
SCAN_TB = 8


def _reversed_block_map(n_blocks, ctx_blocks):
    def rev(i):
        return jnp.where(i < ctx_blocks, ctx_blocks - 1 - i, n_blocks - 1 - i + ctx_blocks)
    return rev


def _rwkv_scan_body(rf_ref, wf_ref, kf_ref, kkf_ref, akkf_ref, vf_ref,
                    rb_ref, wb_ref, kb_ref, kkb_ref, akkb_ref, vb_ref,
                    of_ref, ob_ref, tf_ref, tb_ref, sa_ref, *, tb, nk4, nv):
    @pl.when(pl.program_id(0) == 0)
    def _():
        tf_ref[...] = jnp.zeros_like(tf_ref)
        tb_ref[...] = jnp.zeros_like(tb_ref)

    def fold(acc):
        acc = acc + pltpu.roll(acc, V7X_SUBLANES // 2, axis=1)
        return acc + pltpu.roll(acc, V7X_SUBLANES // 4, axis=1)

    def one_direction(t, r_ref, w_ref, k_ref, kk_ref, akk_ref, v_ref, o_ref, t_ref):
        for vb in range(nv // RWKV_VBLK_A):
            vs = pl.ds(vb * RWKV_VBLK_A, RWKV_VBLK_A)

            def acc_sa(k4, acc):
                return acc + t_ref[k4, vs] * kk_ref[t, k4][None]

            acc = lax.fori_loop(0, nk4, acc_sa, jnp.zeros((RWKV_VBLK_A, V7X_SUBLANES, V7X_LANES), F32), unroll=4)
            sa_ref[vs] = fold(acc)
        for vb in range(nv // RWKV_VBLK_B):
            vs = pl.ds(vb * RWKV_VBLK_B, RWKV_VBLK_B)
            sa = sa_ref[vs]
            vv = v_ref[t, vs]

            def upd(k4, acc):
                tn = (t_ref[k4, vs] * w_ref[t, k4][None] - akk_ref[t, k4][None] * sa
                      + k_ref[t, k4][None] * vv)
                t_ref[k4, vs] = tn
                return acc + tn * r_ref[t, k4][None]

            acc = lax.fori_loop(0, nk4, upd, jnp.zeros((RWKV_VBLK_B, V7X_SUBLANES, V7X_LANES), F32), unroll=4)
            o_ref[t, vs] = fold(acc)

    def step(t, carry):
        one_direction(t, rf_ref, wf_ref, kf_ref, kkf_ref, akkf_ref, vf_ref, of_ref, tf_ref)
        one_direction(tb - 1 - t, rb_ref, wb_ref, kb_ref, kkb_ref, akkb_ref, vb_ref, ob_ref, tb_ref)
        return carry

    lax.fori_loop(0, tb, step, 0)


def rwkv_scan(r, w_f, w_b, k_f, k_b, kk, akk_f, akk_b, v, n_ctx):
    l, nk4 = r.shape[0], r.shape[1]
    nv = v.shape[1]
    tb = SCAN_TB
    rev = _reversed_block_map(l // tb, n_ctx // tb)
    kblk = (tb, nk4, V7X_SUBLANES, V7X_LANES)
    vblk = (tb, nv, V7X_SUBLANES, V7X_LANES)
    kf = pl.BlockSpec(kblk, lambda i: (i, 0, 0, 0))
    kb = pl.BlockSpec(kblk, lambda i: (rev(i), 0, 0, 0))
    vf = pl.BlockSpec(vblk, lambda i: (i, 0, 0, 0))
    vb = pl.BlockSpec(vblk, lambda i: (rev(i), 0, 0, 0))
    out = jax.ShapeDtypeStruct((l, nv, V7X_SUBLANES, V7X_LANES), F32)
    state = pltpu.VMEM((nk4, nv, V7X_SUBLANES, V7X_LANES), F32)
    return pl.pallas_call(
        functools.partial(_rwkv_scan_body, tb=tb, nk4=nk4, nv=nv),
        grid=(l // tb,),
        in_specs=[kf, kf, kf, kf, kf, vf, kb, kb, kb, kb, kb, vb],
        out_specs=[vf, vb],
        out_shape=[out, out],
        scratch_shapes=[state, state, pltpu.VMEM((nv, V7X_SUBLANES, V7X_LANES), F32)],
        compiler_params=_cparams(("arbitrary",)),
        name="rwkv_scan",
    )(r, w_f, k_f, kk, akk_f, v, r, w_b, k_b, kk, akk_b, v)


def _diag_scan_body(af_ref, uf_ref, ab_ref, ub_ref, of_ref, ob_ref, hf_ref, hb_ref, *, tb):
    @pl.when(pl.program_id(0) == 0)
    def _():
        hf_ref[...] = jnp.zeros_like(hf_ref)
        hb_ref[...] = jnp.zeros_like(hb_ref)

    def step(t, carry):
        hf, hb = carry
        hf = af_ref[:, t, :] * hf + uf_ref[:, t, :]
        of_ref[:, t, :] = hf
        tr = tb - 1 - t
        hb = ab_ref[:, tr, :] * hb + ub_ref[:, tr, :]
        ob_ref[:, tr, :] = hb
        return hf, hb

    hf, hb = lax.fori_loop(0, tb, step, (hf_ref[...], hb_ref[...]))
    hf_ref[...] = hf
    hb_ref[...] = hb


def diag_scan(a_f, u_f, a_b, u_b, n_ctx):
    b, l, c = a_f.shape
    tb = SCAN_TB
    rev = _reversed_block_map(l // tb, n_ctx // tb)
    fwd = pl.BlockSpec((b, tb, c), lambda i: (0, i, 0))
    bwd = pl.BlockSpec((b, tb, c), lambda i: (0, rev(i), 0))
    out = jax.ShapeDtypeStruct((b, l, c), F32)
    return pl.pallas_call(
        functools.partial(_diag_scan_body, tb=tb),
        grid=(l // tb,),
        in_specs=[fwd, fwd, bwd, bwd],
        out_specs=[fwd, bwd],
        out_shape=[out, out],
        scratch_shapes=[pltpu.VMEM((b, c), F32), pltpu.VMEM((b, c), F32)],
        compiler_params=_cparams(("arbitrary",)),
        name="diag_scan",
    )(a_f, u_f, a_b, u_b)


def _attn_body(lam_ref, q_ref, k_ref, v_ref, sg_ref, o_ref, *, ctx_tiles, n_ctx, scale, out_scale):
    j = pl.program_id(2)
    q = q_ref[0]
    lane = lax.broadcasted_iota(I32, q.shape, 1)
    qs = q * jnp.asarray(scale, BF16)
    q1 = jnp.where(lane < C_QK_DIM, qs, jnp.zeros_like(qs))
    q2 = jnp.where(lane >= C_QK_DIM, qs, jnp.zeros_like(qs))
    lam = lam_ref[0]

    def attend(kt, vb):
        def softmax_times_v(qm):
            s = jnp.dot(qm, kt, preferred_element_type=F32)
            e = jnp.exp(s - jnp.max(s, axis=-1, keepdims=True))
            pv = jnp.dot(e.astype(BF16), vb, preferred_element_type=F32)
            return pv / jnp.sum(e, axis=-1, keepdims=True)

        o = softmax_times_v(q1) - lam * softmax_times_v(q2)
        o = o * lax.rsqrt(jnp.mean(o * o, axis=-1, keepdims=True) + RMS_EPS)
        o_ref[0] = o * sg_ref[...] * out_scale

    @pl.when(j < ctx_tiles)
    def _():
        attend(k_ref[0, :, :n_ctx], v_ref[0, :n_ctx])

    @pl.when(j >= ctx_tiles)
    def _():
        attend(k_ref[0], v_ref[0])


def diff_attention_core(q, k_t, v, lam, sub_g, n_ctx, lam_init):
    b, l, width = q.shape
    heads = width // C_V_DIM
    assert n_ctx % V7X_LANES == 0
    tq = _token_tile(n_ctx, l - n_ctx, 256)
    ctx_tiles = n_ctx // tq
    scale = C_QK_DIM ** -0.5
    assert math.frexp(scale)[0] == 0.5, "the kernel folds the score scale into q, exact only for a power of two"
    body = functools.partial(_attn_body, ctx_tiles=ctx_tiles, n_ctx=n_ctx, scale=scale, out_scale=1.0 - lam_init)
    return pl.pallas_call(
        body,
        grid=(b, heads, l // tq),
        in_specs=[
            pl.BlockSpec(memory_space=pltpu.SMEM),
            pl.BlockSpec((1, tq, C_V_DIM), lambda i, h, j: (i, j, h)),
            pl.BlockSpec((1, C_V_DIM, l), lambda i, h, j: (i, h, 0)),
            pl.BlockSpec((1, l, C_V_DIM), lambda i, h, j: (i, 0, h)),
            pl.BlockSpec((1, C_V_DIM), lambda i, h, j: (0, 0)),
        ],
        out_specs=pl.BlockSpec((1, tq, C_V_DIM), lambda i, h, j: (i, j, h)),
        out_shape=jax.ShapeDtypeStruct((b, l, width), F32),
        compiler_params=_cparams(("parallel", "parallel", "parallel")),
        name="diff_attention",
    )(lam, q, k_t, v, sub_g)


def _peer_candidates():
    return [(a, b) for a in range(PEER_TOPK) for b in range(PEER_TOPK) if (a + 1) * (b + 1) <= PEER_TOPK]


def _segment_rows(mod_ref, tile, tm, n_ctx):
    row = tile * tm + lax.broadcasted_iota(I32, (tm, 1), 0)
    return jnp.where(row < n_ctx, mod_ref[0, 0], mod_ref[0, 1])


def _peer_route_body(x_ref, sh_ref, sc_ref, wq_ref, keys_ref, h_ref, i_ref, j_ref, g_ref,
                     sv_ref, si_ref, cand_ref, best_ref, bi_ref, bj_ref, s_ref, *, tm, n_ctx):
    tile = pl.program_id(1)
    h = _rms_modulate(x_ref[0], _segment_rows(sh_ref, tile, tm, n_ctx),
                      _segment_rows(sc_ref, tile, tm, n_ctx)).astype(BF16)
    h_ref[0] = h
    q = jnp.dot(h, wq_ref[...], preferred_element_type=F32).astype(BF16)
    nt = (((1,), (1,)), ((), ()))
    neg = jnp.float32(-jnp.inf)

    def best_of(nodes):
        while len(nodes) > 1:
            nxt = []
            for (va, ia), (vb, ib) in zip(nodes[0::2], nodes[1::2]):
                left = va >= vb
                nxt.append((jnp.maximum(va, vb), jnp.where(left, ia, ib)))
            nodes = nxt
        return nodes[0]

    for m in range(2):
        for col in range(tm // V7X_LANES):
            toks = slice(col * V7X_LANES, (col + 1) * V7X_LANES)
            per_head = [
                lax.dot_general(keys_ref[m], q[toks, (hd * 2 + m) * N_KEYS:(hd * 2 + m + 1) * N_KEYS], nt,
                                preferred_element_type=F32) for hd in range(PEER_HEADS)]
            s_ref[...] = jnp.swapaxes(jnp.stack(per_head, axis=0), 0, 1)

            def extract(a, carry, m=m, toks=toks):
                vals = [s_ref[n] for n in range(N_KEYS)]
                mx, idx = best_of([(vals[n], jnp.float32(n)) for n in range(N_KEYS)])
                sv_ref[m, a, :, toks] = mx
                si_ref[m, a, :, toks] = idx
                for n in range(N_KEYS):
                    s_ref[n] = jnp.where(idx == jnp.float32(n), neg, vals[n])
                return carry

            lax.fori_loop(0, PEER_TOPK, extract, 0)

    cands = _peer_candidates()
    for c, (a, b) in enumerate(cands):
        cand_ref[c] = sv_ref[0, a] + sv_ref[1, b]
    big = jnp.float32(PEER_TOPK * PEER_TOPK)

    def pick(k, carry):
        vals = [cand_ref[c] for c in range(len(cands))]
        mx = functools.reduce(jnp.maximum, vals)
        pos = functools.reduce(
            jnp.minimum,
            [jnp.where(vals[c] == mx, jnp.float32(a * PEER_TOPK + b), big) for c, (a, b) in enumerate(cands)])
        ii = jnp.zeros_like(pos)
        jj = jnp.zeros_like(pos)
        for c, (a, b) in enumerate(cands):
            hit = pos == jnp.float32(a * PEER_TOPK + b)
            cand_ref[c] = jnp.where(hit, neg, vals[c])
            ii = jnp.where(hit, si_ref[0, a], ii)
            jj = jnp.where(hit, si_ref[1, b], jj)
        best_ref[k] = mx
        bi_ref[k] = ii
        bj_ref[k] = jj
        return carry

    lax.fori_loop(0, PEER_TOPK, pick, 0)
    best = best_ref[...]
    e = jnp.exp(best - best[0][None])
    g = e / jnp.sum(e, axis=0, keepdims=True)
    g_ref[0] = g.reshape(PEER_TOPK * PEER_HEADS, tm)
    i_ref[0] = bi_ref[...].reshape(PEER_TOPK * PEER_HEADS, tm)
    j_ref[0] = bj_ref[...].reshape(PEER_TOPK * PEER_HEADS, tm)


PEER_TILE_TARGET = 512


def _peer_tile(l):
    t = (PEER_TILE_TARGET // V7X_LANES) * V7X_LANES
    while l % t:
        t -= V7X_LANES
    return t


def _mod_pair_spec(d):
    return pl.BlockSpec((1, 2, 1, d), lambda b, *_: (b, 0, 0, 0))


def peer_route(xs, shift, scale, w_q, keys, n_ctx):
    b, l, d = xs.shape
    tm = _peer_tile(l)
    nsel = PEER_TOPK * PEER_HEADS
    ncand = len(_peer_candidates())
    sel_spec = pl.BlockSpec((1, nsel, tm), lambda i, j: (i, 0, j))
    return pl.pallas_call(
        functools.partial(_peer_route_body, tm=tm, n_ctx=n_ctx),
        grid=(b, l // tm),
        in_specs=[
            pl.BlockSpec((1, tm, d), lambda i, j: (i, j, 0)),
            _mod_pair_spec(d),
            _mod_pair_spec(d),
            pl.BlockSpec(w_q.shape, lambda i, j: (0, 0)),
            pl.BlockSpec(keys.shape, lambda i, j: (0, 0, 0)),
        ],
        out_specs=[pl.BlockSpec((1, tm, d), lambda i, j: (i, j, 0)), sel_spec, sel_spec, sel_spec],
        out_shape=[
            jax.ShapeDtypeStruct((b, l, d), BF16),
            jax.ShapeDtypeStruct((b, nsel, l), F32),
            jax.ShapeDtypeStruct((b, nsel, l), F32),
            jax.ShapeDtypeStruct((b, nsel, l), F32),
        ],
        scratch_shapes=[
            pltpu.VMEM((2, PEER_TOPK, PEER_HEADS, tm), F32),
            pltpu.VMEM((2, PEER_TOPK, PEER_HEADS, tm), F32),
            pltpu.VMEM((ncand, PEER_HEADS, tm), F32),
            pltpu.VMEM((PEER_TOPK, PEER_HEADS, tm), F32),
            pltpu.VMEM((PEER_TOPK, PEER_HEADS, tm), F32),
            pltpu.VMEM((PEER_TOPK, PEER_HEADS, tm), F32),
            pltpu.VMEM((N_KEYS, PEER_HEADS, V7X_LANES), F32),
        ],
        compiler_params=_cparams(("parallel", "parallel")),
        name="peer_route",
    )(xs, shift, scale, w_q.astype(BF16), keys.astype(BF16))


PEER_G_CHUNK = 32
PEER_PAIR = 2 * N_KEYS
PEER_EXPERT_BLOCK = 2048


def _peer_expert_body(h_ref, i_ref, j_ref, g_ref, u_ref, v_ref, x_ref, gate_ref, o_ref, gw_ref, acc_ref, wt_ref,
                      *, tm, eb, n_ctx):
    e = pl.program_id(2)

    @pl.when(e == 0)
    def _():
        acc_ref[...] = jnp.zeros_like(acc_ref)
        isel = i_ref[0].T
        jsel = j_ref[0].T
        gsel = g_ref[0].T
        kio = lax.broadcasted_iota(I32, (PEER_G_CHUNK, N_KEYS, isel.shape[1]), 1).astype(F32)
        for c in range(tm // PEER_G_CHUNK):
            rows = slice(c * PEER_G_CHUNK, (c + 1) * PEER_G_CHUNK)
            oh_i = jnp.where(isel[rows][:, None, :] == kio, 1.0, 0.0).astype(BF16)
            oh_j = jnp.where(jsel[rows][:, None, :] == kio, gsel[rows][:, None, :], 0.0).astype(BF16)
            g3 = jnp.einsum("pis,pjs->pij", oh_i, oh_j, preferred_element_type=F32)
            gw_ref[:, rows, :] = jnp.swapaxes(g3, 0, 1).astype(BF16)

    h = h_ref[0]
    for ip in range(eb // PEER_PAIR):
        rows = slice(ip * PEER_PAIR, (ip + 1) * PEER_PAIR)
        act = jax.nn.gelu(jnp.dot(h, u_ref[:, rows], preferred_element_type=F32)).astype(BF16)
        i0 = e * (eb // N_KEYS) + ip * 2
        wt_ref[:, ip * PEER_PAIR:ip * PEER_PAIR + N_KEYS] = act[:, :N_KEYS] * gw_ref[i0]
        wt_ref[:, ip * PEER_PAIR + N_KEYS:(ip + 1) * PEER_PAIR] = act[:, N_KEYS:] * gw_ref[i0 + 1]
    acc_ref[...] += jnp.dot(wt_ref[...], v_ref[...], preferred_element_type=F32)

    @pl.when(e == pl.num_programs(2) - 1)
    def _():
        gate = _segment_rows(gate_ref, pl.program_id(1), tm, n_ctx)
        o_ref[0] = x_ref[0] + gate * acc_ref[...]


def peer_experts(h, isel, jsel, gsel, u_t, v_tab, xs, gate, n_ctx):
    b, l, d = xs.shape
    ne = v_tab.shape[0]
    tm = _peer_tile(l)
    eb = PEER_EXPERT_BLOCK if tm < PEER_TILE_TARGET else PEER_EXPERT_BLOCK // 2
    nsel = isel.shape[1]
    sel_spec = pl.BlockSpec((1, nsel, tm), lambda i, j, e: (i, 0, j))
    tok_spec = pl.BlockSpec((1, tm, d), lambda i, j, e: (i, j, 0))
    return pl.pallas_call(
        functools.partial(_peer_expert_body, tm=tm, eb=eb, n_ctx=n_ctx),
        grid=(b, l // tm, ne // eb),
        in_specs=[
            tok_spec, sel_spec, sel_spec, sel_spec,
            pl.BlockSpec((d, eb), lambda i, j, e: (0, e)),
            pl.BlockSpec((eb, d), lambda i, j, e: (e, 0)),
            tok_spec,
            _mod_pair_spec(d),
        ],
        out_specs=tok_spec,
        out_shape=jax.ShapeDtypeStruct((b, l, d), F32),
        scratch_shapes=[
            pltpu.VMEM((N_KEYS, tm, N_KEYS), BF16),
            pltpu.VMEM((tm, d), F32),
            pltpu.VMEM((tm, eb), BF16),
        ],
        compiler_params=_cparams(("parallel", "parallel", "arbitrary")),
        name="peer_experts",
    )(h, isel, jsel, gsel, u_t, v_tab, xs, gate)


def _group_ones(width, group):
    g = jnp.arange(width) // group
    return (g[:, None] == g[None, :]).astype(F32)


def _group_sum(x, ones_ref):
    return jnp.dot(x, ones_ref[...], precision=lax.Precision.HIGHEST, preferred_element_type=F32)


def _pad_rank_rows(w, lo, total):
    return jnp.pad(w, ((lo, total - lo - w.shape[0]), (0, 0)))


HALO_ROWS = V7X_SUBLANES


def _even_prep_body(pm_ref, pp_ref, pn_ref, mu_ref, rvec_ref, wup_ref, aup_ref, gup_ref, ones_ref,
                    cw_ref, wa_ref, wx_ref, lvec_ref,
                    r_ref, decf_ref, decb_ref, ktf_ref, ktb_ref, kk_ref, akkf_ref, akkb_ref, v_ref,
                    bonus_ref, ga_ref, af_ref, uf_ref, ab_ref, ub_ref, gb_ref,
                    *, tm, ctx_tiles, n_tiles, width, a_proj):
    j = pl.program_id(1)
    first = jnp.logical_or(j == 0, j == ctx_tiles)
    last = jnp.logical_or(j == ctx_tiles - 1, j == n_tiles - 1)
    p = pm_ref[0]
    prev = jnp.where(first, 0.0, pp_ref[0])
    nxt = jnp.where(last, 0.0, pn_ref[0])
    row = lax.broadcasted_iota(I32, (tm, 1), 0)

    def shifted(lo, hi, off):
        x = p[:, lo:hi]
        if off < 0:
            y = pltpu.roll(x, -off, axis=0)
            for s in range(-off):
                y = jnp.where(row == s, prev[HALO_ROWS + off + s:HALO_ROWS + off + s + 1, lo:hi], y)
        else:
            y = pltpu.roll(x, tm - off, axis=0)
            for s in range(off):
                y = jnp.where(row == tm - off + s, nxt[s:s + 1, lo:hi], y)
        return y

    pa = p[:, :a_proj]
    pa = pa + mu_ref[...] * (0.5 * (shifted(0, a_proj, -1) + shifted(0, a_proj, 1)) - pa)
    r = pa[:, :width]
    k = pa[:, width:2 * width]
    v = pa[:, 2 * width:3 * width]
    c0 = 3 * width
    wd = jnp.tanh(pa[:, c0:c0 + V7X_LANES]).astype(BF16)
    ad = pa[:, c0 + V7X_LANES:c0 + 2 * V7X_LANES].astype(BF16)
    gd = jax.nn.sigmoid(pa[:, c0 + 2 * V7X_LANES:c0 + 3 * V7X_LANES]).astype(BF16)
    w0_f, w0_b, a0_f, a0_b, k_k, k_a, r_k = (rvec_ref[n:n + 1] for n in range(7))

    kk = k * k_k
    kk = kk / jnp.maximum(jnp.sqrt(_group_sum(kk * kk, ones_ref)), 1e-12)
    kts = []
    for w0, a0, d, dec_ref, kt_ref, akk_ref in ((w0_f, a0_f, 0, decf_ref, ktf_ref, akkf_ref),
                                                (w0_b, a0_b, 1, decb_ref, ktb_ref, akkb_ref)):
        lora_w = jnp.dot(wd, wup_ref[d], preferred_element_type=F32)
        dec_ref[0] = jnp.exp(-DECAY_SCALE * jax.nn.sigmoid(w0 + lora_w))
        a = jax.nn.sigmoid(a0 + jnp.dot(ad, aup_ref[d], preferred_element_type=F32))
        kt = k * (1.0 + (a - 1.0) * k_a)
        kt_ref[0] = kt
        akk_ref[0] = a * kk
        kts.append(kt)
    r_ref[0] = r
    kk_ref[0] = kk
    v_ref[0] = v
    bonus_ref[0] = _group_sum(r * (0.5 * (kts[0] + kts[1])) * r_k, ones_ref) * v
    ga_ref[0] = jnp.dot(gd, gup_ref[...], preferred_element_type=F32)

    b0 = a_proj
    conv_b, ba_f, ba_b, bx_f, bx_b, sp_f, sp_b = (lvec_ref[n:n + 1] for n in range(7))
    xb = (cw_ref[0:1] * shifted(b0, b0 + width, -2) + cw_ref[1:2] * shifted(b0, b0 + width, -1)
          + cw_ref[2:3] * p[:, b0:b0 + width] + cw_ref[3:4] * shifted(b0, b0 + width, 1) + conv_b)
    gb_ref[0] = jax.nn.gelu(p[:, b0 + width:b0 + 2 * width])
    xbb = xb.astype(BF16)
    for d, ba, bx, sp, a_ref, u_ref in ((0, ba_f, bx_f, sp_f, af_ref, uf_ref), (1, ba_b, bx_b, sp_b, ab_ref, ub_ref)):
        rg = jax.nn.sigmoid(jnp.dot(xbb, wa_ref[d], preferred_element_type=F32) + ba)
        ig = jax.nn.sigmoid(jnp.dot(xbb, wx_ref[d], preferred_element_type=F32) + bx)
        log_a = -LRU_C * rg * sp
        a_ref[0] = jnp.exp(log_a)
        th = jnp.tanh(log_a)
        u_ref[0] = jnp.sqrt(-2.0 * th / (1.0 - th)) * ig * xb


def even_prep(p, n_ctx, mu, w0, w_up, a0, a_up, g_up, k_k, k_a, r_k, conv_w, conv_b, wa, ba, wx, bx, lam):
    b, l, n_proj = p.shape
    width = w0.shape[-1]
    a_proj = mu.shape[-1]
    assert CONV_W == 4 and 2 * DECAY_RANK == V7X_LANES and 2 * ICLR_RANK == V7X_LANES and GATE_RANK == V7X_LANES
    assert a_proj == 3 * width + 3 * V7X_LANES and n_proj == a_proj + 2 * width
    tm = _token_tile(n_ctx, l - n_ctx, 256)
    n_tiles, ctx_tiles = l // tm, n_ctx // tm
    halo_per_tile = tm // HALO_ROWS
    last_halo = l // HALO_ROWS - 1
    rvec = jnp.concatenate([w0, a0, k_k[None], k_a[None], r_k.reshape(1, width), jnp.zeros((1, width), F32)], axis=0)
    lvec = jnp.concatenate([conv_b[None], ba, bx, jax.nn.softplus(-lam), jnp.zeros((1, width), F32)], axis=0)
    wup = jnp.stack([_pad_rank_rows(w_up[0], 0, V7X_LANES), _pad_rank_rows(w_up[1], DECAY_RANK, V7X_LANES)])
    aup = jnp.stack([_pad_rank_rows(a_up[0], 0, V7X_LANES), _pad_rank_rows(a_up[1], ICLR_RANK, V7X_LANES)])

    def block_diag(w):
        return jnp.stack([jax.scipy.linalg.block_diag(*w[d]) for d in range(2)]).astype(BF16)

    tok = pl.BlockSpec((1, tm, width), lambda i, j: (i, j, 0))

    def full(a):
        return pl.BlockSpec(a.shape, lambda i, j, nd=a.ndim: (0,) * nd)

    consts = [mu[None], rvec, wup.astype(BF16), aup.astype(BF16), g_up.astype(BF16),
              _group_ones(width, A_HEAD_DIM), conv_w, block_diag(wa), block_diag(wx), lvec]
    return pl.pallas_call(
        functools.partial(_even_prep_body, tm=tm, ctx_tiles=ctx_tiles, n_tiles=n_tiles, width=width, a_proj=a_proj),
        grid=(b, n_tiles),
        in_specs=[
            pl.BlockSpec((1, tm, n_proj), lambda i, j: (i, j, 0)),
            pl.BlockSpec((1, HALO_ROWS, n_proj), lambda i, j: (i, jnp.maximum(j * halo_per_tile - 1, 0), 0)),
            pl.BlockSpec((1, HALO_ROWS, n_proj), lambda i, j: (i, jnp.minimum((j + 1) * halo_per_tile, last_halo), 0)),
        ] + [full(a) for a in consts],
        out_specs=[tok] * 16,
        out_shape=[jax.ShapeDtypeStruct((b, l, width), F32)] * 16,
        compiler_params=_cparams(("parallel", "parallel")),
        name="even_prep",
    )(p, p, p, *consts)


def _even_post_body(of_ref, ob_ref, bonus_ref, ga_ref, hf_ref, hb_ref, gb_ref, ln_ref, ones_ref,
                    x_ref, g_ref, w_ref, o_ref, *, width):
    o = of_ref[0] + ob_ref[0]
    inv_n = 1.0 / A_HEAD_DIM
    cen = o - _group_sum(o, ones_ref) * inv_n
    var = _group_sum(cen * cen, ones_ref) * inv_n
    on = cen * lax.rsqrt(var + GN_EPS) * ln_ref[0:1] + ln_ref[1:2]
    ya = ((on + bonus_ref[0]) * ga_ref[0]).astype(BF16)
    yb = ((hf_ref[0] + hb_ref[0]) * gb_ref[0]).astype(BF16)
    acc = (jnp.dot(ya, w_ref[:width], preferred_element_type=F32)
           + jnp.dot(yb, w_ref[width:], preferred_element_type=F32))
    o_ref[0] = x_ref[0] + g_ref[0, 0] * acc


def even_post(o_f, o_b, bonus, gate_a, h_f, h_b, gate_b, ln_w, ln_b, xs, gate, w_out, n_ctx):
    b, l, d = xs.shape
    width = o_f.shape[-1]
    tm = _token_tile(n_ctx, l - n_ctx, 256)
    ctx_tiles = n_ctx // tm
    tok = pl.BlockSpec((1, tm, width), lambda i, j: (i, j, 0))
    ln = jnp.stack([ln_w, ln_b])
    ones = _group_ones(width, A_HEAD_DIM)
    return pl.pallas_call(
        functools.partial(_even_post_body, width=width),
        grid=(b, l // tm),
        in_specs=[tok] * 7 + [
            pl.BlockSpec(ln.shape, lambda i, j: (0, 0)),
            pl.BlockSpec(ones.shape, lambda i, j: (0, 0)),
            pl.BlockSpec((1, tm, d), lambda i, j: (i, j, 0)),
            _mod_spec(d, ctx_tiles),
            pl.BlockSpec(w_out.shape, lambda i, j: (0, 0)),
        ],
        out_specs=pl.BlockSpec((1, tm, d), lambda i, j: (i, j, 0)),
        out_shape=jax.ShapeDtypeStruct((b, l, d), F32),
        compiler_params=_cparams(("parallel", "parallel")),
        name="even_post",
    )(o_f, o_b, bonus, gate_a, h_f, h_b, gate_b, ln, ones, xs, gate, w_out.astype(BF16))


def _rwkv_chain_layout(z, heads):
    b, l, width = z.shape
    n = width // heads
    z = z.reshape(b, l, heads, n).transpose(1, 3, 0, 2).reshape(l, n, b * heads)
    z = jnp.pad(z, ((0, 0), (0, 0), (0, RWKV_CHAINS - b * heads)))
    return z.reshape(l, n, V7X_SUBLANES // RWKV_KPAR, V7X_LANES)


def _rwkv_keyed(z, heads):
    z = _rwkv_chain_layout(z, heads)
    l, n = z.shape[:2]
    return z.reshape(l, n // RWKV_KPAR, V7X_SUBLANES, V7X_LANES)


def _rwkv_valued(z, heads):
    z = _rwkv_chain_layout(z, heads)
    return jnp.concatenate([z] * RWKV_KPAR, axis=2)


def _rwkv_unchain(o, b, heads):
    l, n = o.shape[:2]
    o = o[:, :, :V7X_SUBLANES // RWKV_KPAR].reshape(l, n, RWKV_CHAINS)[:, :, :b * heads]
    return o.reshape(l, n, b, heads).transpose(2, 0, 3, 1).reshape(b, l, heads * n)


def _even_mixer(xs, shift, scale, gate, n_ctx, w_in, mu, w_out, w0, w_up, a0, a_up, g_up, k_k, k_a, r_k, ln_w, ln_b,
                conv_w, conv_b, wa, ba, wx, bx, lam):
    b = xs.shape[0]
    heads = w0.shape[-1] // A_HEAD_DIM
    assert b * heads <= RWKV_CHAINS
    p = modulate_matmul(xs, shift, scale, w_in, n_ctx)
    (r, dec_f, dec_b, kt_f, kt_b, kk, akk_f, akk_b, v, bonus, gate_a, a_f, u_f, a_b, u_b, gate_b) = even_prep(
        p, n_ctx, mu, w0, w_up, a0, a_up, g_up, k_k, k_a, r_k, conv_w, conv_b, wa, ba, wx, bx, lam)
    o_f, o_b = rwkv_scan(
        _rwkv_keyed(r, heads), _rwkv_keyed(dec_f, heads), _rwkv_keyed(dec_b, heads),
        _rwkv_keyed(kt_f, heads), _rwkv_keyed(kt_b, heads), _rwkv_keyed(kk, heads),
        _rwkv_keyed(akk_f, heads), _rwkv_keyed(akk_b, heads), _rwkv_valued(v, heads), n_ctx)
    h_f, h_b = diag_scan(a_f, u_f, a_b, u_b, n_ctx)
    return even_post(_rwkv_unchain(o_f, b, heads), _rwkv_unchain(o_b, b, heads), bonus, gate_a, h_f, h_b, gate_b,
                     ln_w, ln_b, xs, gate, w_out, n_ctx)


ROPE_HALF = C_QK_DIM // 4


def _qkv_body(x_ref, sh_ref, sc_ref, w_ref, ones_ref, g_ref, cos_ref, sin_ref, q_ref, k_ref, v_ref, *, width):
    z = _rms_modulate(x_ref[0], sh_ref[0, 0], sc_ref[0, 0])
    qkv = jnp.dot(z.astype(BF16), w_ref[...], preferred_element_type=F32)
    cos = cos_ref[...]
    sin = sin_ref[...]
    lane = lax.broadcasted_iota(I32, cos.shape, 1)
    first_half = (lane % (2 * ROPE_HALF)) < ROPE_HALF
    for part, out_ref in ((0, q_ref), (1, k_ref)):
        gain = g_ref[part:part + 1]
        for c in range(width // V7X_LANES):
            lo = part * width + c * V7X_LANES
            t = qkv[:, lo:lo + V7X_LANES]
            ms = _group_sum(t * t, ones_ref) * (1.0 / C_QK_DIM)
            t = t * lax.rsqrt(ms + RMS_EPS) * gain
            partner = jnp.where(first_half, pltpu.roll(t, V7X_LANES - ROPE_HALF, axis=1),
                                pltpu.roll(t, ROPE_HALF, axis=1))
            t = t * cos + partner * sin
            if part == 0:
                out_ref[0, :, c * V7X_LANES:(c + 1) * V7X_LANES] = t.astype(BF16)
            else:
                out_ref[0, c * V7X_LANES:(c + 1) * V7X_LANES, :] = t.T.astype(BF16)
    v_ref[0] = qkv[:, 2 * width:].astype(BF16)


def _rope_tables(n_ctx, n_lat):
    n_rows = n_lat // GRID_W
    row_pos = jnp.repeat(jnp.arange(n_rows), GRID_W).astype(F32)
    col_pos = jnp.tile(jnp.arange(GRID_W), n_rows).astype(F32)
    inv_freq = ROPE_BASE ** (-jnp.arange(ROPE_HALF, dtype=F32) / ROPE_HALF)

    def one(pos):
        ang = pos[:, None] * inv_freq
        c, s = jnp.cos(ang), jnp.sin(ang)
        return jnp.concatenate([c, c], axis=-1), jnp.concatenate([-s, s], axis=-1)

    (cr, sr), (cc, sc) = one(row_pos), one(col_pos)
    cos = jnp.concatenate([cr, cc], axis=-1)
    sin = jnp.concatenate([sr, sc], axis=-1)
    cos = jnp.concatenate([jnp.ones((n_ctx, C_QK_DIM), F32), cos], axis=0)
    sin = jnp.concatenate([jnp.zeros((n_ctx, C_QK_DIM), F32), sin], axis=0)
    reps = V7X_LANES // C_QK_DIM
    return jnp.tile(cos, (1, reps)), jnp.tile(sin, (1, reps))


def qkv_project(xs, shift, scale, w_qkv, q_g, k_g, n_ctx):
    b, l, d = xs.shape
    width = w_qkv.shape[1] // 3
    tm = _token_tile(n_ctx, l - n_ctx, 256)
    ctx_tiles = n_ctx // tm
    cos, sin = _rope_tables(n_ctx, l - n_ctx)
    gains = jnp.stack([jnp.tile(q_g, V7X_LANES // C_QK_DIM), jnp.tile(k_g, V7X_LANES // C_QK_DIM)])
    ones = _group_ones(V7X_LANES, C_QK_DIM)
    out = pl.BlockSpec((1, tm, width), lambda i, j: (i, j, 0))
    return pl.pallas_call(
        functools.partial(_qkv_body, width=width),
        grid=(b, l // tm),
        in_specs=[
            pl.BlockSpec((1, tm, d), lambda i, j: (i, j, 0)),
            _mod_spec(d, ctx_tiles),
            _mod_spec(d, ctx_tiles),
            pl.BlockSpec(w_qkv.shape, lambda i, j: (0, 0)),
            pl.BlockSpec(ones.shape, lambda i, j: (0, 0)),
            pl.BlockSpec(gains.shape, lambda i, j: (0, 0)),
            pl.BlockSpec((tm, V7X_LANES), lambda i, j: (j, 0)),
            pl.BlockSpec((tm, V7X_LANES), lambda i, j: (j, 0)),
        ],
        out_specs=[out, pl.BlockSpec((1, width, tm), lambda i, j: (i, 0, j)), out],
        out_shape=[jax.ShapeDtypeStruct((b, l, width), BF16), jax.ShapeDtypeStruct((b, width, l), BF16),
                   jax.ShapeDtypeStruct((b, l, width), BF16)],
        compiler_params=_cparams(("parallel", "parallel")),
        name="qkv_project",
    )(xs, shift, scale, w_qkv.astype(BF16), ones, gains, cos, sin)


def kernel(x, c, ctx, c_ctx, w_mod, b_mod, even_w_in, even_mu, even_w_out, rwkv_w0, rwkv_w_up, rwkv_a0, rwkv_a_up, rwkv_g_up, rwkv_k_k, rwkv_k_a, rwkv_r_k, rwkv_ln_w, rwkv_ln_b, lru_conv_w, lru_conv_b, lru_wa, lru_ba, lru_wx, lru_bx, lru_lam, attn_w_qkv, attn_q_g, attn_k_g, attn_lam_q1, attn_lam_k1, attn_lam_q2, attn_lam_k2, attn_sub_g, attn_w_o, peer_w_q, peer_keys, peer_u, peer_v):
    bsz, n_lat, d = x.shape
    n_ctx = ctx.shape[1]
    depth = w_mod.shape[0]
    a_proj = even_mu.shape[-1]
    s_lat = jax.nn.silu(c)
    s_ctx = jax.nn.silu(c_ctx)
    xs = jnp.concatenate([ctx, x], axis=1)
    for layer in range(depth):
        i = layer // 2
        m_l = s_lat @ w_mod[layer] + b_mod[layer]
        m_c = s_ctx @ w_mod[layer] + b_mod[layer]
        mods = jnp.stack([jnp.broadcast_to(m_c, m_l.shape), m_l], axis=1).reshape(bsz, 2, N_MOD, 1, d)
        shift1, scale1, gate1, shift2, scale2, gate2 = (mods[:, :, n] for n in range(N_MOD))
        if layer % 2 == 0:
            xs = _even_mixer(xs, shift1, scale1, gate1, n_ctx, even_w_in[i], even_mu[i], even_w_out[i],
                             rwkv_w0[i], rwkv_w_up[i], rwkv_a0[i], rwkv_a_up[i], rwkv_g_up[i], rwkv_k_k[i],
                             rwkv_k_a[i], rwkv_r_k[i], rwkv_ln_w[i], rwkv_ln_b[i], lru_conv_w[i], lru_conv_b[i],
                             lru_wa[i], lru_ba[i], lru_wx[i], lru_bx[i], lru_lam[i])
        else:
            lam_init = 0.8 - 0.6 * math.exp(-0.3 * layer)
            q, k, v = qkv_project(xs, shift1, scale1, attn_w_qkv[i], attn_q_g[i], attn_k_g[i], n_ctx)
            lam = (jnp.exp(jnp.sum(attn_lam_q1[i] * attn_lam_k1[i]))
                   - jnp.exp(jnp.sum(attn_lam_q2[i] * attn_lam_k2[i])) + lam_init)
            o = diff_attention_core(q, k, v, lam.reshape(1), attn_sub_g[i].reshape(1, C_V_DIM), n_ctx, lam_init)
            xs = gated_out_proj(o, xs, gate1, attn_w_o[i], n_ctx)
        seg_ctx = n_ctx if layer < depth - 1 else 0
        if layer == depth - 1:
            xs = xs[:, n_ctx:]
        h, isel, jsel, gsel = peer_route(xs, shift2, scale2, peer_w_q[layer], peer_keys[layer], seg_ctx)
        xs = peer_experts(h, isel, jsel, gsel, peer_u[layer].astype(BF16).T, peer_v[layer].astype(BF16), xs,
                          gate2, seg_ctx)
    return xs
```

```python
import functools
import math

import jax
import jax.numpy as jnp
from jax import lax
from jax.experimental import pallas as pl
from jax.experimental.pallas import tpu as pltpu

F32 = jnp.float32
BF16 = jnp.bfloat16
I32 = jnp.int32

V7X_LANES = 128
V7X_SUBLANES = 8
V7X_VMEM_LIMIT_BYTES = 56 * 1024 * 1024

GRID_W = 64
N_MOD = 6
A_HEAD_DIM = 64
DECAY_SCALE = math.exp(-0.5)
GN_EPS = 64e-5
DECAY_RANK = 64
ICLR_RANK = 64
GATE_RANK = 128
B_BLOCK_DIM = 64
CONV_W = 4
LRU_C = 8.0
C_QK_DIM = 64
C_V_DIM = 128
ROPE_BASE = 10000.0
N_KEYS = 128
PEER_HEADS = 8
PEER_TOPK = 16
RMS_EPS = 1e-6


def _cparams(sem):
    return pltpu.CompilerParams(dimension_semantics=sem, vmem_limit_bytes=V7X_VMEM_LIMIT_BYTES)


def _token_tile(n_ctx, n_lat, want):
    t = want
    while n_ctx % t or n_lat % t:
        t //= 2
    return t


def _mod_spec(d, ctx_tiles):
    return pl.BlockSpec((1, 1, 1, d), lambda b, j, *_: (b, jnp.where(j >= ctx_tiles, 1, 0), 0, 0))


def _rms_modulate(x, shift, scale):
    xn = x * lax.rsqrt(jnp.mean(x * x, axis=-1, keepdims=True) + RMS_EPS)
    return xn * (1.0 + scale) + shift


def _modmm_body(x_ref, sh_ref, sc_ref, w_ref, o_ref):
    z = _rms_modulate(x_ref[0], sh_ref[0, 0], sc_ref[0, 0])
    o_ref[0] = jnp.dot(z.astype(BF16), w_ref[...], preferred_element_type=F32)


def modulate_matmul(xs, shift, scale, w, n_ctx):
    b, l, d = xs.shape
    n = w.shape[1]
    tm = _token_tile(n_ctx, l - n_ctx, 256)
    ctx_tiles = n_ctx // tm
    return pl.pallas_call(
        _modmm_body,
        grid=(b, l // tm),
        in_specs=[
            pl.BlockSpec((1, tm, d), lambda i, j: (i, j, 0)),
            _mod_spec(d, ctx_tiles),
            _mod_spec(d, ctx_tiles),
            pl.BlockSpec((d, n), lambda i, j: (0, 0)),
        ],
        out_specs=pl.BlockSpec((1, tm, n), lambda i, j: (i, j, 0)),
        out_shape=jax.ShapeDtypeStruct((b, l, n), F32),
        compiler_params=_cparams(("parallel", "parallel")),
        name="modulate_matmul",
    )(xs, shift, scale, w.astype(BF16))


def _outproj_body(y_ref, x_ref, g_ref, w_ref, o_ref):
    acc = jnp.dot(y_ref[0].astype(BF16), w_ref[...], preferred_element_type=F32)
    o_ref[0] = x_ref[0] + g_ref[0, 0] * acc


def gated_out_proj(y, xs, gate, w, n_ctx):
    b, l, d = xs.shape
    k = y.shape[-1]
    tm = _token_tile(n_ctx, l - n_ctx, 256)
    ctx_tiles = n_ctx // tm
    return pl.pallas_call(
        _outproj_body,
        grid=(b, l // tm),
        in_specs=[
            pl.BlockSpec((1, tm, k), lambda i, j: (i, j, 0)),
            pl.BlockSpec((1, tm, d), lambda i, j: (i, j, 0)),
            _mod_spec(d, ctx_tiles),
            pl.BlockSpec((k, d), lambda i, j: (0, 0)),
        ],
        out_specs=pl.BlockSpec((1, tm, d), lambda i, j: (i, j, 0)),
        out_shape=jax.ShapeDtypeStruct((b, l, d), F32),
        compiler_params=_cparams(("parallel", "parallel")),
        name="gated_out_proj",
    )(y, xs, gate, w.astype(BF16))


RWKV_VBLK_A = 16
RWKV_VBLK_B = 8
RWKV_KPAR = 4
RWKV_CHAINS = (V7X_SUBLANES // RWKV_KPAR) * V7X_LANES
SCAN_TB = 8


def _reversed_block_map(n_blocks, ctx_blocks):
    def rev(i):
        return jnp.where(i < ctx_blocks, ctx_blocks - 1 - i, n_blocks - 1 - i + ctx_blocks)
    return rev


def _rwkv_scan_body(rf_ref, wf_ref, kf_ref, kkf_ref, akkf_ref, vf_ref,
                    rb_ref, wb_ref, kb_ref, kkb_ref, akkb_ref, vb_ref,
                    of_ref, ob_ref, tf_ref, tb_ref, sa_ref, *, tb, nk4, nv):
    @pl.when(pl.program_id(0) == 0)
    def _():
        tf_ref[...] = jnp.zeros_like(tf_ref)
        tb_ref[...] = jnp.zeros_like(tb_ref)

    def fold(acc):
        acc = acc + pltpu.roll(acc, V7X_SUBLANES // 2, axis=1)
        return acc + pltpu.roll(acc, V7X_SUBLANES // 4, axis=1)

    def one_direction(t, r_ref, w_ref, k_ref, kk_ref, akk_ref, v_ref, o_ref, t_ref):
        for vb in range(nv // RWKV_VBLK_A):
            vs = pl.ds(vb * RWKV_VBLK_A, RWKV_VBLK_A)

            def acc_sa(k4, acc):
                return acc + t_ref[k4, vs] * kk_ref[t, k4][None]

            acc = lax.fori_loop(0, nk4, acc_sa, jnp.zeros((RWKV_VBLK_A, V7X_SUBLANES, V7X_LANES), F32), unroll=True)
            sa_ref[vs] = fold(acc)
        for vb in range(nv // RWKV_VBLK_B):
            vs = pl.ds(vb * RWKV_VBLK_B, RWKV_VBLK_B)
            sa = sa_ref[vs]
            vv = v_ref[t, vs]

            def upd(k4, acc):
                tn = (t_ref[k4, vs] * w_ref[t, k4][None] - akk_ref[t, k4][None] * sa
                      + k_ref[t, k4][None] * vv)
                t_ref[k4, vs] = tn
                return acc + tn * r_ref[t, k4][None]

            acc = lax.fori_loop(0, nk4, upd, jnp.zeros((RWKV_VBLK_B, V7X_SUBLANES, V7X_LANES), F32), unroll=True)
            o_ref[t, vs] = fold(acc)

    def step(t, carry):
        one_direction(t, rf_ref, wf_ref, kf_ref, kkf_ref, akkf_ref, vf_ref, of_ref, tf_ref)
        one_direction(tb - 1 - t, rb_ref, wb_ref, kb_ref, kkb_ref, akkb_ref, vb_ref, ob_ref, tb_ref)
        return carry

    lax.fori_loop(0, tb, step, 0)


def rwkv_scan(r, w_f, w_b, k_f, k_b, kk, akk_f, akk_b, v, n_ctx):
    l, nk4 = r.shape[0], r.shape[1]
    nv = v.shape[1]
    tb = SCAN_TB
    rev = _reversed_block_map(l // tb, n_ctx // tb)
    kblk = (tb, nk4, V7X_SUBLANES, V7X_LANES)
    vblk = (tb, nv, V7X_SUBLANES, V7X_LANES)
    kf = pl.BlockSpec(kblk, lambda i: (i, 0, 0, 0))
    kb = pl.BlockSpec(kblk, lambda i: (rev(i), 0, 0, 0))
    vf = pl.BlockSpec(vblk, lambda i: (i, 0, 0, 0))
    vb = pl.BlockSpec(vblk, lambda i: (rev(i), 0, 0, 0))
    out = jax.ShapeDtypeStruct((l, nv, V7X_SUBLANES, V7X_LANES), F32)
    state = pltpu.VMEM((nk4, nv, V7X_SUBLANES, V7X_LANES), F32)
    return pl.pallas_call(
        functools.partial(_rwkv_scan_body, tb=tb, nk4=nk4, nv=nv),
        grid=(l // tb,),
        in_specs=[kf, kf, kf, kf, kf, vf, kb, kb, kb, kb, kb, vb],
        out_specs=[vf, vb],
        out_shape=[out, out],
        scratch_shapes=[state, state, pltpu.VMEM((nv, V7X_SUBLANES, V7X_LANES), F32)],
        compiler_params=_cparams(("arbitrary",)),
        name="rwkv_scan",
    )(r, w_f, k_f, kk, akk_f, v, r, w_b, k_b, kk, akk_b, v)


def _diag_scan_body(af_ref, uf_ref, ab_ref, ub_ref, of_ref, ob_ref, hf_ref, hb_ref, *, tb):
    @pl.when(pl.program_id(0) == 0)
    def _():
        hf_ref[...] = jnp.zeros_like(hf_ref)
        hb_ref[...] = jnp.zeros_like(hb_ref)

    def step(t, carry):
        hf, hb = carry
        hf = af_ref[:, t, :] * hf + uf_ref[:, t, :]
        of_ref[:, t, :] = hf
        tr = tb - 1 - t
        hb = ab_ref[:, tr, :] * hb + ub_ref[:, tr, :]
        ob_ref[:, tr, :] = hb
        return hf, hb

    hf, hb = lax.fori_loop(0, tb, step, (hf_ref[...], hb_ref[...]))
    hf_ref[...] = hf
    hb_ref[...] = hb


def diag_scan(a_f, u_f, a_b, u_b, n_ctx):
    b, l, c = a_f.shape
    tb = SCAN_TB
    rev = _reversed_block_map(l // tb, n_ctx // tb)
    fwd = pl.BlockSpec((b, tb, c), lambda i: (0, i, 0))
    bwd = pl.BlockSpec((b, tb, c), lambda i: (0, rev(i), 0))
    out = jax.ShapeDtypeStruct((b, l, c), F32)
    return pl.pallas_call(
        functools.partial(_diag_scan_body, tb=tb),
        grid=(l // tb,),
        in_specs=[fwd, fwd, bwd, bwd],
        out_specs=[fwd, bwd],
        out_shape=[out, out],
        scratch_shapes=[pltpu.VMEM((b, c), F32), pltpu.VMEM((b, c), F32)],
        compiler_params=_cparams(("arbitrary",)),
        name="diag_scan",
    )(a_f, u_f, a_b, u_b)


def _attn_body(lam_ref, q_ref, k_ref, v_ref, sg_ref, o_ref, *, ctx_tiles, n_ctx, scale, out_scale):
    j = pl.program_id(2)
    q = q_ref[0]
    lane = lax.broadcasted_iota(I32, q.shape, 1)
    qs = q * jnp.asarray(scale, BF16)
    q1 = jnp.where(lane < C_QK_DIM, qs, jnp.zeros_like(qs))
    q2 = jnp.where(lane >= C_QK_DIM, qs, jnp.zeros_like(qs))
    lam = lam_ref[0]

    def attend(kt, vb):
        half = q.shape[0] // 2
        pieces = [qm[r * half:(r + 1) * half] for r in range(2) for qm in (q1, q2)]

        def softmax_times_v(s):
            e = jnp.exp(s - jnp.max(s, axis=-1, keepdims=True))
            pv = jnp.dot(e.astype(BF16), vb, preferred_element_type=F32)
            return pv / jnp.sum(e, axis=-1, keepdims=True)

        ahead = 2
        scores = [jnp.dot(qp, kt, preferred_element_type=F32) for qp in pieces[:ahead]]
        outs = []
        for n in range(len(pieces)):
            if n + ahead < len(pieces):
                scores.append(jnp.dot(pieces[n + ahead], kt, preferred_element_type=F32))
            outs.append(softmax_times_v(scores[n]))
        o = jnp.concatenate([outs[0] - lam * outs[1], outs[2] - lam * outs[3]], axis=0)
        o = o * lax.rsqrt(jnp.mean(o * o, axis=-1, keepdims=True) + RMS_EPS)
        o_ref[0] = o * sg_ref[...] * out_scale

    @pl.when(j < ctx_tiles)
    def _():
        attend(k_ref[0, :, :n_ctx], v_ref[0, :n_ctx])

    @pl.when(j >= ctx_tiles)
    def _():
        attend(k_ref[0], v_ref[0])


def diff_attention_core(q, k_t, v, lam, sub_g, n_ctx, lam_init):
    b, l, width = q.shape
    heads = width // C_V_DIM
    assert n_ctx % V7X_LANES == 0
    tq = _token_tile(n_ctx, l - n_ctx, 256)
    ctx_tiles = n_ctx // tq
    scale = C_QK_DIM ** -0.5
    assert math.frexp(scale)[0] == 0.5, "the kernel folds the score scale into q, exact only for a power of two"
    body = functools.partial(_attn_body, ctx_tiles=ctx_tiles, n_ctx=n_ctx, scale=scale, out_scale=1.0 - lam_init)
    return pl.pallas_call(
        body,
        grid=(b, heads, l // tq),
        in_specs=[
            pl.BlockSpec(memory_space=pltpu.SMEM),
            pl.BlockSpec((1, tq, C_V_DIM), lambda i, h, j: (i, j, h)),
            pl.BlockSpec((1, C_V_DIM, l), lambda i, h, j: (i, h, 0)),
            pl.BlockSpec((1, l, C_V_DIM), lambda i, h, j: (i, 0, h)),
            pl.BlockSpec((1, C_V_DIM), lambda i, h, j: (0, 0)),
        ],
        out_specs=pl.BlockSpec((1, tq, C_V_DIM), lambda i, h, j: (i, j, h)),
        out_shape=jax.ShapeDtypeStruct((b, l, width), F32),
        compiler_params=_cparams(("parallel", "parallel", "parallel")),
        name="diff_attention",
    )(lam, q, k_t, v, sub_g)


def _peer_candidates():
    return [(a, b) for a in range(PEER_TOPK) for b in range(PEER_TOPK) if (a + 1) * (b + 1) <= PEER_TOPK]


def _segment_rows(mod_ref, tile, tm, n_ctx):
    row = tile * tm + lax.broadcasted_iota(I32, (tm, 1), 0)
    return jnp.where(row < n_ctx, mod_ref[0, 0], mod_ref[0, 1])


def _peer_route_body(x_ref, sh_ref, sc_ref, wq_ref, keys_ref, h_ref, i_ref, j_ref, g_ref,
                     sv_ref, si_ref, cand_ref, best_ref, bi_ref, bj_ref, s_ref, *, tm, n_ctx):
    tile = pl.program_id(1)
    h = _rms_modulate(x_ref[0], _segment_rows(sh_ref, tile, tm, n_ctx),
                      _segment_rows(sc_ref, tile, tm, n_ctx)).astype(BF16)
    h_ref[0] = h
    q = jnp.dot(h, wq_ref[...], preferred_element_type=F32).astype(BF16)
    nt = (((1,), (1,)), ((), ()))
    neg = jnp.float32(-jnp.inf)

    def best_of(nodes):
        while len(nodes) > 1:
            nxt = []
            for (va, ia), (vb, ib) in zip(nodes[0::2], nodes[1::2]):
                left = va >= vb
                nxt.append((jnp.maximum(va, vb), jnp.where(left, ia, ib)))
            nodes = nxt
        return nodes[0]

    for m in range(2):
        for col in range(tm // V7X_LANES):
            toks = slice(col * V7X_LANES, (col + 1) * V7X_LANES)
            per_head = [
                lax.dot_general(keys_ref[m], q[toks, (hd * 2 + m) * N_KEYS:(hd * 2 + m + 1) * N_KEYS], nt,
                                preferred_element_type=F32) for hd in range(PEER_HEADS)]
            s_ref[...] = jnp.swapaxes(jnp.stack(per_head, axis=0), 0, 1)

            def extract(a, carry, m=m, toks=toks):
                vals = [s_ref[n] for n in range(N_KEYS)]
                mx, idx = best_of([(vals[n], jnp.float32(n)) for n in range(N_KEYS)])
                sv_ref[m, a, :, toks] = mx
                si_ref[m, a, :, toks] = idx
                for n in range(N_KEYS):
                    s_ref[n] = jnp.where(idx == jnp.float32(n), neg, vals[n])
                return carry

            lax.fori_loop(0, PEER_TOPK, extract, 0)

    cands = _peer_candidates()
    for c, (a, b) in enumerate(cands):
        cand_ref[c] = sv_ref[0, a] + sv_ref[1, b]
    big = jnp.float32(PEER_TOPK * PEER_TOPK)

    def pick(k, carry):
        vals = [cand_ref[c] for c in range(len(cands))]
        mx = functools.reduce(jnp.maximum, vals)
        pos = functools.reduce(
            jnp.minimum,
            [jnp.where(vals[c] == mx, jnp.float32(a * PEER_TOPK + b), big) for c, (a, b) in enumerate(cands)])
        ii = jnp.zeros_like(pos)
        jj = jnp.zeros_like(pos)
        for c, (a, b) in enumerate(cands):
            hit = pos == jnp.float32(a * PEER_TOPK + b)
            cand_ref[c] = jnp.where(hit, neg, vals[c])
            ii = jnp.where(hit, si_ref[0, a], ii)
            jj = jnp.where(hit, si_ref[1, b], jj)
        best_ref[k] = mx
        bi_ref[k] = ii
        bj_ref[k] = jj
        return carry

    lax.fori_loop(0, PEER_TOPK, pick, 0)
    best = best_ref[...]
    e = jnp.exp(best - best[0][None])
    g = e / jnp.sum(e, axis=0, keepdims=True)
    g_ref[0] = g.reshape(PEER_TOPK * PEER_HEADS, tm).T
    i_ref[0] = bi_ref[...].reshape(PEER_TOPK * PEER_HEADS, tm).T
    j_ref[0] = bj_ref[...].reshape(PEER_TOPK * PEER_HEADS, tm).T


PEER_TILE_TARGET = 512


def _peer_tile(l):
    t = (PEER_TILE_TARGET // V7X_LANES) * V7X_LANES
    while l % t:
        t -= V7X_LANES
    return t


def _mod_pair_spec(d):
    return pl.BlockSpec((1, 2, 1, d), lambda b, *_: (b, 0, 0, 0))


def peer_route(xs, shift, scale, w_q, keys, n_ctx):
    b, l, d = xs.shape
    tm = _peer_tile(l)
    nsel = PEER_TOPK * PEER_HEADS
    ncand = len(_peer_candidates())
    sel_spec = pl.BlockSpec((1, tm, nsel), lambda i, j: (i, j, 0))
    return pl.pallas_call(
        functools.partial(_peer_route_body, tm=tm, n_ctx=n_ctx),
        grid=(b, l // tm),
        in_specs=[
            pl.BlockSpec((1, tm, d), lambda i, j: (i, j, 0)),
            _mod_pair_spec(d),
            _mod_pair_spec(d),
            pl.BlockSpec(w_q.shape, lambda i, j: (0, 0)),
            pl.BlockSpec(keys.shape, lambda i, j: (0, 0, 0)),
        ],
        out_specs=[pl.BlockSpec((1, tm, d), lambda i, j: (i, j, 0)), sel_spec, sel_spec, sel_spec],
        out_shape=[
            jax.ShapeDtypeStruct((b, l, d), BF16),
            jax.ShapeDtypeStruct((b, l, nsel), F32),
            jax.ShapeDtypeStruct((b, l, nsel), F32),
            jax.ShapeDtypeStruct((b, l, nsel), F32),
        ],
        scratch_shapes=[
            pltpu.VMEM((2, PEER_TOPK, PEER_HEADS, tm), F32),
            pltpu.VMEM((2, PEER_TOPK, PEER_HEADS, tm), F32),
            pltpu.VMEM((ncand, PEER_HEADS, tm), F32),
            pltpu.VMEM((PEER_TOPK, PEER_HEADS, tm), F32),
            pltpu.VMEM((PEER_TOPK, PEER_HEADS, tm), F32),
            pltpu.VMEM((PEER_TOPK, PEER_HEADS, tm), F32),
            pltpu.VMEM((N_KEYS, PEER_HEADS, V7X_LANES), F32),
        ],
        compiler_params=_cparams(("parallel", "parallel")),
        name="peer_route",
    )(xs, shift, scale, w_q.astype(BF16), keys.astype(BF16))


PEER_G_CHUNK = 16
PEER_PAIR = 2 * N_KEYS
PEER_EXPERT_BLOCK = 2048


def _peer_expert_body(h_ref, i_ref, j_ref, g_ref, u_ref, v_ref, x_ref, gate_ref, o_ref, gw_ref, acc_ref, wt_ref,
                      *, tm, eb, n_ctx):
    e = pl.program_id(2)

    @pl.when(e == 0)
    def _():
        acc_ref[...] = jnp.zeros_like(acc_ref)
        isel, jsel, gsel = i_ref[0], j_ref[0], g_ref[0]
        kio = lax.broadcasted_iota(I32, (PEER_G_CHUNK, N_KEYS, isel.shape[1]), 1).astype(F32)
        for c in range(tm // PEER_G_CHUNK):
            rows = slice(c * PEER_G_CHUNK, (c + 1) * PEER_G_CHUNK)
            oh_i = jnp.where(isel[rows][:, None, :] == kio, 1.0, 0.0).astype(BF16)
            oh_j = jnp.where(jsel[rows][:, None, :] == kio, gsel[rows][:, None, :], 0.0).astype(BF16)
            g3 = jnp.einsum("pis,pjs->pij", oh_i, oh_j, preferred_element_type=F32)
            gw_ref[:, rows, :] = jnp.swapaxes(g3, 0, 1).astype(BF16)

    h = h_ref[0]
    for ip in range(eb // PEER_PAIR):
        rows = slice(ip * PEER_PAIR, (ip + 1) * PEER_PAIR)
        act = jax.nn.gelu(jnp.dot(h, u_ref[:, rows], preferred_element_type=F32)).astype(BF16)
        i0 = e * (eb // N_KEYS) + ip * 2
        wt_ref[:, ip * PEER_PAIR:ip * PEER_PAIR + N_KEYS] = act[:, :N_KEYS] * gw_ref[i0]
        wt_ref[:, ip * PEER_PAIR + N_KEYS:(ip + 1) * PEER_PAIR] = act[:, N_KEYS:] * gw_ref[i0 + 1]
    acc_ref[...] += jnp.dot(wt_ref[...], v_ref[...], preferred_element_type=F32)

    @pl.when(e == pl.num_programs(2) - 1)
    def _():
        gate = _segment_rows(gate_ref, pl.program_id(1), tm, n_ctx)
        o_ref[0] = x_ref[0] + gate * acc_ref[...]


def peer_experts(h, isel, jsel, gsel, u_t, v_tab, xs, gate, n_ctx):
    b, l, d = xs.shape
    ne = v_tab.shape[0]
    tm = _peer_tile(l)
    eb = PEER_EXPERT_BLOCK if tm < PEER_TILE_TARGET else PEER_EXPERT_BLOCK // 2
    nsel = isel.shape[2]
    tok_spec = pl.BlockSpec((1, tm, d), lambda i, j, e: (i, j, 0))
    sel_spec = pl.BlockSpec((1, tm, nsel), lambda i, j, e: (i, j, 0))
    return pl.pallas_call(
        functools.partial(_peer_expert_body, tm=tm, eb=eb, n_ctx=n_ctx),
        grid=(b, l // tm, ne // eb),
        in_specs=[
            tok_spec, sel_spec, sel_spec, sel_spec,
            pl.BlockSpec((d, eb), lambda i, j, e: (0, e)),
            pl.BlockSpec((eb, d), lambda i, j, e: (e, 0)),
            tok_spec,
            _mod_pair_spec(d),
        ],
        out_specs=tok_spec,
        out_shape=jax.ShapeDtypeStruct((b, l, d), F32),
        scratch_shapes=[
            pltpu.VMEM((N_KEYS, tm, N_KEYS), BF16),
            pltpu.VMEM((tm, d), F32),
            pltpu.VMEM((tm, eb), BF16),
        ],
        compiler_params=_cparams(("parallel", "parallel", "arbitrary")),
        name="peer_experts",
    )(h, isel, jsel, gsel, u_t, v_tab, xs, gate)


def _group_ones(width, group):
    g = jnp.arange(width) // group
    return (g[:, None] == g[None, :]).astype(F32)


def _group_sum(x, ones_ref):
    return jnp.dot(x, ones_ref[...], precision=lax.Precision.HIGHEST, preferred_element_type=F32)


def _pad_rank_rows(w, lo, total):
    return jnp.pad(w, ((lo, total - lo - w.shape[0]), (0, 0)))


HALO_ROWS = V7X_SUBLANES


def _even_prep_body(pm_ref, pp_ref, pn_ref, mu_ref, rvec_ref, wup_ref, aup_ref, gup_ref, ones_ref,
                    cw_ref, wa_ref, wx_ref, lvec_ref,
                    r_ref, decf_ref, decb_ref, ktf_ref, ktb_ref, kk_ref, akkf_ref, akkb_ref, v_ref,
                    bonus_ref, ga_ref, af_ref, uf_ref, ab_ref, ub_ref, gb_ref,
                    *, tm, ctx_tiles, n_tiles, width, a_proj):
    j = pl.program_id(1)
    first = jnp.logical_or(j == 0, j == ctx_tiles)
    last = jnp.logical_or(j == ctx_tiles - 1, j == n_tiles - 1)
    p = pm_ref[0]
    prev = jnp.where(first, 0.0, pp_ref[0])
    nxt = jnp.where(last, 0.0, pn_ref[0])
    row = lax.broadcasted_iota(I32, (tm, 1), 0)

    def shifted(lo, hi, off):
        x = p[:, lo:hi]
        if off < 0:
            y = pltpu.roll(x, -off, axis=0)
            for s in range(-off):
                y = jnp.where(row == s, prev[HALO_ROWS + off + s:HALO_ROWS + off + s + 1, lo:hi], y)
        else:
            y = pltpu.roll(x, tm - off, axis=0)
            for s in range(off):
                y = jnp.where(row == tm - off + s, nxt[s:s + 1, lo:hi], y)
        return y

    pa = p[:, :a_proj]
    pa = pa + mu_ref[...] * (0.5 * (shifted(0, a_proj, -1) + shifted(0, a_proj, 1)) - pa)
    r = pa[:, :width]
    k = pa[:, width:2 * width]
    v = pa[:, 2 * width:3 * width]
    c0 = 3 * width
    wd = jnp.tanh(pa[:, c0:c0 + V7X_LANES]).astype(BF16)
    ad = pa[:, c0 + V7X_LANES:c0 + 2 * V7X_LANES].astype(BF16)
    gd = jax.nn.sigmoid(pa[:, c0 + 2 * V7X_LANES:c0 + 3 * V7X_LANES]).astype(BF16)
    w0_f, w0_b, a0_f, a0_b, k_k, k_a, r_k = (rvec_ref[n:n + 1] for n in range(7))

    kk = k * k_k
    kk = kk / jnp.maximum(jnp.sqrt(_group_sum(kk * kk, ones_ref)), 1e-12)
    kts = []
    for w0, a0, d, dec_ref, kt_ref, akk_ref in ((w0_f, a0_f, 0, decf_ref, ktf_ref, akkf_ref),
                                                (w0_b, a0_b, 1, decb_ref, ktb_ref, akkb_ref)):
        lora_w = jnp.dot(wd, wup_ref[d], preferred_element_type=F32)
        dec_ref[0] = jnp.exp(-DECAY_SCALE * jax.nn.sigmoid(w0 + lora_w))
        a = jax.nn.sigmoid(a0 + jnp.dot(ad, aup_ref[d], preferred_element_type=F32))
        kt = k * (1.0 + (a - 1.0) * k_a)
        kt_ref[0] = kt
        akk_ref[0] = a * kk
        kts.append(kt)
    r_ref[0] = r
    kk_ref[0] = kk
    v_ref[0] = v
    bonus_ref[0] = _group_sum(r * (0.5 * (kts[0] + kts[1])) * r_k, ones_ref) * v
    ga_ref[0] = jnp.dot(gd, gup_ref[...], preferred_element_type=F32)

    b0 = a_proj
    conv_b, ba_f, ba_b, bx_f, bx_b, sp_f, sp_b = (lvec_ref[n:n + 1] for n in range(7))
    xb = (cw_ref[0:1] * shifted(b0, b0 + width, -2) + cw_ref[1:2] * shifted(b0, b0 + width, -1)
          + cw_ref[2:3] * p[:, b0:b0 + width] + cw_ref[3:4] * shifted(b0, b0 + width, 1) + conv_b)
    gb_ref[0] = jax.nn.gelu(p[:, b0 + width:b0 + 2 * width])
    xbb = xb.astype(BF16)
    for d, ba, bx, sp, a_ref, u_ref in ((0, ba_f, bx_f, sp_f, af_ref, uf_ref), (1, ba_b, bx_b, sp_b, ab_ref, ub_ref)):
        rg = jax.nn.sigmoid(jnp.dot(xbb, wa_ref[d], preferred_element_type=F32) + ba)
        ig = jax.nn.sigmoid(jnp.dot(xbb, wx_ref[d], preferred_element_type=F32) + bx)
        log_a = -LRU_C * rg * sp
        a_ref[0] = jnp.exp(log_a)
        th = jnp.tanh(log_a)
        u_ref[0] = jnp.sqrt(-2.0 * th / (1.0 - th)) * ig * xb


def even_prep(p, n_ctx, mu, w0, w_up, a0, a_up, g_up, k_k, k_a, r_k, conv_w, conv_b, wa, ba, wx, bx, lam):
    b, l, n_proj = p.shape
    width = w0.shape[-1]
    a_proj = mu.shape[-1]
    assert CONV_W == 4 and 2 * DECAY_RANK == V7X_LANES and 2 * ICLR_RANK == V7X_LANES and GATE_RANK == V7X_LANES
    assert a_proj == 3 * width + 3 * V7X_LANES and n_proj == a_proj + 2 * width
    tm = _token_tile(n_ctx, l - n_ctx, 256)
    n_tiles, ctx_tiles = l // tm, n_ctx // tm
    halo_per_tile = tm // HALO_ROWS
    last_halo = l // HALO_ROWS - 1
    rvec = jnp.concatenate([w0, a0, k_k[None], k_a[None], r_k.reshape(1, width), jnp.zeros((1, width), F32)], axis=0)
    lvec = jnp.concatenate([conv_b[None], ba, bx, jax.nn.softplus(-lam), jnp.zeros((1, width), F32)], axis=0)
    wup = jnp.stack([_pad_rank_rows(w_up[0], 0, V7X_LANES), _pad_rank_rows(w_up[1], DECAY_RANK, V7X_LANES)])
    aup = jnp.stack([_pad_rank_rows(a_up[0], 0, V7X_LANES), _pad_rank_rows(a_up[1], ICLR_RANK, V7X_LANES)])

    def block_diag(w):
        return jnp.stack([jax.scipy.linalg.block_diag(*w[d]) for d in range(2)]).astype(BF16)

    tok = pl.BlockSpec((1, tm, width), lambda i, j: (i, j, 0))

    def full(a):
        return pl.BlockSpec(a.shape, lambda i, j, nd=a.ndim: (0,) * nd)

    consts = [mu[None], rvec, wup.astype(BF16), aup.astype(BF16), g_up.astype(BF16),
              _group_ones(width, A_HEAD_DIM), conv_w, block_diag(wa), block_diag(wx), lvec]
    return pl.pallas_call(
        functools.partial(_even_prep_body, tm=tm, ctx_tiles=ctx_tiles, n_tiles=n_tiles, width=width, a_proj=a_proj),
        grid=(b, n_tiles),
        in_specs=[
            pl.BlockSpec((1, tm, n_proj), lambda i, j: (i, j, 0)),
            pl.BlockSpec((1, HALO_ROWS, n_proj), lambda i, j: (i, jnp.maximum(j * halo_per_tile - 1, 0), 0)),
            pl.BlockSpec((1, HALO_ROWS, n_proj), lambda i, j: (i, jnp.minimum((j + 1) * halo_per_tile, last_halo), 0)),
        ] + [full(a) for a in consts],
        out_specs=[tok] * 16,
        out_shape=[jax.ShapeDtypeStruct((b, l, width), F32)] * 16,
        compiler_params=_cparams(("parallel", "parallel")),
        name="even_prep",
    )(p, p, p, *consts)


def _even_post_body(of_ref, ob_ref, bonus_ref, ga_ref, hf_ref, hb_ref, gb_ref, ln_ref, ones_ref,
                    x_ref, g_ref, w_ref, o_ref, *, width):
    o = of_ref[0] + ob_ref[0]
    inv_n = 1.0 / A_HEAD_DIM
    cen = o - _group_sum(o, ones_ref) * inv_n
    var = _group_sum(cen * cen, ones_ref) * inv_n
    on = cen * lax.rsqrt(var + GN_EPS) * ln_ref[0:1] + ln_ref[1:2]
    ya = ((on + bonus_ref[0]) * ga_ref[0]).astype(BF16)
    yb = ((hf_ref[0] + hb_ref[0]) * gb_ref[0]).astype(BF16)
    acc = (jnp.dot(ya, w_ref[:width], preferred_element_type=F32)
           + jnp.dot(yb, w_ref[width:], preferred_element_type=F32))
    o_ref[0] = x_ref[0] + g_ref[0, 0] * acc


def even_post(o_f, o_b, bonus, gate_a, h_f, h_b, gate_b, ln_w, ln_b, xs, gate, w_out, n_ctx):
    b, l, d = xs.shape
    width = o_f.shape[-1]
    tm = _token_tile(n_ctx, l - n_ctx, 256)
    ctx_tiles = n_ctx // tm
    tok = pl.BlockSpec((1, tm, width), lambda i, j: (i, j, 0))
    ln = jnp.stack([ln_w, ln_b])
    ones = _group_ones(width, A_HEAD_DIM)
    return pl.pallas_call(
        functools.partial(_even_post_body, width=width),
        grid=(b, l // tm),
        in_specs=[tok] * 7 + [
            pl.BlockSpec(ln.shape, lambda i, j: (0, 0)),
            pl.BlockSpec(ones.shape, lambda i, j: (0, 0)),
            pl.BlockSpec((1, tm, d), lambda i, j: (i, j, 0)),
            _mod_spec(d, ctx_tiles),
            pl.BlockSpec(w_out.shape, lambda i, j: (0, 0)),
        ],
        out_specs=pl.BlockSpec((1, tm, d), lambda i, j: (i, j, 0)),
        out_shape=jax.ShapeDtypeStruct((b, l, d), F32),
        compiler_params=_cparams(("parallel", "parallel")),
        name="even_post",
    )(o_f, o_b, bonus, gate_a, h_f, h_b, gate_b, ln, ones, xs, gate, w_out.astype(BF16))


def _rwkv_chain_layout(z, heads):
    b, l, width = z.shape
    n = width // heads
    z = z.reshape(b, l, heads, n).transpose(1, 3, 0, 2).reshape(l, n, b * heads)
    z = jnp.pad(z, ((0, 0), (0, 0), (0, RWKV_CHAINS - b * heads)))
    return z.reshape(l, n, V7X_SUBLANES // RWKV_KPAR, V7X_LANES)


def _rwkv_keyed(z, heads):
    z = _rwkv_chain_layout(z, heads)
    l, n = z.shape[:2]
    return z.reshape(l, n // RWKV_KPAR, V7X_SUBLANES, V7X_LANES)


def _rwkv_valued(z, heads):
    z = _rwkv_chain_layout(z, heads)
    return jnp.concatenate([z] * RWKV_KPAR, axis=2)


def _rwkv_unchain(o, b, heads):
    l, n = o.shape[:2]
    o = o[:, :, :V7X_SUBLANES // RWKV_KPAR].reshape(l, n, RWKV_CHAINS)[:, :, :b * heads]
    return o.reshape(l, n, b, heads).transpose(2, 0, 3, 1).reshape(b, l, heads * n)


def _even_mixer(xs, shift, scale, gate, n_ctx, w_in, mu, w_out, w0, w_up, a0, a_up, g_up, k_k, k_a, r_k, ln_w, ln_b,
                conv_w, conv_b, wa, ba, wx, bx, lam):
    b = xs.shape[0]
    heads = w0.shape[-1] // A_HEAD_DIM
    assert b * heads <= RWKV_CHAINS
    p = modulate_matmul(xs, shift, scale, w_in, n_ctx)
    (r, dec_f, dec_b, kt_f, kt_b, kk, akk_f, akk_b, v, bonus, gate_a, a_f, u_f, a_b, u_b, gate_b) = even_prep(
        p, n_ctx, mu, w0, w_up, a0, a_up, g_up, k_k, k_a, r_k, conv_w, conv_b, wa, ba, wx, bx, lam)
    o_f, o_b = rwkv_scan(
        _rwkv_keyed(r, heads), _rwkv_keyed(dec_f, heads), _rwkv_keyed(dec_b, heads),
        _rwkv_keyed(kt_f, heads), _rwkv_keyed(kt_b, heads), _rwkv_keyed(kk, heads),
        _rwkv_keyed(akk_f, heads), _rwkv_keyed(akk_b, heads), _rwkv_valued(v, heads), n_ctx)
    h_f, h_b = diag_scan(a_f, u_f, a_b, u_b, n_ctx)
    return even_post(_rwkv_unchain(o_f, b, heads), _rwkv_unchain(o_b, b, heads), bonus, gate_a, h_f, h_b, gate_b,
                     ln_w, ln_b, xs, gate, w_out, n_ctx)


ROPE_HALF = C_QK_DIM // 4


def _qkv_body(x_ref, sh_ref, sc_ref, w_ref, ones_ref, g_ref, cos_ref, sin_ref, q_ref, k_ref, v_ref, *, width):
    z = _rms_modulate(x_ref[0], sh_ref[0, 0], sc_ref[0, 0])
    qkv = jnp.dot(z.astype(BF16), w_ref[...], preferred_element_type=F32)
    cos = cos_ref[...]
    sin = sin_ref[...]
    lane = lax.broadcasted_iota(I32, cos.shape, 1)
    first_half = (lane % (2 * ROPE_HALF)) < ROPE_HALF
    for part, out_ref in ((0, q_ref), (1, k_ref)):
        gain = g_ref[part:part + 1]
        for c in range(width // V7X_LANES):
            lo = part * width + c * V7X_LANES
            t = qkv[:, lo:lo + V7X_LANES]
            ms = _group_sum(t * t, ones_ref) * (1.0 / C_QK_DIM)
            t = t * lax.rsqrt(ms + RMS_EPS) * gain
            partner = jnp.where(first_half, pltpu.roll(t, V7X_LANES - ROPE_HALF, axis=1),
                                pltpu.roll(t, ROPE_HALF, axis=1))
            t = t * cos + partner * sin
            if part == 0:
                out_ref[0, :, c * V7X_LANES:(c + 1) * V7X_LANES] = t.astype(BF16)
            else:
                out_ref[0, c * V7X_LANES:(c + 1) * V7X_LANES, :] = t.T.astype(BF16)
    v_ref[0] = qkv[:, 2 * width:].astype(BF16)


def _rope_tables(n_ctx, n_lat):
    n_rows = n_lat // GRID_W
    row_pos = jnp.repeat(jnp.arange(n_rows), GRID_W).astype(F32)
    col_pos = jnp.tile(jnp.arange(GRID_W), n_rows).astype(F32)
    inv_freq = ROPE_BASE ** (-jnp.arange(ROPE_HALF, dtype=F32) / ROPE_HALF)

    def one(pos):
        ang = pos[:, None] * inv_freq
        c, s = jnp.cos(ang), jnp.sin(ang)
        return jnp.concatenate([c, c], axis=-1), jnp.concatenate([-s, s], axis=-1)

    (cr, sr), (cc, sc) = one(row_pos), one(col_pos)
    cos = jnp.concatenate([cr, cc], axis=-1)
    sin = jnp.concatenate([sr, sc], axis=-1)
    cos = jnp.concatenate([jnp.ones((n_ctx, C_QK_DIM), F32), cos], axis=0)
    sin = jnp.concatenate([jnp.zeros((n_ctx, C_QK_DIM), F32), sin], axis=0)
    reps = V7X_LANES // C_QK_DIM
    return jnp.tile(cos, (1, reps)), jnp.tile(sin, (1, reps))


def qkv_project(xs, shift, scale, w_qkv, q_g, k_g, n_ctx):
    b, l, d = xs.shape
    width = w_qkv.shape[1] // 3
    tm = _token_tile(n_ctx, l - n_ctx, 256)
    ctx_tiles = n_ctx // tm
    cos, sin = _rope_tables(n_ctx, l - n_ctx)
    gains = jnp.stack([jnp.tile(q_g, V7X_LANES // C_QK_DIM), jnp.tile(k_g, V7X_LANES // C_QK_DIM)])
    ones = _group_ones(V7X_LANES, C_QK_DIM)
    out = pl.BlockSpec((1, tm, width), lambda i, j: (i, j, 0))
    return pl.pallas_call(
        functools.partial(_qkv_body, width=width),
        grid=(b, l // tm),
        in_specs=[
            pl.BlockSpec((1, tm, d), lambda i, j: (i, j, 0)),
            _mod_spec(d, ctx_tiles),
            _mod_spec(d, ctx_tiles),
            pl.BlockSpec(w_qkv.shape, lambda i, j: (0, 0)),
            pl.BlockSpec(ones.shape, lambda i, j: (0, 0)),
            pl.BlockSpec(gains.shape, lambda i, j: (0, 0)),
            pl.BlockSpec((tm, V7X_LANES), lambda i, j: (j, 0)),
            pl.BlockSpec((tm, V7X_LANES), lambda i, j: (j, 0)),
        ],
        out_specs=[out, pl.BlockSpec((1, width, tm), lambda i, j: (i, 0, j)), out],
        out_shape=[jax.ShapeDtypeStruct((b, l, width), BF16), jax.ShapeDtypeStruct((b, width, l), BF16),
                   jax.ShapeDtypeStruct((b, l, width), BF16)],
        compiler_params=_cparams(("parallel", "parallel")),
        name="qkv_project",
    )(xs, shift, scale, w_qkv.astype(BF16), ones, gains, cos, sin)


def kernel(x, c, ctx, c_ctx, w_mod, b_mod, even_w_in, even_mu, even_w_out, rwkv_w0, rwkv_w_up, rwkv_a0, rwkv_a_up, rwkv_g_up, rwkv_k_k, rwkv_k_a, rwkv_r_k, rwkv_ln_w, rwkv_ln_b, lru_conv_w, lru_conv_b, lru_wa, lru_ba, lru_wx, lru_bx, lru_lam, attn_w_qkv, attn_q_g, attn_k_g, attn_lam_q1, attn_lam_k1, attn_lam_q2, attn_lam_k2, attn_sub_g, attn_w_o, peer_w_q, peer_keys, peer_u, peer_v):
    bsz, n_lat, d = x.shape
    n_ctx = ctx.shape[1]
    depth = w_mod.shape[0]
    a_proj = even_mu.shape[-1]
    s_lat = jax.nn.silu(c)
    s_ctx = jax.nn.silu(c_ctx)
    xs = jnp.concatenate([ctx, x], axis=1)
    for layer in range(depth):
        i = layer // 2
        m_l = s_lat @ w_mod[layer] + b_mod[layer]
        m_c = s_ctx @ w_mod[layer] + b_mod[layer]
        mods = jnp.stack([jnp.broadcast_to(m_c, m_l.shape), m_l], axis=1).reshape(bsz, 2, N_MOD, 1, d)
        shift1, scale1, gate1, shift2, scale2, gate2 = (mods[:, :, n] for n in range(N_MOD))
        if layer % 2 == 0:
            xs = _even_mixer(xs, shift1, scale1, gate1, n_ctx, even_w_in[i], even_mu[i], even_w_out[i],
                             rwkv_w0[i], rwkv_w_up[i], rwkv_a0[i], rwkv_a_up[i], rwkv_g_up[i], rwkv_k_k[i],
                             rwkv_k_a[i], rwkv_r_k[i], rwkv_ln_w[i], rwkv_ln_b[i], lru_conv_w[i], lru_conv_b[i],
                             lru_wa[i], lru_ba[i], lru_wx[i], lru_bx[i], lru_lam[i])
        else:
            lam_init = 0.8 - 0.6 * math.exp(-0.3 * layer)
            q, k, v = qkv_project(xs, shift1, scale1, attn_w_qkv[i], attn_q_g[i], attn_k_g[i], n_ctx)
            lam = (jnp.exp(jnp.sum(attn_lam_q1[i] * attn_lam_k1[i]))
                   - jnp.exp(jnp.sum(attn_lam_q2[i] * attn_lam_k2[i])) + lam_init)
            o = diff_attention_core(q, k, v, lam.reshape(1), attn_sub_g[i].reshape(1, C_V_DIM), n_ctx, lam_init)
            xs = gated_out_proj(o, xs, gate1, attn_w_o[i], n_ctx)
        seg_ctx = n_ctx if layer < depth - 1 else 0
        if layer == depth - 1:
            xs = xs[:, n_ctx:]
        h, isel, jsel, gsel = peer_route(xs, shift2, scale2, peer_w_q[layer], peer_keys[layer], seg_ctx)
        xs = peer_experts(h, isel, jsel, gsel, peer_u[layer].astype(BF16).T, peer_v[layer].astype(BF16), xs,
                          gate2, seg_ctx)
    return xs
```

```python
import functools
import math

import jax
import jax.numpy as jnp
from jax import lax
from jax.experimental import pallas as pl
from jax.experimental.pallas import tpu as pltpu

F32 = jnp.float32
BF16 = jnp.bfloat16
I32 = jnp.int32

V7X_LANES = 128
V7X_SUBLANES = 8
V7X_VMEM_LIMIT_BYTES = 56 * 1024 * 1024

TOKEN_TILE = 256

GRID_W = 64
N_MOD = 6
A_HEAD_DIM = 64
DECAY_SCALE = math.exp(-0.5)
GN_EPS = 64e-5
DECAY_RANK = 64
ICLR_RANK = 64
GATE_RANK = 128
KK_NORM_EPS = 1e-12
CONV_W = 4
LRU_C = 8.0
C_QK_DIM = 64
C_V_DIM = 128
ROPE_BASE = 10000.0
N_KEYS = 128
PEER_HEADS = 8
PEER_TOPK = 16
RMS_EPS = 1e-6


def _cparams(sem):
    return pltpu.CompilerParams(dimension_semantics=sem, vmem_limit_bytes=V7X_VMEM_LIMIT_BYTES)


def _token_tile(n_ctx, n_lat, want):
    t = want
    while n_ctx % t or n_lat % t:
        t //= 2
    return t


def _mod_spec(d, ctx_tiles):
    return pl.BlockSpec((1, 1, 1, d), lambda b, j, *_: (b, jnp.where(j >= ctx_tiles, 1, 0), 0, 0))


def _rms_modulate(x, shift, scale):
    xn = x * lax.rsqrt(jnp.mean(x * x, axis=-1, keepdims=True) + RMS_EPS)
    return xn * (1.0 + scale) + shift


def _outproj_body(y_ref, x_ref, g_ref, w_ref, o_ref):
    acc = jnp.dot(y_ref[0].astype(BF16), w_ref[...], preferred_element_type=F32)
    o_ref[0] = x_ref[0] + g_ref[0, 0] * acc


def gated_out_proj(y, xs, gate, w, n_ctx):
    b, l, d = xs.shape
    k = y.shape[-1]
    tm = _token_tile(n_ctx, l - n_ctx, TOKEN_TILE)
    ctx_tiles = n_ctx // tm
    return pl.pallas_call(
        _outproj_body,
        grid=(b, l // tm),
        in_specs=[
            pl.BlockSpec((1, tm, k), lambda i, j: (i, j, 0)),
            pl.BlockSpec((1, tm, d), lambda i, j: (i, j, 0)),
            _mod_spec(d, ctx_tiles),
            pl.BlockSpec((k, d), lambda i, j: (0, 0)),
        ],
        out_specs=pl.BlockSpec((1, tm, d), lambda i, j: (i, j, 0)),
        out_shape=jax.ShapeDtypeStruct((b, l, d), F32),
        compiler_params=_cparams(("parallel", "parallel")),
        name="gated_out_proj",
    )(y, xs, gate, w.astype(BF16))


RWKV_VBLK_A = 16
RWKV_VBLK_B = 8
RWKV_KPAR = 4
RWKV_CHAINS = (V7X_SUBLANES // RWKV_KPAR) * V7X_LANES
SCAN_TB = 8


def _reversed_block_map(n_blocks, ctx_blocks):
    def rev(i):
        return jnp.where(i < ctx_blocks, ctx_blocks - 1 - i, n_blocks - 1 - i + ctx_blocks)
    return rev


def _rwkv_scan_body(rf_ref, wf_ref, kf_ref, kkf_ref, akkf_ref, vf_ref,
                    rb_ref, wb_ref, kb_ref, kkb_ref, akkb_ref, vb_ref,
                    of_ref, ob_ref, tf_ref, tb_ref, sa_ref, *, tb, nk4, nv):
    @pl.when(pl.program_id(0) == 0)
    def _():
        tf_ref[...] = jnp.zeros_like(tf_ref)
        tb_ref[...] = jnp.zeros_like(tb_ref)

    def fold(acc):
        acc = acc + pltpu.roll(acc, V7X_SUBLANES // 2, axis=1)
        return acc + pltpu.roll(acc, V7X_SUBLANES // 4, axis=1)

    def one_direction(t, r_ref, w_ref, k_ref, kk_ref, akk_ref, v_ref, o_ref, t_ref):
        for vb in range(nv // RWKV_VBLK_A):
            vs = pl.ds(vb * RWKV_VBLK_A, RWKV_VBLK_A)

            def acc_sa(k4, acc):
                return acc + t_ref[k4, vs] * kk_ref[t, k4][None]

            acc = lax.fori_loop(0, nk4, acc_sa, jnp.zeros((RWKV_VBLK_A, V7X_SUBLANES, V7X_LANES), F32), unroll=True)
            sa_ref[vs] = fold(acc)
        for vb in range(nv // RWKV_VBLK_B):
            vs = pl.ds(vb * RWKV_VBLK_B, RWKV_VBLK_B)
            sa = sa_ref[vs]
            vv = v_ref[t, vs]

            def upd(k4, acc):
                tn = (t_ref[k4, vs] * w_ref[t, k4][None] - akk_ref[t, k4][None] * sa
                      + k_ref[t, k4][None] * vv)
                t_ref[k4, vs] = tn
                return acc + tn * r_ref[t, k4][None]

            acc = lax.fori_loop(0, nk4, upd, jnp.zeros((RWKV_VBLK_B, V7X_SUBLANES, V7X_LANES), F32), unroll=True)
            o_ref[t, vs] = fold(acc)

    def step(t, carry):
        one_direction(t, rf_ref, wf_ref, kf_ref, kkf_ref, akkf_ref, vf_ref, of_ref, tf_ref)
        one_direction(tb - 1 - t, rb_ref, wb_ref, kb_ref, kkb_ref, akkb_ref, vb_ref, ob_ref, tb_ref)
        return carry

    lax.fori_loop(0, tb, step, 0)


def rwkv_scan(r, w_f, w_b, k_f, k_b, kk, akk_f, akk_b, v, n_ctx):
    l, nk4 = r.shape[0], r.shape[1]
    nv = v.shape[1]
    tb = SCAN_TB
    rev = _reversed_block_map(l // tb, n_ctx // tb)
    kblk = (tb, nk4, V7X_SUBLANES, V7X_LANES)
    vblk = (tb, nv, V7X_SUBLANES, V7X_LANES)
    kf = pl.BlockSpec(kblk, lambda i: (i, 0, 0, 0))
    kb = pl.BlockSpec(kblk, lambda i: (rev(i), 0, 0, 0))
    vf = pl.BlockSpec(vblk, lambda i: (i, 0, 0, 0))
    vb = pl.BlockSpec(vblk, lambda i: (rev(i), 0, 0, 0))
    out = jax.ShapeDtypeStruct((l, nv, V7X_SUBLANES, V7X_LANES), F32)
    state = pltpu.VMEM((nk4, nv, V7X_SUBLANES, V7X_LANES), F32)
    return pl.pallas_call(
        functools.partial(_rwkv_scan_body, tb=tb, nk4=nk4, nv=nv),
        grid=(l // tb,),
        in_specs=[kf, kf, kf, kf, kf, vf, kb, kb, kb, kb, kb, vb],
        out_specs=[vf, vb],
        out_shape=[out, out],
        scratch_shapes=[state, state, pltpu.VMEM((nv, V7X_SUBLANES, V7X_LANES), F32)],
        compiler_params=_cparams(("arbitrary",)),
        name="rwkv_scan",
    )(r, w_f, k_f, kk, akk_f, v, r, w_b, k_b, kk, akk_b, v)


def _diag_scan_body(af_ref, uf_ref, ab_ref, ub_ref, of_ref, ob_ref, hf_ref, hb_ref, *, tb):
    @pl.when(pl.program_id(0) == 0)
    def _():
        hf_ref[...] = jnp.zeros_like(hf_ref)
        hb_ref[...] = jnp.zeros_like(hb_ref)

    def step(t, carry):
        hf, hb = carry
        hf = af_ref[:, t, :] * hf + uf_ref[:, t, :]
        of_ref[:, t, :] = hf
        tr = tb - 1 - t
        hb = ab_ref[:, tr, :] * hb + ub_ref[:, tr, :]
        ob_ref[:, tr, :] = hb
        return hf, hb

    hf, hb = lax.fori_loop(0, tb, step, (hf_ref[...], hb_ref[...]))
    hf_ref[...] = hf
    hb_ref[...] = hb


def diag_scan(a_f, u_f, a_b, u_b, n_ctx):
    b, l, c = a_f.shape
    tb = SCAN_TB
    rev = _reversed_block_map(l // tb, n_ctx // tb)
    fwd = pl.BlockSpec((b, tb, c), lambda i: (0, i, 0))
    bwd = pl.BlockSpec((b, tb, c), lambda i: (0, rev(i), 0))
    out = jax.ShapeDtypeStruct((b, l, c), F32)
    return pl.pallas_call(
        functools.partial(_diag_scan_body, tb=tb),
        grid=(l // tb,),
        in_specs=[fwd, fwd, bwd, bwd],
        out_specs=[fwd, bwd],
        out_shape=[out, out],
        scratch_shapes=[pltpu.VMEM((b, c), F32), pltpu.VMEM((b, c), F32)],
        compiler_params=_cparams(("arbitrary",)),
        name="diag_scan",
    )(a_f, u_f, a_b, u_b)


def _attn_body(lam_ref, q_ref, k_ref, v_ref, sg_ref, o_ref, *, ctx_tiles, n_ctx, scale, out_scale):
    j = pl.program_id(2)
    q = q_ref[0]
    lane = lax.broadcasted_iota(I32, q.shape, 1)
    qs = q * jnp.asarray(scale, BF16)
    q1 = jnp.where(lane < C_QK_DIM, qs, jnp.zeros_like(qs))
    q2 = jnp.where(lane >= C_QK_DIM, qs, jnp.zeros_like(qs))
    lam = lam_ref[0]

    def attend(kt, vb):
        half = q.shape[0] // 2
        pieces = [qm[r * half:(r + 1) * half] for r in range(2) for qm in (q1, q2)]

        def softmax_times_v(s):
            e = jnp.exp(s - jnp.max(s, axis=-1, keepdims=True))
            pv = jnp.dot(e.astype(BF16), vb, preferred_element_type=F32)
            return pv / jnp.sum(e, axis=-1, keepdims=True)

        ahead = 2
        scores = [jnp.dot(qp, kt, preferred_element_type=F32) for qp in pieces[:ahead]]
        outs = []
        for n in range(len(pieces)):
            if n + ahead < len(pieces):
                scores.append(jnp.dot(pieces[n + ahead], kt, preferred_element_type=F32))
            outs.append(softmax_times_v(scores[n]))
        o = jnp.concatenate([outs[0] - lam * outs[1], outs[2] - lam * outs[3]], axis=0)
        o = o * lax.rsqrt(jnp.mean(o * o, axis=-1, keepdims=True) + RMS_EPS)
        o_ref[0] = o * sg_ref[...] * out_scale

    @pl.when(j < ctx_tiles)
    def _():
        attend(k_ref[0, :, :n_ctx], v_ref[0, :n_ctx])

    @pl.when(j >= ctx_tiles)
    def _():
        attend(k_ref[0], v_ref[0])


def diff_attention_core(q, k_t, v, lam, sub_g, n_ctx, lam_init):
    b, l, width = q.shape
    heads = width // C_V_DIM
    assert n_ctx % V7X_LANES == 0
    tq = _token_tile(n_ctx, l - n_ctx, TOKEN_TILE)
    ctx_tiles = n_ctx // tq
    scale = C_QK_DIM ** -0.5
    assert math.frexp(scale)[0] == 0.5, "the kernel folds the score scale into q, exact only for a power of two"
    body = functools.partial(_attn_body, ctx_tiles=ctx_tiles, n_ctx=n_ctx, scale=scale, out_scale=1.0 - lam_init)
    return pl.pallas_call(
        body,
        grid=(b, heads, l // tq),
        in_specs=[
            pl.BlockSpec(memory_space=pltpu.SMEM),
            pl.BlockSpec((1, tq, C_V_DIM), lambda i, h, j: (i, j, h)),
            pl.BlockSpec((1, C_V_DIM, l), lambda i, h, j: (i, h, 0)),
            pl.BlockSpec((1, l, C_V_DIM), lambda i, h, j: (i, 0, h)),
            pl.BlockSpec((1, C_V_DIM), lambda i, h, j: (0, 0)),
        ],
        out_specs=pl.BlockSpec((1, tq, C_V_DIM), lambda i, h, j: (i, j, h)),
        out_shape=jax.ShapeDtypeStruct((b, l, width), F32),
        compiler_params=_cparams(("parallel", "parallel", "parallel")),
        name="diff_attention",
    )(lam, q, k_t, v, sub_g)


def _peer_candidates():
    return [(a, b) for a in range(PEER_TOPK) for b in range(PEER_TOPK) if (a + 1) * (b + 1) <= PEER_TOPK]


def _segment_rows(mod_ref, tile, tm, n_ctx):
    row = tile * tm + lax.broadcasted_iota(I32, (tm, 1), 0)
    return jnp.where(row < n_ctx, mod_ref[0, 0], mod_ref[0, 1])


def _peer_route_body(x_ref, sh_ref, sc_ref, wq_ref, keys_ref, h_ref, i_ref, j_ref, g_ref,
                     sv_ref, si_ref, cand_ref, best_ref, bi_ref, bj_ref, s_ref, *, tm, n_ctx):
    tile = pl.program_id(1)
    h = _rms_modulate(x_ref[0], _segment_rows(sh_ref, tile, tm, n_ctx),
                      _segment_rows(sc_ref, tile, tm, n_ctx)).astype(BF16)
    h_ref[0] = h
    q = jnp.dot(h, wq_ref[...], preferred_element_type=F32).astype(BF16)
    nt = (((1,), (1,)), ((), ()))
    neg = jnp.float32(-jnp.inf)

    def best_of(nodes):
        while len(nodes) > 1:
            nxt = []
            for (va, ia), (vb, ib) in zip(nodes[0::2], nodes[1::2]):
                left = va >= vb
                nxt.append((jnp.maximum(va, vb), jnp.where(left, ia, ib)))
            nodes = nxt
        return nodes[0]

    for m in range(2):
        for col in range(tm // V7X_LANES):
            toks = slice(col * V7X_LANES, (col + 1) * V7X_LANES)
            per_head = [
                lax.dot_general(keys_ref[m], q[toks, (hd * 2 + m) * N_KEYS:(hd * 2 + m + 1) * N_KEYS], nt,
                                preferred_element_type=F32) for hd in range(PEER_HEADS)]
            s_ref[...] = jnp.swapaxes(jnp.stack(per_head, axis=0), 0, 1)

            def extract(a, carry, m=m, toks=toks):
                vals = [s_ref[n] for n in range(N_KEYS)]
                mx, idx = best_of([(vals[n], jnp.float32(n)) for n in range(N_KEYS)])
                sv_ref[m, a, :, toks] = mx
                si_ref[m, a, :, toks] = idx
                for n in range(N_KEYS):
                    s_ref[n] = jnp.where(idx == jnp.float32(n), neg, vals[n])
                return carry

            lax.fori_loop(0, PEER_TOPK, extract, 0)

    cands = _peer_candidates()
    for c, (a, b) in enumerate(cands):
        cand_ref[c] = sv_ref[0, a] + sv_ref[1, b]
    big = jnp.float32(PEER_TOPK * PEER_TOPK)

    def pick(k, carry):
        vals = [cand_ref[c] for c in range(len(cands))]
        mx = functools.reduce(jnp.maximum, vals)
        pos = functools.reduce(
            jnp.minimum,
            [jnp.where(vals[c] == mx, jnp.float32(a * PEER_TOPK + b), big) for c, (a, b) in enumerate(cands)])
        ii = jnp.zeros_like(pos)
        jj = jnp.zeros_like(pos)
        for c, (a, b) in enumerate(cands):
            hit = pos == jnp.float32(a * PEER_TOPK + b)
            cand_ref[c] = jnp.where(hit, neg, vals[c])
            ii = jnp.where(hit, si_ref[0, a], ii)
            jj = jnp.where(hit, si_ref[1, b], jj)
        best_ref[k] = mx
        bi_ref[k] = ii
        bj_ref[k] = jj
        return carry

    lax.fori_loop(0, PEER_TOPK, pick, 0)
    best = best_ref[...]
    e = jnp.exp(best - best[0][None])
    g = e / jnp.sum(e, axis=0, keepdims=True)
    g_ref[0] = g.reshape(PEER_TOPK * PEER_HEADS, tm).T
    i_ref[0] = bi_ref[...].reshape(PEER_TOPK * PEER_HEADS, tm).T
    j_ref[0] = bj_ref[...].reshape(PEER_TOPK * PEER_HEADS, tm).T


PEER_TILE_TARGET = 512


def _peer_tile(l):
    t = (PEER_TILE_TARGET // V7X_LANES) * V7X_LANES
    while l % t:
        t -= V7X_LANES
    return t


def _mod_pair_spec(d):
    return pl.BlockSpec((1, 2, 1, d), lambda b, *_: (b, 0, 0, 0))


def peer_route(xs, shift, scale, w_q, keys, n_ctx):
    b, l, d = xs.shape
    tm = _peer_tile(l)
    nsel = PEER_TOPK * PEER_HEADS
    ncand = len(_peer_candidates())
    sel_spec = pl.BlockSpec((1, tm, nsel), lambda i, j: (i, j, 0))
    return pl.pallas_call(
        functools.partial(_peer_route_body, tm=tm, n_ctx=n_ctx),
        grid=(b, l // tm),
        in_specs=[
            pl.BlockSpec((1, tm, d), lambda i, j: (i, j, 0)),
            _mod_pair_spec(d),
            _mod_pair_spec(d),
            pl.BlockSpec(w_q.shape, lambda i, j: (0, 0)),
            pl.BlockSpec(keys.shape, lambda i, j: (0, 0, 0)),
        ],
        out_specs=[pl.BlockSpec((1, tm, d), lambda i, j: (i, j, 0)), sel_spec, sel_spec, sel_spec],
        out_shape=[
            jax.ShapeDtypeStruct((b, l, d), BF16),
            jax.ShapeDtypeStruct((b, l, nsel), F32),
            jax.ShapeDtypeStruct((b, l, nsel), F32),
            jax.ShapeDtypeStruct((b, l, nsel), F32),
        ],
        scratch_shapes=[
            pltpu.VMEM((2, PEER_TOPK, PEER_HEADS, tm), F32),
            pltpu.VMEM((2, PEER_TOPK, PEER_HEADS, tm), F32),
            pltpu.VMEM((ncand, PEER_HEADS, tm), F32),
            pltpu.VMEM((PEER_TOPK, PEER_HEADS, tm), F32),
            pltpu.VMEM((PEER_TOPK, PEER_HEADS, tm), F32),
            pltpu.VMEM((PEER_TOPK, PEER_HEADS, tm), F32),
            pltpu.VMEM((N_KEYS, PEER_HEADS, V7X_LANES), F32),
        ],
        compiler_params=_cparams(("parallel", "parallel")),
        name="peer_route",
    )(xs, shift, scale, w_q.astype(BF16), keys.astype(BF16))


PEER_G_CHUNK = 16
PEER_PAIR = 2 * N_KEYS
PEER_EXPERT_BLOCK = 2048


def _peer_expert_body(h_ref, i_ref, j_ref, g_ref, u_ref, v_ref, x_ref, gate_ref, o_ref, gw_ref, acc_ref, wt_ref,
                      *, tm, eb, n_ctx):
    e = pl.program_id(2)

    @pl.when(e == 0)
    def _():
        acc_ref[...] = jnp.zeros_like(acc_ref)
        isel, jsel, gsel = i_ref[0], j_ref[0], g_ref[0]
        kio = lax.broadcasted_iota(I32, (PEER_G_CHUNK, N_KEYS, isel.shape[1]), 1).astype(F32)
        for c in range(tm // PEER_G_CHUNK):
            rows = slice(c * PEER_G_CHUNK, (c + 1) * PEER_G_CHUNK)
            oh_i = jnp.where(isel[rows][:, None, :] == kio, 1.0, 0.0).astype(BF16)
            oh_j = jnp.where(jsel[rows][:, None, :] == kio, gsel[rows][:, None, :], 0.0).astype(BF16)
            g3 = jnp.einsum("pis,pjs->pij", oh_i, oh_j, preferred_element_type=F32)
            gw_ref[:, rows, :] = jnp.swapaxes(g3, 0, 1).astype(BF16)

    h = h_ref[0]
    for ip in range(eb // PEER_PAIR):
        rows = slice(ip * PEER_PAIR, (ip + 1) * PEER_PAIR)
        act = jax.nn.gelu(jnp.dot(h, u_ref[:, rows], preferred_element_type=F32)).astype(BF16)
        i0 = e * (eb // N_KEYS) + ip * 2
        wt_ref[:, ip * PEER_PAIR:ip * PEER_PAIR + N_KEYS] = act[:, :N_KEYS] * gw_ref[i0]
        wt_ref[:, ip * PEER_PAIR + N_KEYS:(ip + 1) * PEER_PAIR] = act[:, N_KEYS:] * gw_ref[i0 + 1]
    acc_ref[...] += jnp.dot(wt_ref[...], v_ref[...], preferred_element_type=F32)

    @pl.when(e == pl.num_programs(2) - 1)
    def _():
        gate = _segment_rows(gate_ref, pl.program_id(1), tm, n_ctx)
        o_ref[0] = x_ref[0] + gate * acc_ref[...]


def peer_experts(h, isel, jsel, gsel, u_t, v_tab, xs, gate, n_ctx):
    b, l, d = xs.shape
    ne = v_tab.shape[0]
    tm = _peer_tile(l)
    eb = PEER_EXPERT_BLOCK if tm < PEER_TILE_TARGET else PEER_EXPERT_BLOCK // 2
    nsel = isel.shape[2]
    tok_spec = pl.BlockSpec((1, tm, d), lambda i, j, e: (i, j, 0))
    sel_spec = pl.BlockSpec((1, tm, nsel), lambda i, j, e: (i, j, 0))
    return pl.pallas_call(
        functools.partial(_peer_expert_body, tm=tm, eb=eb, n_ctx=n_ctx),
        grid=(b, l // tm, ne // eb),
        in_specs=[
            tok_spec, sel_spec, sel_spec, sel_spec,
            pl.BlockSpec((d, eb), lambda i, j, e: (0, e)),
            pl.BlockSpec((eb, d), lambda i, j, e: (e, 0)),
            tok_spec,
            _mod_pair_spec(d),
        ],
        out_specs=tok_spec,
        out_shape=jax.ShapeDtypeStruct((b, l, d), F32),
        scratch_shapes=[
            pltpu.VMEM((N_KEYS, tm, N_KEYS), BF16),
            pltpu.VMEM((tm, d), F32),
            pltpu.VMEM((tm, eb), BF16),
        ],
        compiler_params=_cparams(("parallel", "parallel", "arbitrary")),
        name="peer_experts",
    )(h, isel, jsel, gsel, u_t, v_tab, xs, gate)


def _group_ones(width, group):
    g = jnp.arange(width) // group
    return (g[:, None] == g[None, :]).astype(F32)


def _group_sum(x, ones_ref):
    return jnp.dot(x, ones_ref[...], precision=lax.Precision.HIGHEST, preferred_element_type=F32)


def _pad_rank_rows(w, lo, total):
    return jnp.pad(w, ((lo, total - lo - w.shape[0]), (0, 0)))


HALO_ROWS = V7X_SUBLANES


def _even_prep_body(xm_ref, xp_ref, xn_ref, sh_ref, sc_ref, win_ref, mu_ref, rvec_ref, wup_ref, aup_ref, gup_ref,
                    ones_ref, cw_ref, wa_ref, wx_ref, lvec_ref,
                    r_ref, decf_ref, decb_ref, ktf_ref, ktb_ref, kk_ref, akkf_ref, akkb_ref, v_ref,
                    bonus_ref, ga_ref, af_ref, uf_ref, ab_ref, ub_ref, gb_ref,
                    *, tm, ctx_tiles, n_tiles, width, a_proj):
    j = pl.program_id(1)
    first = jnp.logical_or(j == 0, j == ctx_tiles)
    last = jnp.logical_or(j == ctx_tiles - 1, j == n_tiles - 1)
    shift, scale = sh_ref[0, 0], sc_ref[0, 0]
    z = jnp.concatenate([_rms_modulate(ref[0], shift, scale) for ref in (xp_ref, xm_ref, xn_ref)], axis=0)
    proj = jnp.dot(z.astype(BF16), win_ref[...], preferred_element_type=F32)
    p = proj[HALO_ROWS:HALO_ROWS + tm]
    prev = jnp.where(first, 0.0, proj[:HALO_ROWS])
    nxt = jnp.where(last, 0.0, proj[HALO_ROWS + tm:])
    row = lax.broadcasted_iota(I32, (tm, 1), 0)

    def shifted(lo, hi, off):
        x = p[:, lo:hi]
        if off < 0:
            y = pltpu.roll(x, -off, axis=0)
            for s in range(-off):
                y = jnp.where(row == s, prev[HALO_ROWS + off + s:HALO_ROWS + off + s + 1, lo:hi], y)
        else:
            y = pltpu.roll(x, tm - off, axis=0)
            for s in range(off):
                y = jnp.where(row == tm - off + s, nxt[s:s + 1, lo:hi], y)
        return y

    pa = p[:, :a_proj]
    pa = pa + mu_ref[...] * (0.5 * (shifted(0, a_proj, -1) + shifted(0, a_proj, 1)) - pa)
    r = pa[:, :width]
    k = pa[:, width:2 * width]
    v = pa[:, 2 * width:3 * width]
    c0 = 3 * width
    wd = jnp.tanh(pa[:, c0:c0 + V7X_LANES]).astype(BF16)
    ad = pa[:, c0 + V7X_LANES:c0 + 2 * V7X_LANES].astype(BF16)
    gd = jax.nn.sigmoid(pa[:, c0 + 2 * V7X_LANES:c0 + 3 * V7X_LANES]).astype(BF16)
    w0_f, w0_b, a0_f, a0_b, k_k, k_a, r_k = (rvec_ref[n:n + 1] for n in range(7))

    kk = k * k_k
    kk = kk / jnp.maximum(jnp.sqrt(_group_sum(kk * kk, ones_ref)), KK_NORM_EPS)
    kts = []
    for w0, a0, d, dec_ref, kt_ref, akk_ref in ((w0_f, a0_f, 0, decf_ref, ktf_ref, akkf_ref),
                                                (w0_b, a0_b, 1, decb_ref, ktb_ref, akkb_ref)):
        lora_w = jnp.dot(wd, wup_ref[d], preferred_element_type=F32)
        dec_ref[0] = jnp.exp(-DECAY_SCALE * jax.nn.sigmoid(w0 + lora_w))
        a = jax.nn.sigmoid(a0 + jnp.dot(ad, aup_ref[d], preferred_element_type=F32))
        kt = k * (1.0 + (a - 1.0) * k_a)
        kt_ref[0] = kt
        akk_ref[0] = a * kk
        kts.append(kt)
    r_ref[0] = r
    kk_ref[0] = kk
    v_ref[0] = v
    bonus_ref[0] = _group_sum(r * (0.5 * (kts[0] + kts[1])) * r_k, ones_ref) * v
    ga_ref[0] = jnp.dot(gd, gup_ref[...], preferred_element_type=F32)

    b0 = a_proj
    conv_b, ba_f, ba_b, bx_f, bx_b, sp_f, sp_b = (lvec_ref[n:n + 1] for n in range(7))
    xb = (cw_ref[0:1] * shifted(b0, b0 + width, -2) + cw_ref[1:2] * shifted(b0, b0 + width, -1)
          + cw_ref[2:3] * p[:, b0:b0 + width] + cw_ref[3:4] * shifted(b0, b0 + width, 1) + conv_b)
    gb_ref[0] = jax.nn.gelu(p[:, b0 + width:b0 + 2 * width])
    xbb = xb.astype(BF16)
    for d, ba, bx, sp, a_ref, u_ref in ((0, ba_f, bx_f, sp_f, af_ref, uf_ref), (1, ba_b, bx_b, sp_b, ab_ref, ub_ref)):
        rg = jax.nn.sigmoid(jnp.dot(xbb, wa_ref[d], preferred_element_type=F32) + ba)
        ig = jax.nn.sigmoid(jnp.dot(xbb, wx_ref[d], preferred_element_type=F32) + bx)
        log_a = -LRU_C * rg * sp
        a_ref[0] = jnp.exp(log_a)
        th = jnp.tanh(log_a)
        u_ref[0] = jnp.sqrt(-2.0 * th / (1.0 - th)) * ig * xb


def even_prep(xs, shift, scale, w_in, n_ctx, mu, w0, w_up, a0, a_up, g_up, k_k, k_a, r_k, conv_w, conv_b, wa, ba, wx,
              bx, lam):
    b, l, d = xs.shape
    n_proj = w_in.shape[1]
    width = w0.shape[-1]
    a_proj = mu.shape[-1]
    assert CONV_W == 4 and 2 * DECAY_RANK == V7X_LANES and 2 * ICLR_RANK == V7X_LANES and GATE_RANK == V7X_LANES
    assert a_proj == 3 * width + 3 * V7X_LANES and n_proj == a_proj + 2 * width
    tm = _token_tile(n_ctx, l - n_ctx, TOKEN_TILE)
    n_tiles, ctx_tiles = l // tm, n_ctx // tm
    halo_per_tile = tm // HALO_ROWS
    last_halo = l // HALO_ROWS - 1
    rvec = jnp.concatenate([w0, a0, k_k[None], k_a[None], r_k.reshape(1, width), jnp.zeros((1, width), F32)], axis=0)
    lvec = jnp.concatenate([conv_b[None], ba, bx, jax.nn.softplus(-lam), jnp.zeros((1, width), F32)], axis=0)
    wup = jnp.stack([_pad_rank_rows(w_up[0], 0, V7X_LANES), _pad_rank_rows(w_up[1], DECAY_RANK, V7X_LANES)])
    aup = jnp.stack([_pad_rank_rows(a_up[0], 0, V7X_LANES), _pad_rank_rows(a_up[1], ICLR_RANK, V7X_LANES)])

    def block_diag(w):
        return jnp.stack([jax.scipy.linalg.block_diag(*w[d]) for d in range(2)]).astype(BF16)

    tok = pl.BlockSpec((1, tm, width), lambda i, j: (i, j, 0))

    def full(a):
        return pl.BlockSpec(a.shape, lambda i, j, nd=a.ndim: (0,) * nd)

    consts = [w_in.astype(BF16), mu[None], rvec, wup.astype(BF16), aup.astype(BF16), g_up.astype(BF16),
              _group_ones(width, A_HEAD_DIM), conv_w, block_diag(wa), block_diag(wx), lvec]
    return pl.pallas_call(
        functools.partial(_even_prep_body, tm=tm, ctx_tiles=ctx_tiles, n_tiles=n_tiles, width=width, a_proj=a_proj),
        grid=(b, n_tiles),
        in_specs=[
            pl.BlockSpec((1, tm, d), lambda i, j: (i, j, 0)),
            pl.BlockSpec((1, HALO_ROWS, d), lambda i, j: (i, jnp.maximum(j * halo_per_tile - 1, 0), 0)),
            pl.BlockSpec((1, HALO_ROWS, d), lambda i, j: (i, jnp.minimum((j + 1) * halo_per_tile, last_halo), 0)),
            _mod_spec(d, ctx_tiles),
            _mod_spec(d, ctx_tiles),
        ] + [full(a) for a in consts],
        out_specs=[tok] * 16,
        out_shape=[jax.ShapeDtypeStruct((b, l, width), F32)] * 16,
        compiler_params=_cparams(("parallel", "parallel")),
        name="even_prep",
    )(xs, xs, xs, shift, scale, *consts)


def _even_post_body(of_ref, ob_ref, bonus_ref, ga_ref, hf_ref, hb_ref, gb_ref, ln_ref, ones_ref,
                    x_ref, g_ref, w_ref, o_ref, *, width):
    o = of_ref[0] + ob_ref[0]
    inv_n = 1.0 / A_HEAD_DIM
    cen = o - _group_sum(o, ones_ref) * inv_n
    var = _group_sum(cen * cen, ones_ref) * inv_n
    on = cen * lax.rsqrt(var + GN_EPS) * ln_ref[0:1] + ln_ref[1:2]
    ya = ((on + bonus_ref[0]) * ga_ref[0]).astype(BF16)
    yb = ((hf_ref[0] + hb_ref[0]) * gb_ref[0]).astype(BF16)
    acc = (jnp.dot(ya, w_ref[:width], preferred_element_type=F32)
           + jnp.dot(yb, w_ref[width:], preferred_element_type=F32))
    o_ref[0] = x_ref[0] + g_ref[0, 0] * acc


def even_post(o_f, o_b, bonus, gate_a, h_f, h_b, gate_b, ln_w, ln_b, xs, gate, w_out, n_ctx):
    b, l, d = xs.shape
    width = o_f.shape[-1]
    tm = _token_tile(n_ctx, l - n_ctx, TOKEN_TILE)
    ctx_tiles = n_ctx // tm
    tok = pl.BlockSpec((1, tm, width), lambda i, j: (i, j, 0))
    ln = jnp.stack([ln_w, ln_b])
    ones = _group_ones(width, A_HEAD_DIM)
    return pl.pallas_call(
        functools.partial(_even_post_body, width=width),
        grid=(b, l // tm),
        in_specs=[tok] * 7 + [
            pl.BlockSpec(ln.shape, lambda i, j: (0, 0)),
            pl.BlockSpec(ones.shape, lambda i, j: (0, 0)),
            pl.BlockSpec((1, tm, d), lambda i, j: (i, j, 0)),
            _mod_spec(d, ctx_tiles),
            pl.BlockSpec(w_out.shape, lambda i, j: (0, 0)),
        ],
        out_specs=pl.BlockSpec((1, tm, d), lambda i, j: (i, j, 0)),
        out_shape=jax.ShapeDtypeStruct((b, l, d), F32),
        compiler_params=_cparams(("parallel", "parallel")),
        name="even_post",
    )(o_f, o_b, bonus, gate_a, h_f, h_b, gate_b, ln, ones, xs, gate, w_out.astype(BF16))


def _rwkv_chain_layout(z, heads):
    b, l, width = z.shape
    n = width // heads
    z = z.reshape(b, l, heads, n).transpose(1, 3, 0, 2).reshape(l, n, b * heads)
    z = jnp.pad(z, ((0, 0), (0, 0), (0, RWKV_CHAINS - b * heads)))
    return z.reshape(l, n, V7X_SUBLANES // RWKV_KPAR, V7X_LANES)


def _rwkv_keyed(z, heads):
    z = _rwkv_chain_layout(z, heads)
    l, n = z.shape[:2]
    return z.reshape(l, n // RWKV_KPAR, V7X_SUBLANES, V7X_LANES)


def _rwkv_valued(z, heads):
    z = _rwkv_chain_layout(z, heads)
    return jnp.concatenate([z] * RWKV_KPAR, axis=2)


def _rwkv_unchain(o, b, heads):
    l, n = o.shape[:2]
    o = o[:, :, :V7X_SUBLANES // RWKV_KPAR].reshape(l, n, RWKV_CHAINS)[:, :, :b * heads]
    return o.reshape(l, n, b, heads).transpose(2, 0, 3, 1).reshape(b, l, heads * n)


def _even_mixer(xs, shift, scale, gate, n_ctx, w_in, mu, w_out, w0, w_up, a0, a_up, g_up, k_k, k_a, r_k, ln_w, ln_b,
                conv_w, conv_b, wa, ba, wx, bx, lam):
    b = xs.shape[0]
    heads = w0.shape[-1] // A_HEAD_DIM
    assert b * heads <= RWKV_CHAINS
    (r, dec_f, dec_b, kt_f, kt_b, kk, akk_f, akk_b, v, bonus, gate_a, a_f, u_f, a_b, u_b, gate_b) = even_prep(
        xs, shift, scale, w_in, n_ctx, mu, w0, w_up, a0, a_up, g_up, k_k, k_a, r_k, conv_w, conv_b, wa, ba, wx, bx, lam)
    o_f, o_b = rwkv_scan(
        _rwkv_keyed(r, heads), _rwkv_keyed(dec_f, heads), _rwkv_keyed(dec_b, heads),
        _rwkv_keyed(kt_f, heads), _rwkv_keyed(kt_b, heads), _rwkv_keyed(kk, heads),
        _rwkv_keyed(akk_f, heads), _rwkv_keyed(akk_b, heads), _rwkv_valued(v, heads), n_ctx)
    h_f, h_b = diag_scan(a_f, u_f, a_b, u_b, n_ctx)
    return even_post(_rwkv_unchain(o_f, b, heads), _rwkv_unchain(o_b, b, heads), bonus, gate_a, h_f, h_b, gate_b,
                     ln_w, ln_b, xs, gate, w_out, n_ctx)


ROPE_HALF = C_QK_DIM // 4


def _qkv_body(x_ref, sh_ref, sc_ref, w_ref, ones_ref, g_ref, cos_ref, sin_ref, q_ref, k_ref, v_ref, *, width):
    z = _rms_modulate(x_ref[0], sh_ref[0, 0], sc_ref[0, 0])
    qkv = jnp.dot(z.astype(BF16), w_ref[...], preferred_element_type=F32)
    cos = cos_ref[...]
    sin = sin_ref[...]
    lane = lax.broadcasted_iota(I32, cos.shape, 1)
    first_half = (lane % (2 * ROPE_HALF)) < ROPE_HALF
    for part, out_ref in ((0, q_ref), (1, k_ref)):
        gain = g_ref[part:part + 1]
        for c in range(width // V7X_LANES):
            lo = part * width + c * V7X_LANES
            t = qkv[:, lo:lo + V7X_LANES]
            ms = _group_sum(t * t, ones_ref) * (1.0 / C_QK_DIM)
            t = t * lax.rsqrt(ms + RMS_EPS) * gain
            partner = jnp.where(first_half, pltpu.roll(t, V7X_LANES - ROPE_HALF, axis=1),
                                pltpu.roll(t, ROPE_HALF, axis=1))
            t = t * cos + partner * sin
            if part == 0:
                out_ref[0, :, c * V7X_LANES:(c + 1) * V7X_LANES] = t.astype(BF16)
            else:
                out_ref[0, c * V7X_LANES:(c + 1) * V7X_LANES, :] = t.T.astype(BF16)
    v_ref[0] = qkv[:, 2 * width:].astype(BF16)


def _rope_tables(n_ctx, n_lat):
    n_rows = n_lat // GRID_W
    row_pos = jnp.repeat(jnp.arange(n_rows), GRID_W).astype(F32)
    col_pos = jnp.tile(jnp.arange(GRID_W), n_rows).astype(F32)
    inv_freq = ROPE_BASE ** (-jnp.arange(ROPE_HALF, dtype=F32) / ROPE_HALF)

    def one(pos):
        ang = pos[:, None] * inv_freq
        c, s = jnp.cos(ang), jnp.sin(ang)
        return jnp.concatenate([c, c], axis=-1), jnp.concatenate([-s, s], axis=-1)

    (cr, sr), (cc, sc) = one(row_pos), one(col_pos)
    cos = jnp.concatenate([cr, cc], axis=-1)
    sin = jnp.concatenate([sr, sc], axis=-1)
    cos = jnp.concatenate([jnp.ones((n_ctx, C_QK_DIM), F32), cos], axis=0)
    sin = jnp.concatenate([jnp.zeros((n_ctx, C_QK_DIM), F32), sin], axis=0)
    reps = V7X_LANES // C_QK_DIM
    return jnp.tile(cos, (1, reps)), jnp.tile(sin, (1, reps))


def qkv_project(xs, shift, scale, w_qkv, q_g, k_g, n_ctx):
    b, l, d = xs.shape
    width = w_qkv.shape[1] // 3
    tm = _token_tile(n_ctx, l - n_ctx, TOKEN_TILE)
    ctx_tiles = n_ctx // tm
    cos, sin = _rope_tables(n_ctx, l - n_ctx)
    gains = jnp.stack([jnp.tile(q_g, V7X_LANES // C_QK_DIM), jnp.tile(k_g, V7X_LANES // C_QK_DIM)])
    ones = _group_ones(V7X_LANES, C_QK_DIM)
    out = pl.BlockSpec((1, tm, width), lambda i, j: (i, j, 0))
    return pl.pallas_call(
        functools.partial(_qkv_body, width=width),
        grid=(b, l // tm),
        in_specs=[
            pl.BlockSpec((1, tm, d), lambda i, j: (i, j, 0)),
            _mod_spec(d, ctx_tiles),
            _mod_spec(d, ctx_tiles),
            pl.BlockSpec(w_qkv.shape, lambda i, j: (0, 0)),
            pl.BlockSpec(ones.shape, lambda i, j: (0, 0)),
            pl.BlockSpec(gains.shape, lambda i, j: (0, 0)),
            pl.BlockSpec((tm, V7X_LANES), lambda i, j: (j, 0)),
            pl.BlockSpec((tm, V7X_LANES), lambda i, j: (j, 0)),
        ],
        out_specs=[out, pl.BlockSpec((1, width, tm), lambda i, j: (i, 0, j)), out],
        out_shape=[jax.ShapeDtypeStruct((b, l, width), BF16), jax.ShapeDtypeStruct((b, width, l), BF16),
                   jax.ShapeDtypeStruct((b, l, width), BF16)],
        compiler_params=_cparams(("parallel", "parallel")),
        name="qkv_project",
    )(xs, shift, scale, w_qkv.astype(BF16), ones, gains, cos, sin)


def kernel(x, c, ctx, c_ctx, w_mod, b_mod, even_w_in, even_mu, even_w_out, rwkv_w0, rwkv_w_up, rwkv_a0, rwkv_a_up, rwkv_g_up, rwkv_k_k, rwkv_k_a, rwkv_r_k, rwkv_ln_w, rwkv_ln_b, lru_conv_w, lru_conv_b, lru_wa, lru_ba, lru_wx, lru_bx, lru_lam, attn_w_qkv, attn_q_g, attn_k_g, attn_lam_q1, attn_lam_k1, attn_lam_q2, attn_lam_k2, attn_sub_g, attn_w_o, peer_w_q, peer_keys, peer_u, peer_v):
    bsz, n_lat, d = x.shape
    n_ctx = ctx.shape[1]
    depth = w_mod.shape[0]
    a_proj = even_mu.shape[-1]
    s_lat = jax.nn.silu(c)
    s_ctx = jax.nn.silu(c_ctx)
    xs = jnp.concatenate([ctx, x], axis=1)
    for layer in range(depth):
        i = layer // 2
        m_l = s_lat @ w_mod[layer] + b_mod[layer]
        m_c = s_ctx @ w_mod[layer] + b_mod[layer]
        mods = jnp.stack([jnp.broadcast_to(m_c, m_l.shape), m_l], axis=1).reshape(bsz, 2, N_MOD, 1, d)
        shift1, scale1, gate1, shift2, scale2, gate2 = (mods[:, :, n] for n in range(N_MOD))
        if layer % 2 == 0:
            xs = _even_mixer(xs, shift1, scale1, gate1, n_ctx, even_w_in[i], even_mu[i], even_w_out[i],
                             rwkv_w0[i], rwkv_w_up[i], rwkv_a0[i], rwkv_a_up[i], rwkv_g_up[i], rwkv_k_k[i],
                             rwkv_k_a[i], rwkv_r_k[i], rwkv_ln_w[i], rwkv_ln_b[i], lru_conv_w[i], lru_conv_b[i],
                             lru_wa[i], lru_ba[i], lru_wx[i], lru_bx[i], lru_lam[i])
        else:
            lam_init = 0.8 - 0.6 * math.exp(-0.3 * layer)
            q, k, v = qkv_project(xs, shift1, scale1, attn_w_qkv[i], attn_q_g[i], attn_k_g[i], n_ctx)
            lam = (jnp.exp(jnp.sum(attn_lam_q1[i] * attn_lam_k1[i]))
                   - jnp.exp(jnp.sum(attn_lam_q2[i] * attn_lam_k2[i])) + lam_init)
            o = diff_attention_core(q, k, v, lam.reshape(1), attn_sub_g[i].reshape(1, C_V_DIM), n_ctx, lam_init)
            xs = gated_out_proj(o, xs, gate1, attn_w_o[i], n_ctx)
        seg_ctx = n_ctx if layer < depth - 1 else 0
        if layer == depth - 1:
            xs = xs[:, n_ctx:]
        h, isel, jsel, gsel = peer_route(xs, shift2, scale2, peer_w_q[layer], peer_keys[layer], seg_ctx)
        xs = peer_experts(h, isel, jsel, gsel, peer_u[layer].astype(BF16).T, peer_v[layer].astype(BF16), xs,
                          gate2, seg_ctx)
    return xs
```

```python
import functools
import math

import jax
import jax.numpy as jnp
from jax import lax
from jax.experimental import pallas as pl
from jax.experimental.pallas import tpu as pltpu

F32 = jnp.float32
BF16 = jnp.bfloat16
I32 = jnp.int32

V7X_LANES = 128
V7X_SUBLANES = 8
V7X_VMEM_LIMIT_BYTES = 56 * 1024 * 1024

TOKEN_TILE = 256

GRID_W = 64
N_MOD = 6
A_HEAD_DIM = 64
DECAY_SCALE = math.exp(-0.5)
GN_EPS = 64e-5
DECAY_RANK = 64
ICLR_RANK = 64
GATE_RANK = 128
KK_NORM_EPS = 1e-12
CONV_W = 4
LRU_C = 8.0
C_QK_DIM = 64
C_V_DIM = 128
ROPE_BASE = 10000.0
N_KEYS = 128
PEER_HEADS = 8
PEER_TOPK = 16
RMS_EPS = 1e-6


def _cparams(sem):
    return pltpu.CompilerParams(dimension_semantics=sem, vmem_limit_bytes=V7X_VMEM_LIMIT_BYTES)


def _token_tile(n_ctx, n_lat, want):
    t = want
    while n_ctx % t or n_lat % t:
        t //= 2
    return t


def _mod_spec(d, ctx_tiles):
    return pl.BlockSpec((1, 1, 1, d), lambda b, j, *_: (b, jnp.where(j >= ctx_tiles, 1, 0), 0, 0))


def _rms_modulate(x, shift, scale):
    xn = x * lax.rsqrt(jnp.mean(x * x, axis=-1, keepdims=True) + RMS_EPS)
    return xn * (1.0 + scale) + shift


def _outproj_body(y_ref, x_ref, g_ref, w_ref, o_ref):
    acc = jnp.dot(y_ref[0].astype(BF16), w_ref[...], preferred_element_type=F32)
    o_ref[0] = x_ref[0] + g_ref[0, 0] * acc


def gated_out_proj(y, xs, gate, w, n_ctx):
    b, l, d = xs.shape
    k = y.shape[-1]
    tm = _token_tile(n_ctx, l - n_ctx, TOKEN_TILE)
    ctx_tiles = n_ctx // tm
    return pl.pallas_call(
        _outproj_body,
        grid=(b, l // tm),
        in_specs=[
            pl.BlockSpec((1, tm, k), lambda i, j: (i, j, 0)),
            pl.BlockSpec((1, tm, d), lambda i, j: (i, j, 0)),
            _mod_spec(d, ctx_tiles),
            pl.BlockSpec((k, d), lambda i, j: (0, 0)),
        ],
        out_specs=pl.BlockSpec((1, tm, d), lambda i, j: (i, j, 0)),
        out_shape=jax.ShapeDtypeStruct((b, l, d), F32),
        compiler_params=_cparams(("parallel", "parallel")),
        name="gated_out_proj",
    )(y, xs, gate, w.astype(BF16))


RWKV_VBLK_A = 16
RWKV_VBLK_B = 8
RWKV_KPAR = 4
RWKV_CHAINS = (V7X_SUBLANES // RWKV_KPAR) * V7X_LANES
SCAN_TB = 8


def _reversed_block_map(n_blocks, ctx_blocks):
    def rev(i):
        return jnp.where(i < ctx_blocks, ctx_blocks - 1 - i, n_blocks - 1 - i + ctx_blocks)
    return rev


def _rwkv_scan_body(rf_ref, wf_ref, kf_ref, kkf_ref, akkf_ref, vf_ref,
                    rb_ref, wb_ref, kb_ref, kkb_ref, akkb_ref, vb_ref,
                    of_ref, ob_ref, tf_ref, tb_ref, sa_ref, *, tb, nk4, nv):
    @pl.when(pl.program_id(0) == 0)
    def _():
        tf_ref[...] = jnp.zeros_like(tf_ref)
        tb_ref[...] = jnp.zeros_like(tb_ref)

    def fold(acc):
        acc = acc + pltpu.roll(acc, V7X_SUBLANES // 2, axis=1)
        return acc + pltpu.roll(acc, V7X_SUBLANES // 4, axis=1)

    def one_direction(t, r_ref, w_ref, k_ref, kk_ref, akk_ref, v_ref, o_ref, t_ref):
        for vb in range(nv // RWKV_VBLK_A):
            vs = pl.ds(vb * RWKV_VBLK_A, RWKV_VBLK_A)

            def acc_sa(k4, acc):
                return acc + t_ref[k4, vs] * kk_ref[t, k4][None]

            acc = lax.fori_loop(0, nk4, acc_sa, jnp.zeros((RWKV_VBLK_A, V7X_SUBLANES, V7X_LANES), F32), unroll=True)
            sa_ref[vs] = fold(acc)
        for vb in range(nv // RWKV_VBLK_B):
            vs = pl.ds(vb * RWKV_VBLK_B, RWKV_VBLK_B)
            sa = sa_ref[vs]
            vv = v_ref[t, vs]

            def upd(k4, acc):
                tn = (t_ref[k4, vs] * w_ref[t, k4][None] - akk_ref[t, k4][None] * sa
                      + k_ref[t, k4][None] * vv)
                t_ref[k4, vs] = tn
                return acc + tn * r_ref[t, k4][None]

            acc = lax.fori_loop(0, nk4, upd, jnp.zeros((RWKV_VBLK_B, V7X_SUBLANES, V7X_LANES), F32), unroll=True)
            o_ref[t, vs] = fold(acc)[:, :V7X_SUBLANES // RWKV_KPAR]

    def step(t, carry):
        one_direction(t, rf_ref, wf_ref, kf_ref, kkf_ref, akkf_ref, vf_ref, of_ref, tf_ref)
        one_direction(tb - 1 - t, rb_ref, wb_ref, kb_ref, kkb_ref, akkb_ref, vb_ref, ob_ref, tb_ref)
        return carry

    lax.fori_loop(0, tb, step, 0)


def rwkv_scan(r, w_f, w_b, k_f, k_b, kk, akk_f, akk_b, v, n_ctx):
    l, nk4 = r.shape[0], r.shape[1]
    nv = v.shape[1]
    tb = SCAN_TB
    rev = _reversed_block_map(l // tb, n_ctx // tb)
    kblk = (tb, nk4, V7X_SUBLANES, V7X_LANES)
    vblk = (tb, nv, V7X_SUBLANES, V7X_LANES)
    kf = pl.BlockSpec(kblk, lambda i: (i, 0, 0, 0))
    kb = pl.BlockSpec(kblk, lambda i: (rev(i), 0, 0, 0))
    vf = pl.BlockSpec(vblk, lambda i: (i, 0, 0, 0))
    vb = pl.BlockSpec(vblk, lambda i: (rev(i), 0, 0, 0))
    oblk = (tb, nv, V7X_SUBLANES // RWKV_KPAR, V7X_LANES)
    out = jax.ShapeDtypeStruct((l,) + oblk[1:], F32)
    state = pltpu.VMEM((nk4, nv, V7X_SUBLANES, V7X_LANES), F32)
    return pl.pallas_call(
        functools.partial(_rwkv_scan_body, tb=tb, nk4=nk4, nv=nv),
        grid=(l // tb,),
        in_specs=[kf, kf, kf, kf, kf, vf, kb, kb, kb, kb, kb, vb],
        out_specs=[pl.BlockSpec(oblk, lambda i: (i, 0, 0, 0)), pl.BlockSpec(oblk, lambda i: (rev(i), 0, 0, 0))],
        out_shape=[out, out],
        scratch_shapes=[state, state, pltpu.VMEM((nv, V7X_SUBLANES, V7X_LANES), F32)],
        compiler_params=_cparams(("arbitrary",)),
        name="rwkv_scan",
    )(r, w_f, k_f, kk, akk_f, v, r, w_b, k_b, kk, akk_b, v)


def _diag_scan_body(af_ref, uf_ref, ab_ref, ub_ref, of_ref, ob_ref, hf_ref, hb_ref, *, tb):
    @pl.when(pl.program_id(0) == 0)
    def _():
        hf_ref[...] = jnp.zeros_like(hf_ref)
        hb_ref[...] = jnp.zeros_like(hb_ref)

    def step(t, carry):
        hf, hb = carry
        hf = af_ref[:, t, :] * hf + uf_ref[:, t, :]
        of_ref[:, t, :] = hf
        tr = tb - 1 - t
        hb = ab_ref[:, tr, :] * hb + ub_ref[:, tr, :]
        ob_ref[:, tr, :] = hb
        return hf, hb

    hf, hb = lax.fori_loop(0, tb, step, (hf_ref[...], hb_ref[...]))
    hf_ref[...] = hf
    hb_ref[...] = hb


def diag_scan(a_f, u_f, a_b, u_b, n_ctx):
    b, l, c = a_f.shape
    tb = SCAN_TB
    rev = _reversed_block_map(l // tb, n_ctx // tb)
    fwd = pl.BlockSpec((b, tb, c), lambda i: (0, i, 0))
    bwd = pl.BlockSpec((b, tb, c), lambda i: (0, rev(i), 0))
    out = jax.ShapeDtypeStruct((b, l, c), F32)
    return pl.pallas_call(
        functools.partial(_diag_scan_body, tb=tb),
        grid=(l // tb,),
        in_specs=[fwd, fwd, bwd, bwd],
        out_specs=[fwd, bwd],
        out_shape=[out, out],
        scratch_shapes=[pltpu.VMEM((b, c), F32), pltpu.VMEM((b, c), F32)],
        compiler_params=_cparams(("arbitrary",)),
        name="diag_scan",
    )(a_f, u_f, a_b, u_b)


ATTN_ROW_GROUPS = 2


def _attn_body(lam_ref, q_ref, k_ref, v_ref, sg_ref, o_ref, *, ctx_tiles, n_ctx, scale, out_scale):
    j = pl.program_id(2)
    q = q_ref[0]
    lane = lax.broadcasted_iota(I32, q.shape, 1)
    qs = q * jnp.asarray(scale, BF16)
    q1 = jnp.where(lane < C_QK_DIM, qs, jnp.zeros_like(qs))
    q2 = jnp.where(lane >= C_QK_DIM, qs, jnp.zeros_like(qs))
    lam = lam_ref[0]

    def attend(kt, vb):
        rows = q.shape[0] // ATTN_ROW_GROUPS
        pieces = [qm[r * rows:(r + 1) * rows] for r in range(ATTN_ROW_GROUPS) for qm in (q1, q2)]

        def softmax_times_v(s):
            e = jnp.exp(s - jnp.max(s, axis=-1, keepdims=True))
            pv = jnp.dot(e.astype(BF16), vb, preferred_element_type=F32)
            return pv / jnp.sum(e, axis=-1, keepdims=True)

        ahead = 2
        scores = [jnp.dot(qp, kt, preferred_element_type=F32) for qp in pieces[:ahead]]
        outs = []
        for n in range(len(pieces)):
            if n + ahead < len(pieces):
                scores.append(jnp.dot(pieces[n + ahead], kt, preferred_element_type=F32))
            outs.append(softmax_times_v(scores[n]))
        o = jnp.concatenate([outs[2 * r] - lam * outs[2 * r + 1] for r in range(ATTN_ROW_GROUPS)], axis=0)
        o = o * lax.rsqrt(jnp.mean(o * o, axis=-1, keepdims=True) + RMS_EPS)
        o_ref[0] = o * sg_ref[...] * out_scale

    @pl.when(j < ctx_tiles)
    def _():
        attend(k_ref[0, :, :n_ctx], v_ref[0, :n_ctx])

    @pl.when(j >= ctx_tiles)
    def _():
        attend(k_ref[0], v_ref[0])


def diff_attention_core(q, k_t, v, lam, sub_g, n_ctx, lam_init):
    b, l, width = q.shape
    heads = width // C_V_DIM
    assert n_ctx % V7X_LANES == 0
    tq = _token_tile(n_ctx, l - n_ctx, TOKEN_TILE)
    ctx_tiles = n_ctx // tq
    scale = C_QK_DIM ** -0.5
    assert math.frexp(scale)[0] == 0.5, "the kernel folds the score scale into q, exact only for a power of two"
    body = functools.partial(_attn_body, ctx_tiles=ctx_tiles, n_ctx=n_ctx, scale=scale, out_scale=1.0 - lam_init)
    return pl.pallas_call(
        body,
        grid=(b, heads, l // tq),
        in_specs=[
            pl.BlockSpec(memory_space=pltpu.SMEM),
            pl.BlockSpec((1, tq, C_V_DIM), lambda i, h, j: (i, j, h)),
            pl.BlockSpec((1, C_V_DIM, l), lambda i, h, j: (i, h, 0)),
            pl.BlockSpec((1, l, C_V_DIM), lambda i, h, j: (i, 0, h)),
            pl.BlockSpec((1, C_V_DIM), lambda i, h, j: (0, 0)),
        ],
        out_specs=pl.BlockSpec((1, tq, C_V_DIM), lambda i, h, j: (i, j, h)),
        out_shape=jax.ShapeDtypeStruct((b, l, width), F32),
        compiler_params=_cparams(("parallel", "parallel", "parallel")),
        name="diff_attention",
    )(lam, q, k_t, v, sub_g)


def _peer_candidates():
    return [(a, b) for a in range(PEER_TOPK) for b in range(PEER_TOPK) if (a + 1) * (b + 1) <= PEER_TOPK]


def _segment_rows(mod_ref, tile, tm, n_ctx):
    row = tile * tm + lax.broadcasted_iota(I32, (tm, 1), 0)
    return jnp.where(row < n_ctx, mod_ref[0, 0], mod_ref[0, 1])


def _peer_route_body(x_ref, sh_ref, sc_ref, wq_ref, keys_ref, h_ref, i_ref, j_ref, g_ref,
                     sv_ref, si_ref, cand_ref, best_ref, bi_ref, bj_ref, s_ref, *, tm, n_ctx):
    tile = pl.program_id(1)
    h = _rms_modulate(x_ref[0], _segment_rows(sh_ref, tile, tm, n_ctx),
                      _segment_rows(sc_ref, tile, tm, n_ctx)).astype(BF16)
    h_ref[0] = h
    q = jnp.dot(h, wq_ref[...], preferred_element_type=F32).astype(BF16)
    nt = (((1,), (1,)), ((), ()))
    neg = jnp.float32(-jnp.inf)

    def best_of(nodes):
        while len(nodes) > 1:
            nxt = []
            for (va, ia), (vb, ib) in zip(nodes[0::2], nodes[1::2]):
                left = va >= vb
                nxt.append((jnp.maximum(va, vb), jnp.where(left, ia, ib)))
            nodes = nxt
        return nodes[0]

    for m in range(2):
        for col in range(tm // V7X_LANES):
            toks = slice(col * V7X_LANES, (col + 1) * V7X_LANES)
            per_head = [
                lax.dot_general(keys_ref[m], q[toks, (hd * 2 + m) * N_KEYS:(hd * 2 + m + 1) * N_KEYS], nt,
                                preferred_element_type=F32) for hd in range(PEER_HEADS)]
            s_ref[...] = jnp.swapaxes(jnp.stack(per_head, axis=0), 0, 1)

            def extract(a, carry, m=m, toks=toks):
                vals = [s_ref[n] for n in range(N_KEYS)]
                mx, idx = best_of([(vals[n], jnp.float32(n)) for n in range(N_KEYS)])
                sv_ref[m, a, :, toks] = mx
                si_ref[m, a, :, toks] = idx
                for n in range(N_KEYS):
                    s_ref[n] = jnp.where(idx == jnp.float32(n), neg, vals[n])
                return carry

            lax.fori_loop(0, PEER_TOPK, extract, 0)

    cands = _peer_candidates()
    for c, (a, b) in enumerate(cands):
        cand_ref[c] = sv_ref[0, a] + sv_ref[1, b]
    big = jnp.float32(PEER_TOPK * PEER_TOPK)

    def pick(k, carry):
        vals = [cand_ref[c] for c in range(len(cands))]
        mx = functools.reduce(jnp.maximum, vals)
        pos = functools.reduce(
            jnp.minimum,
            [jnp.where(vals[c] == mx, jnp.float32(a * PEER_TOPK + b), big) for c, (a, b) in enumerate(cands)])
        ii = jnp.zeros_like(pos)
        jj = jnp.zeros_like(pos)
        for c, (a, b) in enumerate(cands):
            hit = pos == jnp.float32(a * PEER_TOPK + b)
            cand_ref[c] = jnp.where(hit, neg, vals[c])
            ii = jnp.where(hit, si_ref[0, a], ii)
            jj = jnp.where(hit, si_ref[1, b], jj)
        best_ref[k] = mx
        bi_ref[k] = ii
        bj_ref[k] = jj
        return carry

    lax.fori_loop(0, PEER_TOPK, pick, 0)
    best = best_ref[...]
    e = jnp.exp(best - best[0][None])
    g = e / jnp.sum(e, axis=0, keepdims=True)
    g_ref[0] = g.reshape(PEER_TOPK * PEER_HEADS, tm).T
    i_ref[0] = bi_ref[...].reshape(PEER_TOPK * PEER_HEADS, tm).T
    j_ref[0] = bj_ref[...].reshape(PEER_TOPK * PEER_HEADS, tm).T


PEER_TILE_TARGET = 512


def _peer_tile(l):
    t = (PEER_TILE_TARGET // V7X_LANES) * V7X_LANES
    while l % t:
        t -= V7X_LANES
    return t


def _mod_pair_spec(d):
    return pl.BlockSpec((1, 2, 1, d), lambda b, *_: (b, 0, 0, 0))


def peer_route(xs, shift, scale, w_q, keys, n_ctx):
    b, l, d = xs.shape
    tm = _peer_tile(l)
    nsel = PEER_TOPK * PEER_HEADS
    ncand = len(_peer_candidates())
    sel_spec = pl.BlockSpec((1, tm, nsel), lambda i, j: (i, j, 0))
    return pl.pallas_call(
        functools.partial(_peer_route_body, tm=tm, n_ctx=n_ctx),
        grid=(b, l // tm),
        in_specs=[
            pl.BlockSpec((1, tm, d), lambda i, j: (i, j, 0)),
            _mod_pair_spec(d),
            _mod_pair_spec(d),
            pl.BlockSpec(w_q.shape, lambda i, j: (0, 0)),
            pl.BlockSpec(keys.shape, lambda i, j: (0, 0, 0)),
        ],
        out_specs=[pl.BlockSpec((1, tm, d), lambda i, j: (i, j, 0)), sel_spec, sel_spec, sel_spec],
        out_shape=[
            jax.ShapeDtypeStruct((b, l, d), BF16),
            jax.ShapeDtypeStruct((b, l, nsel), F32),
            jax.ShapeDtypeStruct((b, l, nsel), F32),
            jax.ShapeDtypeStruct((b, l, nsel), F32),
        ],
        scratch_shapes=[
            pltpu.VMEM((2, PEER_TOPK, PEER_HEADS, tm), F32),
            pltpu.VMEM((2, PEER_TOPK, PEER_HEADS, tm), F32),
            pltpu.VMEM((ncand, PEER_HEADS, tm), F32),
            pltpu.VMEM((PEER_TOPK, PEER_HEADS, tm), F32),
            pltpu.VMEM((PEER_TOPK, PEER_HEADS, tm), F32),
            pltpu.VMEM((PEER_TOPK, PEER_HEADS, tm), F32),
            pltpu.VMEM((N_KEYS, PEER_HEADS, V7X_LANES), F32),
        ],
        compiler_params=_cparams(("parallel", "parallel")),
        name="peer_route",
    )(xs, shift, scale, w_q.astype(BF16), keys.astype(BF16))


PEER_G_CHUNK = 16
PEER_PAIR = 2 * N_KEYS
PEER_EXPERT_BLOCK = 2048


def _peer_expert_body(h_ref, i_ref, j_ref, g_ref, u_ref, v_ref, x_ref, gate_ref, o_ref, gw_ref, acc_ref, wt_ref,
                      *, tm, eb, n_ctx):
    e = pl.program_id(2)

    @pl.when(e == 0)
    def _():
        acc_ref[...] = jnp.zeros_like(acc_ref)
        isel, jsel, gsel = i_ref[0], j_ref[0], g_ref[0]
        kio = lax.broadcasted_iota(I32, (PEER_G_CHUNK, N_KEYS, isel.shape[1]), 1).astype(F32)
        for c in range(tm // PEER_G_CHUNK):
            rows = slice(c * PEER_G_CHUNK, (c + 1) * PEER_G_CHUNK)
            oh_i = jnp.where(isel[rows][:, None, :] == kio, 1.0, 0.0).astype(BF16)
            oh_j = jnp.where(jsel[rows][:, None, :] == kio, gsel[rows][:, None, :], 0.0).astype(BF16)
            g3 = jnp.einsum("pis,pjs->pij", oh_i, oh_j, preferred_element_type=F32)
            gw_ref[:, rows, :] = jnp.swapaxes(g3, 0, 1).astype(BF16)

    h = h_ref[0]
    for ip in range(eb // PEER_PAIR):
        rows = slice(ip * PEER_PAIR, (ip + 1) * PEER_PAIR)
        act = jax.nn.gelu(jnp.dot(h, u_ref[:, rows], preferred_element_type=F32)).astype(BF16)
        i0 = e * (eb // N_KEYS) + ip * 2
        wt_ref[:, ip * PEER_PAIR:ip * PEER_PAIR + N_KEYS] = act[:, :N_KEYS] * gw_ref[i0]
        wt_ref[:, ip * PEER_PAIR + N_KEYS:(ip + 1) * PEER_PAIR] = act[:, N_KEYS:] * gw_ref[i0 + 1]
    acc_ref[...] += jnp.dot(wt_ref[...], v_ref[...], preferred_element_type=F32)

    @pl.when(e == pl.num_programs(2) - 1)
    def _():
        gate = _segment_rows(gate_ref, pl.program_id(1), tm, n_ctx)
        o_ref[0] = x_ref[0] + gate * acc_ref[...]


def peer_experts(h, isel, jsel, gsel, u_t, v_tab, xs, gate, n_ctx):
    b, l, d = xs.shape
    ne = v_tab.shape[0]
    tm = _peer_tile(l)
    eb = PEER_EXPERT_BLOCK if tm < PEER_TILE_TARGET else PEER_EXPERT_BLOCK // 2
    nsel = isel.shape[2]
    tok_spec = pl.BlockSpec((1, tm, d), lambda i, j, e: (i, j, 0))
    sel_spec = pl.BlockSpec((1, tm, nsel), lambda i, j, e: (i, j, 0))
    return pl.pallas_call(
        functools.partial(_peer_expert_body, tm=tm, eb=eb, n_ctx=n_ctx),
        grid=(b, l // tm, ne // eb),
        in_specs=[
            tok_spec, sel_spec, sel_spec, sel_spec,
            pl.BlockSpec((d, eb), lambda i, j, e: (0, e)),
            pl.BlockSpec((eb, d), lambda i, j, e: (e, 0)),
            tok_spec,
            _mod_pair_spec(d),
        ],
        out_specs=tok_spec,
        out_shape=jax.ShapeDtypeStruct((b, l, d), F32),
        scratch_shapes=[
            pltpu.VMEM((N_KEYS, tm, N_KEYS), BF16),
            pltpu.VMEM((tm, d), F32),
            pltpu.VMEM((tm, eb), BF16),
        ],
        compiler_params=_cparams(("parallel", "parallel", "arbitrary")),
        name="peer_experts",
    )(h, isel, jsel, gsel, u_t, v_tab, xs, gate)


def _group_ones(width, group):
    g = jnp.arange(width) // group
    return (g[:, None] == g[None, :]).astype(F32)


def _group_sum(x, ones_ref):
    return jnp.dot(x, ones_ref[...], precision=lax.Precision.HIGHEST, preferred_element_type=F32)


def _pad_rank_rows(w, lo, total):
    return jnp.pad(w, ((lo, total - lo - w.shape[0]), (0, 0)))


HALO_ROWS = V7X_SUBLANES


def _even_prep_body(xm_ref, xp_ref, xn_ref, sh_ref, sc_ref, win_ref, mu_ref, rvec_ref, wup_ref, aup_ref, gup_ref,
                    ones_ref, cw_ref, wa_ref, wx_ref, lvec_ref,
                    r_ref, decf_ref, decb_ref, ktf_ref, ktb_ref, kk_ref, akkf_ref, akkb_ref, v_ref,
                    bonus_ref, ga_ref, af_ref, uf_ref, ab_ref, ub_ref, gb_ref,
                    *, tm, ctx_tiles, n_tiles, width, a_proj):
    j = pl.program_id(1)
    first = jnp.logical_or(j == 0, j == ctx_tiles)
    last = jnp.logical_or(j == ctx_tiles - 1, j == n_tiles - 1)
    shift, scale = sh_ref[0, 0], sc_ref[0, 0]
    z = jnp.concatenate([_rms_modulate(ref[0], shift, scale) for ref in (xp_ref, xm_ref, xn_ref)], axis=0)
    proj = jnp.dot(z.astype(BF16), win_ref[...], preferred_element_type=F32)
    p = proj[HALO_ROWS:HALO_ROWS + tm]
    prev = jnp.where(first, 0.0, proj[:HALO_ROWS])
    nxt = jnp.where(last, 0.0, proj[HALO_ROWS + tm:])
    row = lax.broadcasted_iota(I32, (tm, 1), 0)

    def shifted(lo, hi, off):
        x = p[:, lo:hi]
        if off < 0:
            y = pltpu.roll(x, -off, axis=0)
            for s in range(-off):
                y = jnp.where(row == s, prev[HALO_ROWS + off + s:HALO_ROWS + off + s + 1, lo:hi], y)
        else:
            y = pltpu.roll(x, tm - off, axis=0)
            for s in range(off):
                y = jnp.where(row == tm - off + s, nxt[s:s + 1, lo:hi], y)
        return y

    pa = p[:, :a_proj]
    pa = pa + mu_ref[...] * (0.5 * (shifted(0, a_proj, -1) + shifted(0, a_proj, 1)) - pa)
    r = pa[:, :width]
    k = pa[:, width:2 * width]
    v = pa[:, 2 * width:3 * width]
    c0 = 3 * width
    wd = jnp.tanh(pa[:, c0:c0 + V7X_LANES]).astype(BF16)
    ad = pa[:, c0 + V7X_LANES:c0 + 2 * V7X_LANES].astype(BF16)
    gd = jax.nn.sigmoid(pa[:, c0 + 2 * V7X_LANES:c0 + 3 * V7X_LANES]).astype(BF16)
    w0_f, w0_b, a0_f, a0_b, k_k, k_a, r_k = (rvec_ref[n:n + 1] for n in range(7))

    kk = k * k_k
    kk = kk / jnp.maximum(jnp.sqrt(_group_sum(kk * kk, ones_ref)), KK_NORM_EPS)
    kts = []
    for w0, a0, d, dec_ref, kt_ref, akk_ref in ((w0_f, a0_f, 0, decf_ref, ktf_ref, akkf_ref),
                                                (w0_b, a0_b, 1, decb_ref, ktb_ref, akkb_ref)):
        lora_w = jnp.dot(wd, wup_ref[d], preferred_element_type=F32)
        dec_ref[0] = jnp.exp(-DECAY_SCALE * jax.nn.sigmoid(w0 + lora_w))
        a = jax.nn.sigmoid(a0 + jnp.dot(ad, aup_ref[d], preferred_element_type=F32))
        kt = k * (1.0 + (a - 1.0) * k_a)
        kt_ref[0] = kt
        akk_ref[0] = a * kk
        kts.append(kt)
    r_ref[0] = r
    kk_ref[0] = kk
    v_ref[0] = v
    bonus_ref[0] = _group_sum(r * (0.5 * (kts[0] + kts[1])) * r_k, ones_ref) * v
    ga_ref[0] = jnp.dot(gd, gup_ref[...], preferred_element_type=F32)

    b0 = a_proj
    conv_b, ba_f, ba_b, bx_f, bx_b, sp_f, sp_b = (lvec_ref[n:n + 1] for n in range(7))
    xb = (cw_ref[0:1] * shifted(b0, b0 + width, -2) + cw_ref[1:2] * shifted(b0, b0 + width, -1)
          + cw_ref[2:3] * p[:, b0:b0 + width] + cw_ref[3:4] * shifted(b0, b0 + width, 1) + conv_b)
    gb_ref[0] = jax.nn.gelu(p[:, b0 + width:b0 + 2 * width])
    xbb = xb.astype(BF16)
    for d, ba, bx, sp, a_ref, u_ref in ((0, ba_f, bx_f, sp_f, af_ref, uf_ref), (1, ba_b, bx_b, sp_b, ab_ref, ub_ref)):
        rg = jax.nn.sigmoid(jnp.dot(xbb, wa_ref[d], preferred_element_type=F32) + ba)
        ig = jax.nn.sigmoid(jnp.dot(xbb, wx_ref[d], preferred_element_type=F32) + bx)
        log_a = -LRU_C * rg * sp
        a_ref[0] = jnp.exp(log_a)
        th = jnp.tanh(log_a)
        u_ref[0] = jnp.sqrt(-2.0 * th / (1.0 - th)) * ig * xb


def even_prep(xs, shift, scale, w_in, n_ctx, mu, w0, w_up, a0, a_up, g_up, k_k, k_a, r_k, conv_w, conv_b, wa, ba, wx,
              bx, lam):
    b, l, d = xs.shape
    n_proj = w_in.shape[1]
    width = w0.shape[-1]
    a_proj = mu.shape[-1]
    assert CONV_W == 4 and 2 * DECAY_RANK == V7X_LANES and 2 * ICLR_RANK == V7X_LANES and GATE_RANK == V7X_LANES
    assert a_proj == 3 * width + 3 * V7X_LANES and n_proj == a_proj + 2 * width
    tm = _token_tile(n_ctx, l - n_ctx, TOKEN_TILE)
    n_tiles, ctx_tiles = l // tm, n_ctx // tm
    halo_per_tile = tm // HALO_ROWS
    last_halo = l // HALO_ROWS - 1
    rvec = jnp.concatenate([w0, a0, k_k[None], k_a[None], r_k.reshape(1, width), jnp.zeros((1, width), F32)], axis=0)
    lvec = jnp.concatenate([conv_b[None], ba, bx, jax.nn.softplus(-lam), jnp.zeros((1, width), F32)], axis=0)
    wup = jnp.stack([_pad_rank_rows(w_up[0], 0, V7X_LANES), _pad_rank_rows(w_up[1], DECAY_RANK, V7X_LANES)])
    aup = jnp.stack([_pad_rank_rows(a_up[0], 0, V7X_LANES), _pad_rank_rows(a_up[1], ICLR_RANK, V7X_LANES)])

    def block_diag(w):
        return jnp.stack([jax.scipy.linalg.block_diag(*w[d]) for d in range(2)]).astype(BF16)

    tok = pl.BlockSpec((1, tm, width), lambda i, j: (i, j, 0))

    def full(a):
        return pl.BlockSpec(a.shape, lambda i, j, nd=a.ndim: (0,) * nd)

    consts = [w_in.astype(BF16), mu[None], rvec, wup.astype(BF16), aup.astype(BF16), g_up.astype(BF16),
              _group_ones(width, A_HEAD_DIM), conv_w, block_diag(wa), block_diag(wx), lvec]
    return pl.pallas_call(
        functools.partial(_even_prep_body, tm=tm, ctx_tiles=ctx_tiles, n_tiles=n_tiles, width=width, a_proj=a_proj),
        grid=(b, n_tiles),
        in_specs=[
            pl.BlockSpec((1, tm, d), lambda i, j: (i, j, 0)),
            pl.BlockSpec((1, HALO_ROWS, d), lambda i, j: (i, jnp.maximum(j * halo_per_tile - 1, 0), 0)),
            pl.BlockSpec((1, HALO_ROWS, d), lambda i, j: (i, jnp.minimum((j + 1) * halo_per_tile, last_halo), 0)),
            _mod_spec(d, ctx_tiles),
            _mod_spec(d, ctx_tiles),
        ] + [full(a) for a in consts],
        out_specs=[tok] * 16,
        out_shape=[jax.ShapeDtypeStruct((b, l, width), F32)] * 16,
        compiler_params=_cparams(("parallel", "parallel")),
        name="even_prep",
    )(xs, xs, xs, shift, scale, *consts)


def _even_post_body(of_ref, ob_ref, bonus_ref, ga_ref, hf_ref, hb_ref, gb_ref, ln_ref, ones_ref,
                    x_ref, g_ref, w_ref, o_ref, *, width):
    o = of_ref[0] + ob_ref[0]
    inv_n = 1.0 / A_HEAD_DIM
    cen = o - _group_sum(o, ones_ref) * inv_n
    var = _group_sum(cen * cen, ones_ref) * inv_n
    on = cen * lax.rsqrt(var + GN_EPS) * ln_ref[0:1] + ln_ref[1:2]
    ya = ((on + bonus_ref[0]) * ga_ref[0]).astype(BF16)
    yb = ((hf_ref[0] + hb_ref[0]) * gb_ref[0]).astype(BF16)
    acc = (jnp.dot(ya, w_ref[:width], preferred_element_type=F32)
           + jnp.dot(yb, w_ref[width:], preferred_element_type=F32))
    o_ref[0] = x_ref[0] + g_ref[0, 0] * acc


def even_post(o_f, o_b, bonus, gate_a, h_f, h_b, gate_b, ln_w, ln_b, xs, gate, w_out, n_ctx):
    b, l, d = xs.shape
    width = o_f.shape[-1]
    tm = _token_tile(n_ctx, l - n_ctx, TOKEN_TILE)
    ctx_tiles = n_ctx // tm
    tok = pl.BlockSpec((1, tm, width), lambda i, j: (i, j, 0))
    ln = jnp.stack([ln_w, ln_b])
    ones = _group_ones(width, A_HEAD_DIM)
    return pl.pallas_call(
        functools.partial(_even_post_body, width=width),
        grid=(b, l // tm),
        in_specs=[tok] * 7 + [
            pl.BlockSpec(ln.shape, lambda i, j: (0, 0)),
            pl.BlockSpec(ones.shape, lambda i, j: (0, 0)),
            pl.BlockSpec((1, tm, d), lambda i, j: (i, j, 0)),
            _mod_spec(d, ctx_tiles),
            pl.BlockSpec(w_out.shape, lambda i, j: (0, 0)),
        ],
        out_specs=pl.BlockSpec((1, tm, d), lambda i, j: (i, j, 0)),
        out_shape=jax.ShapeDtypeStruct((b, l, d), F32),
        compiler_params=_cparams(("parallel", "parallel")),
        name="even_post",
    )(o_f, o_b, bonus, gate_a, h_f, h_b, gate_b, ln, ones, xs, gate, w_out.astype(BF16))


def _rwkv_chain_layout(z, heads):
    b, l, width = z.shape
    n = width // heads
    z = z.reshape(b, l, heads, n).transpose(1, 3, 0, 2).reshape(l, n, b * heads)
    z = jnp.pad(z, ((0, 0), (0, 0), (0, RWKV_CHAINS - b * heads)))
    return z.reshape(l, n, V7X_SUBLANES // RWKV_KPAR, V7X_LANES)


def _rwkv_keyed(z, heads):
    z = _rwkv_chain_layout(z, heads)
    l, n = z.shape[:2]
    return z.reshape(l, n // RWKV_KPAR, V7X_SUBLANES, V7X_LANES)


def _rwkv_valued(z, heads):
    z = _rwkv_chain_layout(z, heads)
    return jnp.concatenate([z] * RWKV_KPAR, axis=2)


def _rwkv_unchain(o, b, heads):
    l, n = o.shape[:2]
    o = o.reshape(l, n, RWKV_CHAINS)[:, :, :b * heads]
    return o.reshape(l, n, b, heads).transpose(2, 0, 3, 1).reshape(b, l, heads * n)


def _even_mixer(xs, shift, scale, gate, n_ctx, w_in, mu, w_out, w0, w_up, a0, a_up, g_up, k_k, k_a, r_k, ln_w, ln_b,
                conv_w, conv_b, wa, ba, wx, bx, lam):
    b = xs.shape[0]
    heads = w0.shape[-1] // A_HEAD_DIM
    assert b * heads <= RWKV_CHAINS
    (r, dec_f, dec_b, kt_f, kt_b, kk, akk_f, akk_b, v, bonus, gate_a, a_f, u_f, a_b, u_b, gate_b) = even_prep(
        xs, shift, scale, w_in, n_ctx, mu, w0, w_up, a0, a_up, g_up, k_k, k_a, r_k, conv_w, conv_b, wa, ba, wx, bx, lam)
    o_f, o_b = rwkv_scan(
        _rwkv_keyed(r, heads), _rwkv_keyed(dec_f, heads), _rwkv_keyed(dec_b, heads),
        _rwkv_keyed(kt_f, heads), _rwkv_keyed(kt_b, heads), _rwkv_keyed(kk, heads),
        _rwkv_keyed(akk_f, heads), _rwkv_keyed(akk_b, heads), _rwkv_valued(v, heads), n_ctx)
    h_f, h_b = diag_scan(a_f, u_f, a_b, u_b, n_ctx)
    return even_post(_rwkv_unchain(o_f, b, heads), _rwkv_unchain(o_b, b, heads), bonus, gate_a, h_f, h_b, gate_b,
                     ln_w, ln_b, xs, gate, w_out, n_ctx)


ROPE_HALF = C_QK_DIM // 4


def _qkv_body(x_ref, sh_ref, sc_ref, w_ref, ones_ref, g_ref, cos_ref, sin_ref, q_ref, k_ref, v_ref, *, width):
    z = _rms_modulate(x_ref[0], sh_ref[0, 0], sc_ref[0, 0])
    qkv = jnp.dot(z.astype(BF16), w_ref[...], preferred_element_type=F32)
    cos = cos_ref[...]
    sin = sin_ref[...]
    lane = lax.broadcasted_iota(I32, cos.shape, 1)
    first_half = (lane % (2 * ROPE_HALF)) < ROPE_HALF
    for part, out_ref in ((0, q_ref), (1, k_ref)):
        gain = g_ref[part:part + 1]
        for c in range(width // V7X_LANES):
            lo = part * width + c * V7X_LANES
            t = qkv[:, lo:lo + V7X_LANES]
            ms = _group_sum(t * t, ones_ref) * (1.0 / C_QK_DIM)
            t = t * lax.rsqrt(ms + RMS_EPS) * gain
            partner = jnp.where(first_half, pltpu.roll(t, V7X_LANES - ROPE_HALF, axis=1),
                                pltpu.roll(t, ROPE_HALF, axis=1))
            t = t * cos + partner * sin
            if part == 0:
                out_ref[0, :, c * V7X_LANES:(c + 1) * V7X_LANES] = t.astype(BF16)
            else:
                out_ref[0, c * V7X_LANES:(c + 1) * V7X_LANES, :] = t.T.astype(BF16)
    v_ref[0] = qkv[:, 2 * width:].astype(BF16)


def _rope_tables(n_ctx, n_lat):
    n_rows = n_lat // GRID_W
    row_pos = jnp.repeat(jnp.arange(n_rows), GRID_W).astype(F32)
    col_pos = jnp.tile(jnp.arange(GRID_W), n_rows).astype(F32)
    inv_freq = ROPE_BASE ** (-jnp.arange(ROPE_HALF, dtype=F32) / ROPE_HALF)

    def one(pos):
        ang = pos[:, None] * inv_freq
        c, s = jnp.cos(ang), jnp.sin(ang)
        return jnp.concatenate([c, c], axis=-1), jnp.concatenate([-s, s], axis=-1)

    (cr, sr), (cc, sc) = one(row_pos), one(col_pos)
    cos = jnp.concatenate([cr, cc], axis=-1)
    sin = jnp.concatenate([sr, sc], axis=-1)
    cos = jnp.concatenate([jnp.ones((n_ctx, C_QK_DIM), F32), cos], axis=0)
    sin = jnp.concatenate([jnp.zeros((n_ctx, C_QK_DIM), F32), sin], axis=0)
    reps = V7X_LANES // C_QK_DIM
    return jnp.tile(cos, (1, reps)), jnp.tile(sin, (1, reps))


def qkv_project(xs, shift, scale, w_qkv, q_g, k_g, n_ctx):
    b, l, d = xs.shape
    width = w_qkv.shape[1] // 3
    tm = _token_tile(n_ctx, l - n_ctx, TOKEN_TILE)
    ctx_tiles = n_ctx // tm
    cos, sin = _rope_tables(n_ctx, l - n_ctx)
    gains = jnp.stack([jnp.tile(q_g, V7X_LANES // C_QK_DIM), jnp.tile(k_g, V7X_LANES // C_QK_DIM)])
    ones = _group_ones(V7X_LANES, C_QK_DIM)
    out = pl.BlockSpec((1, tm, width), lambda i, j: (i, j, 0))
    return pl.pallas_call(
        functools.partial(_qkv_body, width=width),
        grid=(b, l // tm),
        in_specs=[
            pl.BlockSpec((1, tm, d), lambda i, j: (i, j, 0)),
            _mod_spec(d, ctx_tiles),
            _mod_spec(d, ctx_tiles),
            pl.BlockSpec(w_qkv.shape, lambda i, j: (0, 0)),
            pl.BlockSpec(ones.shape, lambda i, j: (0, 0)),
            pl.BlockSpec(gains.shape, lambda i, j: (0, 0)),
            pl.BlockSpec((tm, V7X_LANES), lambda i, j: (j, 0)),
            pl.BlockSpec((tm, V7X_LANES), lambda i, j: (j, 0)),
        ],
        out_specs=[out, pl.BlockSpec((1, width, tm), lambda i, j: (i, 0, j)), out],
        out_shape=[jax.ShapeDtypeStruct((b, l, width), BF16), jax.ShapeDtypeStruct((b, width, l), BF16),
                   jax.ShapeDtypeStruct((b, l, width), BF16)],
        compiler_params=_cparams(("parallel", "parallel")),
        name="qkv_project",
    )(xs, shift, scale, w_qkv.astype(BF16), ones, gains, cos, sin)


def kernel(x, c, ctx, c_ctx, w_mod, b_mod, even_w_in, even_mu, even_w_out, rwkv_w0, rwkv_w_up, rwkv_a0, rwkv_a_up, rwkv_g_up, rwkv_k_k, rwkv_k_a, rwkv_r_k, rwkv_ln_w, rwkv_ln_b, lru_conv_w, lru_conv_b, lru_wa, lru_ba, lru_wx, lru_bx, lru_lam, attn_w_qkv, attn_q_g, attn_k_g, attn_lam_q1, attn_lam_k1, attn_lam_q2, attn_lam_k2, attn_sub_g, attn_w_o, peer_w_q, peer_keys, peer_u, peer_v):
    bsz, n_lat, d = x.shape
    n_ctx = ctx.shape[1]
    depth = w_mod.shape[0]
    a_proj = even_mu.shape[-1]
    s_lat = jax.nn.silu(c)
    s_ctx = jax.nn.silu(c_ctx)
    xs = jnp.concatenate([ctx, x], axis=1)
    for layer in range(depth):
        i = layer // 2
        m_l = s_lat @ w_mod[layer] + b_mod[layer]
        m_c = s_ctx @ w_mod[layer] + b_mod[layer]
        mods = jnp.stack([jnp.broadcast_to(m_c, m_l.shape), m_l], axis=1).reshape(bsz, 2, N_MOD, 1, d)
        shift1, scale1, gate1, shift2, scale2, gate2 = (mods[:, :, n] for n in range(N_MOD))
        if layer % 2 == 0:
            xs = _even_mixer(xs, shift1, scale1, gate1, n_ctx, even_w_in[i], even_mu[i], even_w_out[i],
                             rwkv_w0[i], rwkv_w_up[i], rwkv_a0[i], rwkv_a_up[i], rwkv_g_up[i], rwkv_k_k[i],
                             rwkv_k_a[i], rwkv_r_k[i], rwkv_ln_w[i], rwkv_ln_b[i], lru_conv_w[i], lru_conv_b[i],
                             lru_wa[i], lru_ba[i], lru_wx[i], lru_bx[i], lru_lam[i])
        else:
            lam_init = 0.8 - 0.6 * math.exp(-0.3 * layer)
            q, k, v = qkv_project(xs, shift1, scale1, attn_w_qkv[i], attn_q_g[i], attn_k_g[i], n_ctx)
            lam = (jnp.exp(jnp.sum(attn_lam_q1[i] * attn_lam_k1[i]))
                   - jnp.exp(jnp.sum(attn_lam_q2[i] * attn_lam_k2[i])) + lam_init)
            o = diff_attention_core(q, k, v, lam.reshape(1), attn_sub_g[i].reshape(1, C_V_DIM), n_ctx, lam_init)
            xs = gated_out_proj(o, xs, gate1, attn_w_o[i], n_ctx)
        seg_ctx = n_ctx if layer < depth - 1 else 0
        if layer == depth - 1:
            xs = xs[:, n_ctx:]
        h, isel, jsel, gsel = peer_route(xs, shift2, scale2, peer_w_q[layer], peer_keys[layer], seg_ctx)
        xs = peer_experts(h, isel, jsel, gsel, peer_u[layer].astype(BF16).T, peer_v[layer].astype(BF16), xs,
                          gate2, seg_ctx)
    return xs
```

```python
import functools
import math

import jax
import jax.numpy as jnp
from jax import lax
from jax.experimental import pallas as pl
from jax.experimental.pallas import tpu as pltpu

F32 = jnp.float32
BF16 = jnp.bfloat16
I32 = jnp.int32

V7X_LANES = 128
V7X_SUBLANES = 8
V7X_VMEM_LIMIT_BYTES = 56 * 1024 * 1024

TOKEN_TILE = 256

GRID_W = 64
N_MOD = 6
A_HEAD_DIM = 64
DECAY_SCALE = math.exp(-0.5)
GN_EPS = 64e-5
DECAY_RANK = 64
ICLR_RANK = 64
GATE_RANK = 128
KK_NORM_EPS = 1e-12
CONV_W = 4
LRU_C = 8.0
C_QK_DIM = 64
C_V_DIM = 128
ROPE_BASE = 10000.0
N_KEYS = 128
PEER_HEADS = 8
PEER_TOPK = 16
RMS_EPS = 1e-6


def _cparams(sem):
    return pltpu.CompilerParams(dimension_semantics=sem, vmem_limit_bytes=V7X_VMEM_LIMIT_BYTES)


def _token_tile(n_ctx, n_lat, want):
    t = want
    while n_ctx % t or n_lat % t:
        t //= 2
    return t


def _mod_spec(d, ctx_tiles):
    return pl.BlockSpec((1, 1, 1, d), lambda b, j, *_: (b, jnp.where(j >= ctx_tiles, 1, 0), 0, 0))


def _rms_modulate(x, shift, scale):
    xn = x * lax.rsqrt(jnp.mean(x * x, axis=-1, keepdims=True) + RMS_EPS)
    return xn * (1.0 + scale) + shift


def _outproj_body(y_ref, x_ref, g_ref, w_ref, o_ref):
    acc = jnp.dot(y_ref[0].astype(BF16), w_ref[...], preferred_element_type=F32)
    o_ref[0] = x_ref[0] + g_ref[0, 0] * acc


def gated_out_proj(y, xs, gate, w, n_ctx):
    b, l, d = xs.shape
    k = y.shape[-1]
    tm = _token_tile(n_ctx, l - n_ctx, TOKEN_TILE)
    ctx_tiles = n_ctx // tm
    return pl.pallas_call(
        _outproj_body,
        grid=(b, l // tm),
        in_specs=[
            pl.BlockSpec((1, tm, k), lambda i, j: (i, j, 0)),
            pl.BlockSpec((1, tm, d), lambda i, j: (i, j, 0)),
            _mod_spec(d, ctx_tiles),
            pl.BlockSpec((k, d), lambda i, j: (0, 0)),
        ],
        out_specs=pl.BlockSpec((1, tm, d), lambda i, j: (i, j, 0)),
        out_shape=jax.ShapeDtypeStruct((b, l, d), F32),
        compiler_params=_cparams(("parallel", "parallel")),
        name="gated_out_proj",
    )(y, xs, gate, w.astype(BF16))


RWKV_VBLK_A = 16
RWKV_VBLK_B = 8
RWKV_KPAR = 4
RWKV_CHAINS = (V7X_SUBLANES // RWKV_KPAR) * V7X_LANES
SCAN_TB = 8


def _reversed_block_map(n_blocks, ctx_blocks):
    def rev(i):
        return jnp.where(i < ctx_blocks, ctx_blocks - 1 - i, n_blocks - 1 - i + ctx_blocks)
    return rev


def _rwkv_scan_body(rf_ref, wf_ref, kf_ref, kkf_ref, akkf_ref, vf_ref,
                    rb_ref, wb_ref, kb_ref, kkb_ref, akkb_ref, vb_ref,
                    of_ref, ob_ref, tf_ref, tb_ref, sa_ref, *, tb, nk4, nv):
    @pl.when(pl.program_id(0) == 0)
    def _():
        tf_ref[...] = jnp.zeros_like(tf_ref)
        tb_ref[...] = jnp.zeros_like(tb_ref)

    def fold(acc):
        acc = acc + pltpu.roll(acc, V7X_SUBLANES // 2, axis=1)
        return acc + pltpu.roll(acc, V7X_SUBLANES // 4, axis=1)

    def one_direction(t, r_ref, w_ref, k_ref, kk_ref, akk_ref, v_ref, o_ref, t_ref):
        for vb in range(nv // RWKV_VBLK_A):
            vs = pl.ds(vb * RWKV_VBLK_A, RWKV_VBLK_A)

            def acc_sa(k4, acc):
                return acc + t_ref[k4, vs] * kk_ref[t, k4][None]

            acc = lax.fori_loop(0, nk4, acc_sa, jnp.zeros((RWKV_VBLK_A, V7X_SUBLANES, V7X_LANES), F32), unroll=True)
            sa_ref[vs] = fold(acc)
        for vb in range(nv // RWKV_VBLK_B):
            vs = pl.ds(vb * RWKV_VBLK_B, RWKV_VBLK_B)
            sa = sa_ref[vs]
            vv = v_ref[t, vs]

            def upd(k4, acc):
                tn = (t_ref[k4, vs] * w_ref[t, k4][None] - akk_ref[t, k4][None] * sa
                      + k_ref[t, k4][None] * vv)
                t_ref[k4, vs] = tn
                return acc + tn * r_ref[t, k4][None]

            acc = lax.fori_loop(0, nk4, upd, jnp.zeros((RWKV_VBLK_B, V7X_SUBLANES, V7X_LANES), F32), unroll=True)
            o_ref[t, vs] = fold(acc)[:, :V7X_SUBLANES // RWKV_KPAR]

    def step(t, carry):
        one_direction(t, rf_ref, wf_ref, kf_ref, kkf_ref, akkf_ref, vf_ref, of_ref, tf_ref)
        one_direction(tb - 1 - t, rb_ref, wb_ref, kb_ref, kkb_ref, akkb_ref, vb_ref, ob_ref, tb_ref)
        return carry

    lax.fori_loop(0, tb, step, 0)


def rwkv_scan(r, w_f, w_b, k_f, k_b, kk, akk_f, akk_b, v, n_ctx):
    l, nk4 = r.shape[0], r.shape[1]
    nv = v.shape[1]
    tb = SCAN_TB
    rev = _reversed_block_map(l // tb, n_ctx // tb)
    kblk = (tb, nk4, V7X_SUBLANES, V7X_LANES)
    vblk = (tb, nv, V7X_SUBLANES, V7X_LANES)
    kf = pl.BlockSpec(kblk, lambda i: (i, 0, 0, 0))
    kb = pl.BlockSpec(kblk, lambda i: (rev(i), 0, 0, 0))
    vf = pl.BlockSpec(vblk, lambda i: (i, 0, 0, 0))
    vb = pl.BlockSpec(vblk, lambda i: (rev(i), 0, 0, 0))
    oblk = (tb, nv, V7X_SUBLANES // RWKV_KPAR, V7X_LANES)
    out = jax.ShapeDtypeStruct((l,) + oblk[1:], F32)
    state = pltpu.VMEM((nk4, nv, V7X_SUBLANES, V7X_LANES), F32)
    return pl.pallas_call(
        functools.partial(_rwkv_scan_body, tb=tb, nk4=nk4, nv=nv),
        grid=(l // tb,),
        in_specs=[kf, kf, kf, kf, kf, vf, kb, kb, kb, kb, kb, vb],
        out_specs=[pl.BlockSpec(oblk, lambda i: (i, 0, 0, 0)), pl.BlockSpec(oblk, lambda i: (rev(i), 0, 0, 0))],
        out_shape=[out, out],
        scratch_shapes=[state, state, pltpu.VMEM((nv, V7X_SUBLANES, V7X_LANES), F32)],
        compiler_params=_cparams(("arbitrary",)),
        name="rwkv_scan",
    )(r, w_f, k_f, kk, akk_f, v, r, w_b, k_b, kk, akk_b, v)


def _diag_scan_body(af_ref, uf_ref, ab_ref, ub_ref, of_ref, ob_ref, hf_ref, hb_ref, *, tb):
    @pl.when(pl.program_id(0) == 0)
    def _():
        hf_ref[...] = jnp.zeros_like(hf_ref)
        hb_ref[...] = jnp.zeros_like(hb_ref)

    def step(t, carry):
        hf, hb = carry
        hf = af_ref[:, t, :] * hf + uf_ref[:, t, :]
        of_ref[:, t, :] = hf
        tr = tb - 1 - t
        hb = ab_ref[:, tr, :] * hb + ub_ref[:, tr, :]
        ob_ref[:, tr, :] = hb
        return hf, hb

    hf, hb = lax.fori_loop(0, tb, step, (hf_ref[...], hb_ref[...]))
    hf_ref[...] = hf
    hb_ref[...] = hb


def diag_scan(a_f, u_f, a_b, u_b, n_ctx):
    b, l, c = a_f.shape
    tb = SCAN_TB
    rev = _reversed_block_map(l // tb, n_ctx // tb)
    fwd = pl.BlockSpec((b, tb, c), lambda i: (0, i, 0))
    bwd = pl.BlockSpec((b, tb, c), lambda i: (0, rev(i), 0))
    out = jax.ShapeDtypeStruct((b, l, c), F32)
    return pl.pallas_call(
        functools.partial(_diag_scan_body, tb=tb),
        grid=(l // tb,),
        in_specs=[fwd, fwd, bwd, bwd],
        out_specs=[fwd, bwd],
        out_shape=[out, out],
        scratch_shapes=[pltpu.VMEM((b, c), F32), pltpu.VMEM((b, c), F32)],
        compiler_params=_cparams(("arbitrary",)),
        name="diag_scan",
    )(a_f, u_f, a_b, u_b)


ATTN_ROW_GROUPS = 2


def _attn_body(lam_ref, q_ref, k_ref, v_ref, sg_ref, o_ref, *, ctx_tiles, n_ctx, scale, out_scale):
    j = pl.program_id(2)
    q = q_ref[0]
    lane = lax.broadcasted_iota(I32, q.shape, 1)
    qs = q * jnp.asarray(scale, BF16)
    q1 = jnp.where(lane < C_QK_DIM, qs, jnp.zeros_like(qs))
    q2 = jnp.where(lane >= C_QK_DIM, qs, jnp.zeros_like(qs))
    lam = lam_ref[0]

    def attend(kt, vb):
        rows = q.shape[0] // ATTN_ROW_GROUPS
        pieces = [qm[r * rows:(r + 1) * rows] for r in range(ATTN_ROW_GROUPS) for qm in (q1, q2)]

        def softmax_times_v(s):
            e = jnp.exp(s - jnp.max(s, axis=-1, keepdims=True))
            pv = jnp.dot(e.astype(BF16), vb, preferred_element_type=F32)
            return pv / jnp.sum(e, axis=-1, keepdims=True)

        ahead = 2
        scores = [jnp.dot(qp, kt, preferred_element_type=F32) for qp in pieces[:ahead]]
        outs = []
        for n in range(len(pieces)):
            if n + ahead < len(pieces):
                scores.append(jnp.dot(pieces[n + ahead], kt, preferred_element_type=F32))
            outs.append(softmax_times_v(scores[n]))
        o = jnp.concatenate([outs[2 * r] - lam * outs[2 * r + 1] for r in range(ATTN_ROW_GROUPS)], axis=0)
        o = o * lax.rsqrt(jnp.mean(o * o, axis=-1, keepdims=True) + RMS_EPS)
        o_ref[0] = o * sg_ref[...] * out_scale

    @pl.when(j < ctx_tiles)
    def _():
        attend(k_ref[0, :, :n_ctx], v_ref[0, :n_ctx])

    @pl.when(j >= ctx_tiles)
    def _():
        attend(k_ref[0], v_ref[0])


def diff_attention_core(q, k_t, v, lam, sub_g, n_ctx, lam_init):
    b, l, width = q.shape
    heads = width // C_V_DIM
    assert n_ctx % V7X_LANES == 0
    tq = _token_tile(n_ctx, l - n_ctx, TOKEN_TILE)
    ctx_tiles = n_ctx // tq
    scale = C_QK_DIM ** -0.5
    assert math.frexp(scale)[0] == 0.5, "the kernel folds the score scale into q, exact only for a power of two"
    body = functools.partial(_attn_body, ctx_tiles=ctx_tiles, n_ctx=n_ctx, scale=scale, out_scale=1.0 - lam_init)
    return pl.pallas_call(
        body,
        grid=(b, heads, l // tq),
        in_specs=[
            pl.BlockSpec(memory_space=pltpu.SMEM),
            pl.BlockSpec((1, tq, C_V_DIM), lambda i, h, j: (i, j, h)),
            pl.BlockSpec((1, C_V_DIM, l), lambda i, h, j: (i, h, 0)),
            pl.BlockSpec((1, l, C_V_DIM), lambda i, h, j: (i, 0, h)),
            pl.BlockSpec((1, C_V_DIM), lambda i, h, j: (0, 0)),
        ],
        out_specs=pl.BlockSpec((1, tq, C_V_DIM), lambda i, h, j: (i, j, h)),
        out_shape=jax.ShapeDtypeStruct((b, l, width), F32),
        compiler_params=_cparams(("parallel", "parallel", "parallel")),
        name="diff_attention",
    )(lam, q, k_t, v, sub_g)


def _peer_candidates():
    return [(a, b) for a in range(PEER_TOPK) for b in range(PEER_TOPK) if (a + 1) * (b + 1) <= PEER_TOPK]


def _segment_rows(mod_ref, tile, tm, n_ctx):
    row = tile * tm + lax.broadcasted_iota(I32, (tm, 1), 0)
    return jnp.where(row < n_ctx, mod_ref[0, 0], mod_ref[0, 1])


def _peer_route_body(x_ref, sh_ref, sc_ref, wq_ref, keys_ref, h_ref, i_ref, j_ref, g_ref,
                     sv_ref, si_ref, cand_ref, best_ref, bi_ref, bj_ref, s_ref, *, tm, n_ctx):
    tile = pl.program_id(1)
    h = _rms_modulate(x_ref[0], _segment_rows(sh_ref, tile, tm, n_ctx),
                      _segment_rows(sc_ref, tile, tm, n_ctx)).astype(BF16)
    h_ref[0] = h
    q = jnp.dot(h, wq_ref[...], preferred_element_type=F32).astype(BF16)
    nt = (((1,), (1,)), ((), ()))
    neg = jnp.float32(-jnp.inf)

    def best_of(nodes):
        while len(nodes) > 1:
            nxt = []
            for (va, ia), (vb, ib) in zip(nodes[0::2], nodes[1::2]):
                left = va >= vb
                nxt.append((jnp.maximum(va, vb), jnp.where(left, ia, ib)))
            nodes = nxt
        return nodes[0]

    for m in range(2):
        for col in range(tm // V7X_LANES):
            toks = slice(col * V7X_LANES, (col + 1) * V7X_LANES)
            per_head = [
                lax.dot_general(keys_ref[m], q[toks, (hd * 2 + m) * N_KEYS:(hd * 2 + m + 1) * N_KEYS], nt,
                                preferred_element_type=F32) for hd in range(PEER_HEADS)]
            s_ref[...] = jnp.swapaxes(jnp.stack(per_head, axis=0), 0, 1)

            def extract(a, carry, m=m, toks=toks):
                vals = [s_ref[n] for n in range(N_KEYS)]
                mx, idx = best_of([(vals[n], jnp.float32(n)) for n in range(N_KEYS)])
                sv_ref[m, a, :, toks] = mx
                si_ref[m, a, :, toks] = idx
                for n in range(N_KEYS):
                    s_ref[n] = jnp.where(idx == jnp.float32(n), neg, vals[n])
                return carry

            lax.fori_loop(0, PEER_TOPK, extract, 0)

    cands = _peer_candidates()
    for c, (a, b) in enumerate(cands):
        cand_ref[c] = sv_ref[0, a] + sv_ref[1, b]
    big = jnp.float32(PEER_TOPK * PEER_TOPK)

    def pick(k, carry):
        vals = [cand_ref[c] for c in range(len(cands))]
        mx = functools.reduce(jnp.maximum, vals)
        pos = functools.reduce(
            jnp.minimum,
            [jnp.where(vals[c] == mx, jnp.float32(a * PEER_TOPK + b), big) for c, (a, b) in enumerate(cands)])
        ii = jnp.zeros_like(pos)
        jj = jnp.zeros_like(pos)
        for c, (a, b) in enumerate(cands):
            hit = pos == jnp.float32(a * PEER_TOPK + b)
            cand_ref[c] = jnp.where(hit, neg, vals[c])
            ii = jnp.where(hit, si_ref[0, a], ii)
            jj = jnp.where(hit, si_ref[1, b], jj)
        best_ref[k] = mx
        bi_ref[k] = ii
        bj_ref[k] = jj
        return carry

    lax.fori_loop(0, PEER_TOPK, pick, 0)
    best = best_ref[...]
    e = jnp.exp(best - best[0][None])
    g = e / jnp.sum(e, axis=0, keepdims=True)
    g_ref[0] = g.reshape(PEER_TOPK * PEER_HEADS, tm).T
    i_ref[0] = bi_ref[...].reshape(PEER_TOPK * PEER_HEADS, tm).T
    j_ref[0] = bj_ref[...].reshape(PEER_TOPK * PEER_HEADS, tm).T


PEER_TILE_TARGET = 512


def _peer_tile(l):
    t = (PEER_TILE_TARGET // V7X_LANES) * V7X_LANES
    while l % t:
        t -= V7X_LANES
    return t


def _mod_pair_spec(d):
    return pl.BlockSpec((1, 2, 1, d), lambda b, *_: (b, 0, 0, 0))


def peer_route(xs, shift, scale, w_q, keys, n_ctx):
    b, l, d = xs.shape
    tm = _peer_tile(l)
    nsel = PEER_TOPK * PEER_HEADS
    ncand = len(_peer_candidates())
    sel_spec = pl.BlockSpec((1, tm, nsel), lambda i, j: (i, j, 0))
    return pl.pallas_call(
        functools.partial(_peer_route_body, tm=tm, n_ctx=n_ctx),
        grid=(b, l // tm),
        in_specs=[
            pl.BlockSpec((1, tm, d), lambda i, j: (i, j, 0)),
            _mod_pair_spec(d),
            _mod_pair_spec(d),
            pl.BlockSpec(w_q.shape, lambda i, j: (0, 0)),
            pl.BlockSpec(keys.shape, lambda i, j: (0, 0, 0)),
        ],
        out_specs=[pl.BlockSpec((1, tm, d), lambda i, j: (i, j, 0)), sel_spec, sel_spec, sel_spec],
        out_shape=[
            jax.ShapeDtypeStruct((b, l, d), BF16),
            jax.ShapeDtypeStruct((b, l, nsel), F32),
            jax.ShapeDtypeStruct((b, l, nsel), F32),
            jax.ShapeDtypeStruct((b, l, nsel), F32),
        ],
        scratch_shapes=[
            pltpu.VMEM((2, PEER_TOPK, PEER_HEADS, tm), F32),
            pltpu.VMEM((2, PEER_TOPK, PEER_HEADS, tm), F32),
            pltpu.VMEM((ncand, PEER_HEADS, tm), F32),
            pltpu.VMEM((PEER_TOPK, PEER_HEADS, tm), F32),
            pltpu.VMEM((PEER_TOPK, PEER_HEADS, tm), F32),
            pltpu.VMEM((PEER_TOPK, PEER_HEADS, tm), F32),
            pltpu.VMEM((N_KEYS, PEER_HEADS, V7X_LANES), F32),
        ],
        compiler_params=_cparams(("parallel", "parallel")),
        name="peer_route",
    )(xs, shift, scale, w_q.astype(BF16), keys.astype(BF16))


PEER_G_CHUNK = 16
PEER_PAIR = 2 * N_KEYS
PEER_EXPERT_BLOCK = 2048


def _peer_expert_body(h_ref, i_ref, j_ref, g_ref, u_ref, v_ref, x_ref, gate_ref, o_ref, gw_ref, acc_ref, wt_ref,
                      *, tm, eb, n_ctx):
    e = pl.program_id(2)

    @pl.when(e == 0)
    def _():
        acc_ref[...] = jnp.zeros_like(acc_ref)
        isel, jsel, gsel = i_ref[0], j_ref[0], g_ref[0]
        kio = lax.broadcasted_iota(I32, (PEER_G_CHUNK, N_KEYS, isel.shape[1]), 1).astype(F32)
        for c in range(tm // PEER_G_CHUNK):
            rows = slice(c * PEER_G_CHUNK, (c + 1) * PEER_G_CHUNK)
            oh_i = jnp.where(isel[rows][:, None, :] == kio, 1.0, 0.0).astype(BF16)
            oh_j = jnp.where(jsel[rows][:, None, :] == kio, gsel[rows][:, None, :], 0.0).astype(BF16)
            g3 = jnp.einsum("pis,pjs->pij", oh_i, oh_j, preferred_element_type=F32)
            gw_ref[:, rows, :] = jnp.swapaxes(g3, 0, 1).astype(BF16)

    h = h_ref[0]
    for ip in range(eb // PEER_PAIR):
        rows = slice(ip * PEER_PAIR, (ip + 1) * PEER_PAIR)
        act = jax.nn.gelu(jnp.dot(h, u_ref[:, rows], preferred_element_type=F32)).astype(BF16)
        i0 = e * (eb // N_KEYS) + ip * 2
        wt_ref[:, ip * PEER_PAIR:ip * PEER_PAIR + N_KEYS] = act[:, :N_KEYS] * gw_ref[i0]
        wt_ref[:, ip * PEER_PAIR + N_KEYS:(ip + 1) * PEER_PAIR] = act[:, N_KEYS:] * gw_ref[i0 + 1]
    acc_ref[...] += jnp.dot(wt_ref[...], v_ref[...], preferred_element_type=F32)

    @pl.when(e == pl.num_programs(2) - 1)
    def _():
        gate = _segment_rows(gate_ref, pl.program_id(1), tm, n_ctx)
        o_ref[0] = x_ref[0] + gate * acc_ref[...]


def peer_experts(h, isel, jsel, gsel, u_t, v_tab, xs, gate, n_ctx):
    b, l, d = xs.shape
    ne = v_tab.shape[0]
    tm = _peer_tile(l)
    eb = PEER_EXPERT_BLOCK if tm < PEER_TILE_TARGET else PEER_EXPERT_BLOCK // 2
    nsel = isel.shape[2]
    tok_spec = pl.BlockSpec((1, tm, d), lambda i, j, e: (i, j, 0))
    sel_spec = pl.BlockSpec((1, tm, nsel), lambda i, j, e: (i, j, 0))
    return pl.pallas_call(
        functools.partial(_peer_expert_body, tm=tm, eb=eb, n_ctx=n_ctx),
        grid=(b, l // tm, ne // eb),
        in_specs=[
            tok_spec, sel_spec, sel_spec, sel_spec,
            pl.BlockSpec((d, eb), lambda i, j, e: (0, e)),
            pl.BlockSpec((eb, d), lambda i, j, e: (e, 0)),
            tok_spec,
            _mod_pair_spec(d),
        ],
        out_specs=tok_spec,
        out_shape=jax.ShapeDtypeStruct((b, l, d), F32),
        scratch_shapes=[
            pltpu.VMEM((N_KEYS, tm, N_KEYS), BF16),
            pltpu.VMEM((tm, d), F32),
            pltpu.VMEM((tm, eb), BF16),
        ],
        compiler_params=_cparams(("parallel", "parallel", "arbitrary")),
        name="peer_experts",
    )(h, isel, jsel, gsel, u_t, v_tab, xs, gate)


def _group_ones(width, group):
    g = jnp.arange(width) // group
    return (g[:, None] == g[None, :]).astype(BF16)


def _group_sum(x, ones_ref):
    hi = x.astype(BF16)
    rest = x - hi.astype(F32)
    mid = rest.astype(BF16)
    lo = (rest - mid.astype(F32)).astype(BF16)
    ones = ones_ref[...]
    return (jnp.dot(hi, ones, preferred_element_type=F32) + jnp.dot(mid, ones, preferred_element_type=F32)
            + jnp.dot(lo, ones, preferred_element_type=F32))


def _pad_rank_rows(w, lo, total):
    return jnp.pad(w, ((lo, total - lo - w.shape[0]), (0, 0)))


HALO_ROWS = V7X_SUBLANES


def _even_prep_body(xm_ref, xp_ref, xn_ref, sh_ref, sc_ref, win_ref, mu_ref, rvec_ref, wup_ref, aup_ref, gup_ref,
                    ones_ref, cw_ref, wa_ref, wx_ref, lvec_ref,
                    r_ref, decf_ref, decb_ref, ktf_ref, ktb_ref, kk_ref, akkf_ref, akkb_ref, v_ref,
                    bonus_ref, ga_ref, af_ref, uf_ref, ab_ref, ub_ref, gb_ref,
                    *, tm, ctx_tiles, n_tiles, width, a_proj):
    j = pl.program_id(1)
    first = jnp.logical_or(j == 0, j == ctx_tiles)
    last = jnp.logical_or(j == ctx_tiles - 1, j == n_tiles - 1)
    shift, scale = sh_ref[0, 0], sc_ref[0, 0]
    z = jnp.concatenate([_rms_modulate(ref[0], shift, scale) for ref in (xp_ref, xm_ref, xn_ref)], axis=0)
    proj = jnp.dot(z.astype(BF16), win_ref[...], preferred_element_type=F32)
    p = proj[HALO_ROWS:HALO_ROWS + tm]
    prev = jnp.where(first, 0.0, proj[:HALO_ROWS])
    nxt = jnp.where(last, 0.0, proj[HALO_ROWS + tm:])
    row = lax.broadcasted_iota(I32, (tm, 1), 0)

    def shifted(lo, hi, off):
        x = p[:, lo:hi]
        if off < 0:
            y = pltpu.roll(x, -off, axis=0)
            for s in range(-off):
                y = jnp.where(row == s, prev[HALO_ROWS + off + s:HALO_ROWS + off + s + 1, lo:hi], y)
        else:
            y = pltpu.roll(x, tm - off, axis=0)
            for s in range(off):
                y = jnp.where(row == tm - off + s, nxt[s:s + 1, lo:hi], y)
        return y

    pa = p[:, :a_proj]
    pa = pa + mu_ref[...] * (0.5 * (shifted(0, a_proj, -1) + shifted(0, a_proj, 1)) - pa)
    r = pa[:, :width]
    k = pa[:, width:2 * width]
    v = pa[:, 2 * width:3 * width]
    c0 = 3 * width
    wd = jnp.tanh(pa[:, c0:c0 + V7X_LANES]).astype(BF16)
    ad = pa[:, c0 + V7X_LANES:c0 + 2 * V7X_LANES].astype(BF16)
    gd = jax.nn.sigmoid(pa[:, c0 + 2 * V7X_LANES:c0 + 3 * V7X_LANES]).astype(BF16)
    w0_f, w0_b, a0_f, a0_b, k_k, k_a, r_k = (rvec_ref[n:n + 1] for n in range(7))

    kk = k * k_k
    kk = kk / jnp.maximum(jnp.sqrt(_group_sum(kk * kk, ones_ref)), KK_NORM_EPS)
    kts = []
    for w0, a0, d, dec_ref, kt_ref, akk_ref in ((w0_f, a0_f, 0, decf_ref, ktf_ref, akkf_ref),
                                                (w0_b, a0_b, 1, decb_ref, ktb_ref, akkb_ref)):
        lora_w = jnp.dot(wd, wup_ref[d], preferred_element_type=F32)
        dec_ref[0] = jnp.exp(-DECAY_SCALE * jax.nn.sigmoid(w0 + lora_w))
        a = jax.nn.sigmoid(a0 + jnp.dot(ad, aup_ref[d], preferred_element_type=F32))
        kt = k * (1.0 + (a - 1.0) * k_a)
        kt_ref[0] = kt
        akk_ref[0] = a * kk
        kts.append(kt)
    r_ref[0] = r
    kk_ref[0] = kk
    v_ref[0] = v
    bonus_ref[0] = _group_sum(r * (0.5 * (kts[0] + kts[1])) * r_k, ones_ref) * v
    ga_ref[0] = jnp.dot(gd, gup_ref[...], preferred_element_type=F32)

    b0 = a_proj
    conv_b, ba_f, ba_b, bx_f, bx_b, sp_f, sp_b = (lvec_ref[n:n + 1] for n in range(7))
    xb = (cw_ref[0:1] * shifted(b0, b0 + width, -2) + cw_ref[1:2] * shifted(b0, b0 + width, -1)
          + cw_ref[2:3] * p[:, b0:b0 + width] + cw_ref[3:4] * shifted(b0, b0 + width, 1) + conv_b)
    gb_ref[0] = jax.nn.gelu(p[:, b0 + width:b0 + 2 * width])
    xbb = xb.astype(BF16)
    for d, ba, bx, sp, a_ref, u_ref in ((0, ba_f, bx_f, sp_f, af_ref, uf_ref), (1, ba_b, bx_b, sp_b, ab_ref, ub_ref)):
        rg = jax.nn.sigmoid(jnp.dot(xbb, wa_ref[d], preferred_element_type=F32) + ba)
        ig = jax.nn.sigmoid(jnp.dot(xbb, wx_ref[d], preferred_element_type=F32) + bx)
        log_a = -LRU_C * rg * sp
        a_ref[0] = jnp.exp(log_a)
        th = jnp.tanh(log_a)
        u_ref[0] = jnp.sqrt(-2.0 * th / (1.0 - th)) * ig * xb


def even_prep(xs, shift, scale, w_in, n_ctx, mu, w0, w_up, a0, a_up, g_up, k_k, k_a, r_k, conv_w, conv_b, wa, ba, wx,
              bx, lam):
    b, l, d = xs.shape
    n_proj = w_in.shape[1]
    width = w0.shape[-1]
    a_proj = mu.shape[-1]
    assert CONV_W == 4 and 2 * DECAY_RANK == V7X_LANES and 2 * ICLR_RANK == V7X_LANES and GATE_RANK == V7X_LANES
    assert a_proj == 3 * width + 3 * V7X_LANES and n_proj == a_proj + 2 * width
    tm = _token_tile(n_ctx, l - n_ctx, TOKEN_TILE)
    n_tiles, ctx_tiles = l // tm, n_ctx // tm
    halo_per_tile = tm // HALO_ROWS
    last_halo = l // HALO_ROWS - 1
    rvec = jnp.concatenate([w0, a0, k_k[None], k_a[None], r_k.reshape(1, width), jnp.zeros((1, width), F32)], axis=0)
    lvec = jnp.concatenate([conv_b[None], ba, bx, jax.nn.softplus(-lam), jnp.zeros((1, width), F32)], axis=0)
    wup = jnp.stack([_pad_rank_rows(w_up[0], 0, V7X_LANES), _pad_rank_rows(w_up[1], DECAY_RANK, V7X_LANES)])
    aup = jnp.stack([_pad_rank_rows(a_up[0], 0, V7X_LANES), _pad_rank_rows(a_up[1], ICLR_RANK, V7X_LANES)])

    def block_diag(w):
        return jnp.stack([jax.scipy.linalg.block_diag(*w[d]) for d in range(2)]).astype(BF16)

    tok = pl.BlockSpec((1, tm, width), lambda i, j: (i, j, 0))

    def full(a):
        return pl.BlockSpec(a.shape, lambda i, j, nd=a.ndim: (0,) * nd)

    consts = [w_in.astype(BF16), mu[None], rvec, wup.astype(BF16), aup.astype(BF16), g_up.astype(BF16),
              _group_ones(width, A_HEAD_DIM), conv_w, block_diag(wa), block_diag(wx), lvec]
    return pl.pallas_call(
        functools.partial(_even_prep_body, tm=tm, ctx_tiles=ctx_tiles, n_tiles=n_tiles, width=width, a_proj=a_proj),
        grid=(b, n_tiles),
        in_specs=[
            pl.BlockSpec((1, tm, d), lambda i, j: (i, j, 0)),
            pl.BlockSpec((1, HALO_ROWS, d), lambda i, j: (i, jnp.maximum(j * halo_per_tile - 1, 0), 0)),
            pl.BlockSpec((1, HALO_ROWS, d), lambda i, j: (i, jnp.minimum((j + 1) * halo_per_tile, last_halo), 0)),
            _mod_spec(d, ctx_tiles),
            _mod_spec(d, ctx_tiles),
        ] + [full(a) for a in consts],
        out_specs=[tok] * 16,
        out_shape=[jax.ShapeDtypeStruct((b, l, width), F32)] * 16,
        compiler_params=_cparams(("parallel", "parallel")),
        name="even_prep",
    )(xs, xs, xs, shift, scale, *consts)


def _even_post_body(of_ref, ob_ref, bonus_ref, ga_ref, hf_ref, hb_ref, gb_ref, ln_ref, ones_ref,
                    x_ref, g_ref, w_ref, o_ref, *, width):
    o = of_ref[0] + ob_ref[0]
    inv_n = 1.0 / A_HEAD_DIM
    cen = o - _group_sum(o, ones_ref) * inv_n
    var = _group_sum(cen * cen, ones_ref) * inv_n
    on = cen * lax.rsqrt(var + GN_EPS) * ln_ref[0:1] + ln_ref[1:2]
    ya = ((on + bonus_ref[0]) * ga_ref[0]).astype(BF16)
    yb = ((hf_ref[0] + hb_ref[0]) * gb_ref[0]).astype(BF16)
    acc = (jnp.dot(ya, w_ref[:width], preferred_element_type=F32)
           + jnp.dot(yb, w_ref[width:], preferred_element_type=F32))
    o_ref[0] = x_ref[0] + g_ref[0, 0] * acc


def even_post(o_f, o_b, bonus, gate_a, h_f, h_b, gate_b, ln_w, ln_b, xs, gate, w_out, n_ctx):
    b, l, d = xs.shape
    width = o_f.shape[-1]
    tm = _token_tile(n_ctx, l - n_ctx, TOKEN_TILE)
    ctx_tiles = n_ctx // tm
    tok = pl.BlockSpec((1, tm, width), lambda i, j: (i, j, 0))
    ln = jnp.stack([ln_w, ln_b])
    ones = _group_ones(width, A_HEAD_DIM)
    return pl.pallas_call(
        functools.partial(_even_post_body, width=width),
        grid=(b, l // tm),
        in_specs=[tok] * 7 + [
            pl.BlockSpec(ln.shape, lambda i, j: (0, 0)),
            pl.BlockSpec(ones.shape, lambda i, j: (0, 0)),
            pl.BlockSpec((1, tm, d), lambda i, j: (i, j, 0)),
            _mod_spec(d, ctx_tiles),
            pl.BlockSpec(w_out.shape, lambda i, j: (0, 0)),
        ],
        out_specs=pl.BlockSpec((1, tm, d), lambda i, j: (i, j, 0)),
        out_shape=jax.ShapeDtypeStruct((b, l, d), F32),
        compiler_params=_cparams(("parallel", "parallel")),
        name="even_post",
    )(o_f, o_b, bonus, gate_a, h_f, h_b, gate_b, ln, ones, xs, gate, w_out.astype(BF16))


def _rwkv_chain_layout(z, heads):
    b, l, width = z.shape
    n = width // heads
    z = z.reshape(b, l, heads, n).transpose(1, 3, 0, 2).reshape(l, n, b * heads)
    z = jnp.pad(z, ((0, 0), (0, 0), (0, RWKV_CHAINS - b * heads)))
    return z.reshape(l, n, V7X_SUBLANES // RWKV_KPAR, V7X_LANES)


def _rwkv_keyed(z, heads):
    z = _rwkv_chain_layout(z, heads)
    l, n = z.shape[:2]
    return z.reshape(l, n // RWKV_KPAR, V7X_SUBLANES, V7X_LANES)


def _rwkv_valued(z, heads):
    z = _rwkv_chain_layout(z, heads)
    return jnp.concatenate([z] * RWKV_KPAR, axis=2)


def _rwkv_unchain(o, b, heads):
    l, n = o.shape[:2]
    o = o.reshape(l, n, RWKV_CHAINS)[:, :, :b * heads]
    return o.reshape(l, n, b, heads).transpose(2, 0, 3, 1).reshape(b, l, heads * n)


def _even_mixer(xs, shift, scale, gate, n_ctx, w_in, mu, w_out, w0, w_up, a0, a_up, g_up, k_k, k_a, r_k, ln_w, ln_b,
                conv_w, conv_b, wa, ba, wx, bx, lam):
    b = xs.shape[0]
    heads = w0.shape[-1] // A_HEAD_DIM
    assert b * heads <= RWKV_CHAINS
    (r, dec_f, dec_b, kt_f, kt_b, kk, akk_f, akk_b, v, bonus, gate_a, a_f, u_f, a_b, u_b, gate_b) = even_prep(
        xs, shift, scale, w_in, n_ctx, mu, w0, w_up, a0, a_up, g_up, k_k, k_a, r_k, conv_w, conv_b, wa, ba, wx, bx, lam)
    o_f, o_b = rwkv_scan(
        _rwkv_keyed(r, heads), _rwkv_keyed(dec_f, heads), _rwkv_keyed(dec_b, heads),
        _rwkv_keyed(kt_f, heads), _rwkv_keyed(kt_b, heads), _rwkv_keyed(kk, heads),
        _rwkv_keyed(akk_f, heads), _rwkv_keyed(akk_b, heads), _rwkv_valued(v, heads), n_ctx)
    h_f, h_b = diag_scan(a_f, u_f, a_b, u_b, n_ctx)
    return even_post(_rwkv_unchain(o_f, b, heads), _rwkv_unchain(o_b, b, heads), bonus, gate_a, h_f, h_b, gate_b,
                     ln_w, ln_b, xs, gate, w_out, n_ctx)


ROPE_HALF = C_QK_DIM // 4


def _qkv_body(x_ref, sh_ref, sc_ref, w_ref, ones_ref, g_ref, cos_ref, sin_ref, q_ref, k_ref, v_ref, *, width):
    z = _rms_modulate(x_ref[0], sh_ref[0, 0], sc_ref[0, 0])
    qkv = jnp.dot(z.astype(BF16), w_ref[...], preferred_element_type=F32)
    cos = cos_ref[...]
    sin = sin_ref[...]
    lane = lax.broadcasted_iota(I32, cos.shape, 1)
    first_half = (lane % (2 * ROPE_HALF)) < ROPE_HALF
    for part, out_ref in ((0, q_ref), (1, k_ref)):
        gain = g_ref[part:part + 1]
        for c in range(width // V7X_LANES):
            lo = part * width + c * V7X_LANES
            t = qkv[:, lo:lo + V7X_LANES]
            ms = _group_sum(t * t, ones_ref) * (1.0 / C_QK_DIM)
            t = t * lax.rsqrt(ms + RMS_EPS) * gain
            partner = jnp.where(first_half, pltpu.roll(t, V7X_LANES - ROPE_HALF, axis=1),
                                pltpu.roll(t, ROPE_HALF, axis=1))
            t = t * cos + partner * sin
            if part == 0:
                out_ref[0, :, c * V7X_LANES:(c + 1) * V7X_LANES] = t.astype(BF16)
            else:
                out_ref[0, c * V7X_LANES:(c + 1) * V7X_LANES, :] = t.T.astype(BF16)
    v_ref[0] = qkv[:, 2 * width:].astype(BF16)


def _rope_tables(n_ctx, n_lat):
    n_rows = n_lat // GRID_W
    row_pos = jnp.repeat(jnp.arange(n_rows), GRID_W).astype(F32)
    col_pos = jnp.tile(jnp.arange(GRID_W), n_rows).astype(F32)
    inv_freq = ROPE_BASE ** (-jnp.arange(ROPE_HALF, dtype=F32) / ROPE_HALF)

    def one(pos):
        ang = pos[:, None] * inv_freq
        c, s = jnp.cos(ang), jnp.sin(ang)
        return jnp.concatenate([c, c], axis=-1), jnp.concatenate([-s, s], axis=-1)

    (cr, sr), (cc, sc) = one(row_pos), one(col_pos)
    cos = jnp.concatenate([cr, cc], axis=-1)
    sin = jnp.concatenate([sr, sc], axis=-1)
    cos = jnp.concatenate([jnp.ones((n_ctx, C_QK_DIM), F32), cos], axis=0)
    sin = jnp.concatenate([jnp.zeros((n_ctx, C_QK_DIM), F32), sin], axis=0)
    reps = V7X_LANES // C_QK_DIM
    return jnp.tile(cos, (1, reps)), jnp.tile(sin, (1, reps))


def qkv_project(xs, shift, scale, w_qkv, q_g, k_g, n_ctx):
    b, l, d = xs.shape
    width = w_qkv.shape[1] // 3
    tm = _token_tile(n_ctx, l - n_ctx, TOKEN_TILE)
    ctx_tiles = n_ctx // tm
    cos, sin = _rope_tables(n_ctx, l - n_ctx)
    gains = jnp.stack([jnp.tile(q_g, V7X_LANES // C_QK_DIM), jnp.tile(k_g, V7X_LANES // C_QK_DIM)])
    ones = _group_ones(V7X_LANES, C_QK_DIM)
    out = pl.BlockSpec((1, tm, width), lambda i, j: (i, j, 0))
    return pl.pallas_call(
        functools.partial(_qkv_body, width=width),
        grid=(b, l // tm),
        in_specs=[
            pl.BlockSpec((1, tm, d), lambda i, j: (i, j, 0)),
            _mod_spec(d, ctx_tiles),
            _mod_spec(d, ctx_tiles),
            pl.BlockSpec(w_qkv.shape, lambda i, j: (0, 0)),
            pl.BlockSpec(ones.shape, lambda i, j: (0, 0)),
            pl.BlockSpec(gains.shape, lambda i, j: (0, 0)),
            pl.BlockSpec((tm, V7X_LANES), lambda i, j: (j, 0)),
            pl.BlockSpec((tm, V7X_LANES), lambda i, j: (j, 0)),
        ],
        out_specs=[out, pl.BlockSpec((1, width, tm), lambda i, j: (i, 0, j)), out],
        out_shape=[jax.ShapeDtypeStruct((b, l, width), BF16), jax.ShapeDtypeStruct((b, width, l), BF16),
                   jax.ShapeDtypeStruct((b, l, width), BF16)],
        compiler_params=_cparams(("parallel", "parallel")),
        name="qkv_project",
    )(xs, shift, scale, w_qkv.astype(BF16), ones, gains, cos, sin)


def kernel(x, c, ctx, c_ctx, w_mod, b_mod, even_w_in, even_mu, even_w_out, rwkv_w0, rwkv_w_up, rwkv_a0, rwkv_a_up, rwkv_g_up, rwkv_k_k, rwkv_k_a, rwkv_r_k, rwkv_ln_w, rwkv_ln_b, lru_conv_w, lru_conv_b, lru_wa, lru_ba, lru_wx, lru_bx, lru_lam, attn_w_qkv, attn_q_g, attn_k_g, attn_lam_q1, attn_lam_k1, attn_lam_q2, attn_lam_k2, attn_sub_g, attn_w_o, peer_w_q, peer_keys, peer_u, peer_v):
    bsz, n_lat, d = x.shape
    n_ctx = ctx.shape[1]
    depth = w_mod.shape[0]
    a_proj = even_mu.shape[-1]
    s_lat = jax.nn.silu(c)
    s_ctx = jax.nn.silu(c_ctx)
    xs = jnp.concatenate([ctx, x], axis=1)
    for layer in range(depth):
        i = layer // 2
        m_l = s_lat @ w_mod[layer] + b_mod[layer]
        m_c = s_ctx @ w_mod[layer] + b_mod[layer]
        mods = jnp.stack([jnp.broadcast_to(m_c, m_l.shape), m_l], axis=1).reshape(bsz, 2, N_MOD, 1, d)
        shift1, scale1, gate1, shift2, scale2, gate2 = (mods[:, :, n] for n in range(N_MOD))
        if layer % 2 == 0:
            xs = _even_mixer(xs, shift1, scale1, gate1, n_ctx, even_w_in[i], even_mu[i], even_w_out[i],
                             rwkv_w0[i], rwkv_w_up[i], rwkv_a0[i], rwkv_a_up[i], rwkv_g_up[i], rwkv_k_k[i],
                             rwkv_k_a[i], rwkv_r_k[i], rwkv_ln_w[i], rwkv_ln_b[i], lru_conv_w[i], lru_conv_b[i],
                             lru_wa[i], lru_ba[i], lru_wx[i], lru_bx[i], lru_lam[i])
        else:
            lam_init = 0.8 - 0.6 * math.exp(-0.3 * layer)
            q, k, v = qkv_project(xs, shift1, scale1, attn_w_qkv[i], attn_q_g[i], attn_k_g[i], n_ctx)
            lam = (jnp.exp(jnp.sum(attn_lam_q1[i] * attn_lam_k1[i]))
                   - jnp.exp(jnp.sum(attn_lam_q2[i] * attn_lam_k2[i])) + lam_init)
            o = diff_attention_core(q, k, v, lam.reshape(1), attn_sub_g[i].reshape(1, C_V_DIM), n_ctx, lam_init)
            xs = gated_out_proj(o, xs, gate1, attn_w_o[i], n_ctx)
        seg_ctx = n_ctx if layer < depth - 1 else 0
        if layer == depth - 1:
            xs = xs[:, n_ctx:]
        h, isel, jsel, gsel = peer_route(xs, shift2, scale2, peer_w_q[layer], peer_keys[layer], seg_ctx)
        xs = peer_experts(h, isel, jsel, gsel, peer_u[layer].astype(BF16).T, peer_v[layer].astype(BF16), xs,
                          gate2, seg_ctx)
    return xs
```

```python
import functools
import math

import jax
import jax.numpy as jnp
from jax import lax
from jax.experimental import pallas as pl
from jax.experimental.pallas import tpu as pltpu

F32 = jnp.float32
BF16 = jnp.bfloat16
I32 = jnp.int32

V7X_LANES = 128
V7X_SUBLANES = 8
V7X_VMEM_LIMIT_BYTES = 56 * 1024 * 1024

TOKEN_TILE = 256

GRID_W = 64
N_MOD = 6
A_HEAD_DIM = 64
DECAY_SCALE = math.exp(-0.5)
GN_EPS = 64e-5
DECAY_RANK = 64
ICLR_RANK = 64
GATE_RANK = 128
KK_NORM_EPS = 1e-12
CONV_W = 4
LRU_C = 8.0
C_QK_DIM = 64
C_V_DIM = 128
ROPE_BASE = 10000.0
N_KEYS = 128
PEER_HEADS = 8
PEER_TOPK = 16
RMS_EPS = 1e-6


def _cparams(sem):
    return pltpu.CompilerParams(dimension_semantics=sem, vmem_limit_bytes=V7X_VMEM_LIMIT_BYTES)


def _token_tile(n_ctx, n_lat, want):
    t = want
    while n_ctx % t or n_lat % t:
        t //= 2
    return t


def _mod_spec(d, ctx_tiles):
    return pl.BlockSpec((1, 1, 1, d), lambda b, j, *_: (b, jnp.where(j >= ctx_tiles, 1, 0), 0, 0))


def _rms_modulate(x, shift, scale):
    xn = x * lax.rsqrt(jnp.mean(x * x, axis=-1, keepdims=True) + RMS_EPS)
    return xn * (1.0 + scale) + shift


def _outproj_body(y_ref, x_ref, g_ref, w_ref, o_ref):
    acc = jnp.dot(y_ref[0].astype(BF16), w_ref[...], preferred_element_type=F32)
    o_ref[0] = x_ref[0] + g_ref[0, 0] * acc


def gated_out_proj(y, xs, gate, w, n_ctx):
    b, l, d = xs.shape
    k = y.shape[-1]
    tm = _token_tile(n_ctx, l - n_ctx, TOKEN_TILE)
    ctx_tiles = n_ctx // tm
    return pl.pallas_call(
        _outproj_body,
        grid=(b, l // tm),
        in_specs=[
            pl.BlockSpec((1, tm, k), lambda i, j: (i, j, 0)),
            pl.BlockSpec((1, tm, d), lambda i, j: (i, j, 0)),
            _mod_spec(d, ctx_tiles),
            pl.BlockSpec((k, d), lambda i, j: (0, 0)),
        ],
        out_specs=pl.BlockSpec((1, tm, d), lambda i, j: (i, j, 0)),
        out_shape=jax.ShapeDtypeStruct((b, l, d), F32),
        compiler_params=_cparams(("parallel", "parallel")),
        name="gated_out_proj",
    )(y, xs, gate, w.astype(BF16))


RWKV_VBLK_A = 16
RWKV_VBLK_B = 8
RWKV_KPAR = 4
RWKV_CHAINS = (V7X_SUBLANES // RWKV_KPAR) * V7X_LANES
SCAN_TB = 8


def _reversed_block_map(n_blocks, ctx_blocks):
    def rev(i):
        return jnp.where(i < ctx_blocks, ctx_blocks - 1 - i, n_blocks - 1 - i + ctx_blocks)
    return rev


def _rwkv_scan_body(rf_ref, wf_ref, kf_ref, kkf_ref, akkf_ref, vf_ref,
                    rb_ref, wb_ref, kb_ref, kkb_ref, akkb_ref, vb_ref,
                    of_ref, ob_ref, tf_ref, tb_ref, sa_ref, *, tb, nk4, nv):
    @pl.when(pl.program_id(0) == 0)
    def _():
        tf_ref[...] = jnp.zeros_like(tf_ref)
        tb_ref[...] = jnp.zeros_like(tb_ref)

    def fold(acc):
        acc = acc + pltpu.roll(acc, V7X_SUBLANES // 2, axis=1)
        return acc + pltpu.roll(acc, V7X_SUBLANES // 4, axis=1)

    def one_direction(t, r_ref, w_ref, k_ref, kk_ref, akk_ref, v_ref, o_ref, t_ref):
        for vb in range(nv // RWKV_VBLK_A):
            vs = pl.ds(vb * RWKV_VBLK_A, RWKV_VBLK_A)

            def acc_sa(k4, acc):
                return acc + t_ref[k4, vs] * kk_ref[t, k4][None]

            acc = lax.fori_loop(0, nk4, acc_sa, jnp.zeros((RWKV_VBLK_A, V7X_SUBLANES, V7X_LANES), F32), unroll=True)
            sa_ref[vs] = fold(acc)
        for vb in range(nv // RWKV_VBLK_B):
            vs = pl.ds(vb * RWKV_VBLK_B, RWKV_VBLK_B)
            sa = sa_ref[vs]
            vv = v_ref[t, vs]

            def upd(k4, acc):
                tn = (t_ref[k4, vs] * w_ref[t, k4][None] - akk_ref[t, k4][None] * sa
                      + k_ref[t, k4][None] * vv)
                t_ref[k4, vs] = tn
                return acc + tn * r_ref[t, k4][None]

            acc = lax.fori_loop(0, nk4, upd, jnp.zeros((RWKV_VBLK_B, V7X_SUBLANES, V7X_LANES), F32), unroll=True)
            o_ref[t, vs] = fold(acc)[:, :V7X_SUBLANES // RWKV_KPAR]

    def step(t, carry):
        one_direction(t, rf_ref, wf_ref, kf_ref, kkf_ref, akkf_ref, vf_ref, of_ref, tf_ref)
        one_direction(tb - 1 - t, rb_ref, wb_ref, kb_ref, kkb_ref, akkb_ref, vb_ref, ob_ref, tb_ref)
        return carry

    lax.fori_loop(0, tb, step, 0)


def rwkv_scan(r, w_f, w_b, k_f, k_b, kk, akk_f, akk_b, v, n_ctx):
    l, nk4 = r.shape[0], r.shape[1]
    nv = v.shape[1]
    tb = SCAN_TB
    rev = _reversed_block_map(l // tb, n_ctx // tb)
    kblk = (tb, nk4, V7X_SUBLANES, V7X_LANES)
    vblk = (tb, nv, V7X_SUBLANES, V7X_LANES)
    kf = pl.BlockSpec(kblk, lambda i: (i, 0, 0, 0))
    kb = pl.BlockSpec(kblk, lambda i: (rev(i), 0, 0, 0))
    vf = pl.BlockSpec(vblk, lambda i: (i, 0, 0, 0))
    vb = pl.BlockSpec(vblk, lambda i: (rev(i), 0, 0, 0))
    oblk = (tb, nv, V7X_SUBLANES // RWKV_KPAR, V7X_LANES)
    out = jax.ShapeDtypeStruct((l,) + oblk[1:], F32)
    state = pltpu.VMEM((nk4, nv, V7X_SUBLANES, V7X_LANES), F32)
    return pl.pallas_call(
        functools.partial(_rwkv_scan_body, tb=tb, nk4=nk4, nv=nv),
        grid=(l // tb,),
        in_specs=[kf, kf, kf, kf, kf, vf, kb, kb, kb, kb, kb, vb],
        out_specs=[pl.BlockSpec(oblk, lambda i: (i, 0, 0, 0)), pl.BlockSpec(oblk, lambda i: (rev(i), 0, 0, 0))],
        out_shape=[out, out],
        scratch_shapes=[state, state, pltpu.VMEM((nv, V7X_SUBLANES, V7X_LANES), F32)],
        compiler_params=_cparams(("arbitrary",)),
        name="rwkv_scan",
    )(r, w_f, k_f, kk, akk_f, v, r, w_b, k_b, kk, akk_b, v)


def _diag_scan_body(af_ref, uf_ref, ab_ref, ub_ref, of_ref, ob_ref, hf_ref, hb_ref, *, tb):
    @pl.when(pl.program_id(0) == 0)
    def _():
        hf_ref[...] = jnp.zeros_like(hf_ref)
        hb_ref[...] = jnp.zeros_like(hb_ref)

    def step(t, carry):
        hf, hb = carry
        hf = af_ref[:, t, :] * hf + uf_ref[:, t, :]
        of_ref[:, t, :] = hf
        tr = tb - 1 - t
        hb = ab_ref[:, tr, :] * hb + ub_ref[:, tr, :]
        ob_ref[:, tr, :] = hb
        return hf, hb

    hf, hb = lax.fori_loop(0, tb, step, (hf_ref[...], hb_ref[...]))
    hf_ref[...] = hf
    hb_ref[...] = hb


def diag_scan(a_f, u_f, a_b, u_b, n_ctx):
    b, l, c = a_f.shape
    tb = SCAN_TB
    rev = _reversed_block_map(l // tb, n_ctx // tb)
    fwd = pl.BlockSpec((b, tb, c), lambda i: (0, i, 0))
    bwd = pl.BlockSpec((b, tb, c), lambda i: (0, rev(i), 0))
    out = jax.ShapeDtypeStruct((b, l, c), F32)
    return pl.pallas_call(
        functools.partial(_diag_scan_body, tb=tb),
        grid=(l // tb,),
        in_specs=[fwd, fwd, bwd, bwd],
        out_specs=[fwd, bwd],
        out_shape=[out, out],
        scratch_shapes=[pltpu.VMEM((b, c), F32), pltpu.VMEM((b, c), F32)],
        compiler_params=_cparams(("arbitrary",)),
        name="diag_scan",
    )(a_f, u_f, a_b, u_b)


ATTN_ROW_GROUPS = 2


def _attn_body(lam_ref, q_ref, k_ref, v_ref, sg_ref, o_ref, *, ctx_tiles, n_ctx, scale, out_scale):
    j = pl.program_id(2)
    q = q_ref[0]
    lane = lax.broadcasted_iota(I32, q.shape, 1)
    qs = q * jnp.asarray(scale, BF16)
    q1 = jnp.where(lane < C_QK_DIM, qs, jnp.zeros_like(qs))
    q2 = jnp.where(lane >= C_QK_DIM, qs, jnp.zeros_like(qs))
    lam = lam_ref[0]

    def attend(kt, vb):
        rows = q.shape[0] // ATTN_ROW_GROUPS
        pieces = [qm[r * rows:(r + 1) * rows] for r in range(ATTN_ROW_GROUPS) for qm in (q1, q2)]

        def softmax_times_v(s):
            e = jnp.exp(s - jnp.max(s, axis=-1, keepdims=True))
            pv = jnp.dot(e.astype(BF16), vb, preferred_element_type=F32)
            return pv / jnp.sum(e, axis=-1, keepdims=True)

        ahead = 2
        scores = [jnp.dot(qp, kt, preferred_element_type=F32) for qp in pieces[:ahead]]
        outs = []
        for n in range(len(pieces)):
            if n + ahead < len(pieces):
                scores.append(jnp.dot(pieces[n + ahead], kt, preferred_element_type=F32))
            outs.append(softmax_times_v(scores[n]))
        o = jnp.concatenate([outs[2 * r] - lam * outs[2 * r + 1] for r in range(ATTN_ROW_GROUPS)], axis=0)
        o = o * lax.rsqrt(jnp.mean(o * o, axis=-1, keepdims=True) + RMS_EPS)
        o_ref[0] = o * sg_ref[...] * out_scale

    @pl.when(j < ctx_tiles)
    def _():
        attend(k_ref[0, :, :n_ctx], v_ref[0, :n_ctx])

    @pl.when(j >= ctx_tiles)
    def _():
        attend(k_ref[0], v_ref[0])


def diff_attention_core(q, k_t, v, lam, sub_g, n_ctx, lam_init):
    b, l, width = q.shape
    heads = width // C_V_DIM
    assert n_ctx % V7X_LANES == 0
    tq = _token_tile(n_ctx, l - n_ctx, TOKEN_TILE)
    ctx_tiles = n_ctx // tq
    scale = C_QK_DIM ** -0.5
    assert math.frexp(scale)[0] == 0.5, "the kernel folds the score scale into q, exact only for a power of two"
    body = functools.partial(_attn_body, ctx_tiles=ctx_tiles, n_ctx=n_ctx, scale=scale, out_scale=1.0 - lam_init)
    return pl.pallas_call(
        body,
        grid=(b, heads, l // tq),
        in_specs=[
            pl.BlockSpec(memory_space=pltpu.SMEM),
            pl.BlockSpec((1, tq, C_V_DIM), lambda i, h, j: (i, j, h)),
            pl.BlockSpec((1, C_V_DIM, l), lambda i, h, j: (i, h, 0)),
            pl.BlockSpec((1, l, C_V_DIM), lambda i, h, j: (i, 0, h)),
            pl.BlockSpec((1, C_V_DIM), lambda i, h, j: (0, 0)),
        ],
        out_specs=pl.BlockSpec((1, tq, C_V_DIM), lambda i, h, j: (i, j, h)),
        out_shape=jax.ShapeDtypeStruct((b, l, width), F32),
        compiler_params=_cparams(("parallel", "parallel", "parallel")),
        name="diff_attention",
    )(lam, q, k_t, v, sub_g)


def _peer_candidates():
    return [(a, b) for a in range(PEER_TOPK) for b in range(PEER_TOPK) if (a + 1) * (b + 1) <= PEER_TOPK]


def _segment_rows(mod_ref, tile, tm, n_ctx):
    row = tile * tm + lax.broadcasted_iota(I32, (tm, 1), 0)
    return jnp.where(row < n_ctx, mod_ref[0, 0], mod_ref[0, 1])


def _peer_route_body(x_ref, sh_ref, sc_ref, wq_ref, keys_ref, h_ref, i_ref, j_ref, g_ref,
                     sv_ref, si_ref, cand_ref, best_ref, bi_ref, bj_ref, s_ref, *, tm, n_ctx):
    tile = pl.program_id(1)
    h = _rms_modulate(x_ref[0], _segment_rows(sh_ref, tile, tm, n_ctx),
                      _segment_rows(sc_ref, tile, tm, n_ctx)).astype(BF16)
    h_ref[0] = h
    q = jnp.dot(h, wq_ref[...], preferred_element_type=F32).astype(BF16)
    nt = (((1,), (1,)), ((), ()))
    neg = jnp.float32(-jnp.inf)

    def best_of(nodes):
        while len(nodes) > 1:
            nxt = []
            for (va, ia), (vb, ib) in zip(nodes[0::2], nodes[1::2]):
                left = va >= vb
                nxt.append((jnp.maximum(va, vb), jnp.where(left, ia, ib)))
            nodes = nxt
        return nodes[0]

    for m in range(2):
        for col in range(tm // V7X_LANES):
            toks = slice(col * V7X_LANES, (col + 1) * V7X_LANES)
            per_head = [
                lax.dot_general(keys_ref[m], q[toks, (hd * 2 + m) * N_KEYS:(hd * 2 + m + 1) * N_KEYS], nt,
                                preferred_element_type=F32) for hd in range(PEER_HEADS)]
            s_ref[...] = jnp.swapaxes(jnp.stack(per_head, axis=0), 0, 1)

            def extract(a, carry, m=m, toks=toks):
                vals = [s_ref[n] for n in range(N_KEYS)]
                mx, idx = best_of([(vals[n], jnp.float32(n)) for n in range(N_KEYS)])
                sv_ref[m, a, :, toks] = mx
                si_ref[m, a, :, toks] = idx
                for n in range(N_KEYS):
                    s_ref[n] = jnp.where(idx == jnp.float32(n), neg, vals[n])
                return carry

            lax.fori_loop(0, PEER_TOPK, extract, 0)

    cands = _peer_candidates()
    for c, (a, b) in enumerate(cands):
        cand_ref[c] = sv_ref[0, a] + sv_ref[1, b]
    big = jnp.float32(PEER_TOPK * PEER_TOPK)

    def pick(k, carry):
        vals = [cand_ref[c] for c in range(len(cands))]
        mx = functools.reduce(jnp.maximum, vals)
        pos = functools.reduce(
            jnp.minimum,
            [jnp.where(vals[c] == mx, jnp.float32(a * PEER_TOPK + b), big) for c, (a, b) in enumerate(cands)])
        ii = jnp.zeros_like(pos)
        jj = jnp.zeros_like(pos)
        for c, (a, b) in enumerate(cands):
            hit = pos == jnp.float32(a * PEER_TOPK + b)
            cand_ref[c] = jnp.where(hit, neg, vals[c])
            ii = jnp.where(hit, si_ref[0, a], ii)
            jj = jnp.where(hit, si_ref[1, b], jj)
        best_ref[k] = mx
        bi_ref[k] = ii
        bj_ref[k] = jj
        return carry

    lax.fori_loop(0, PEER_TOPK, pick, 0)
    best = best_ref[...]
    e = jnp.exp(best - best[0][None])
    g = e / jnp.sum(e, axis=0, keepdims=True)
    g_ref[0] = g.reshape(PEER_TOPK * PEER_HEADS, tm).T
    i_ref[0] = bi_ref[...].reshape(PEER_TOPK * PEER_HEADS, tm).T
    j_ref[0] = bj_ref[...].reshape(PEER_TOPK * PEER_HEADS, tm).T


PEER_TILE_TARGET = 512


def _peer_tile(l):
    t = (PEER_TILE_TARGET // V7X_LANES) * V7X_LANES
    while l % t:
        t -= V7X_LANES
    return t


def _mod_pair_spec(d):
    return pl.BlockSpec((1, 2, 1, d), lambda b, *_: (b, 0, 0, 0))


def peer_route(xs, shift, scale, w_q, keys, n_ctx):
    b, l, d = xs.shape
    tm = _peer_tile(l)
    nsel = PEER_TOPK * PEER_HEADS
    ncand = len(_peer_candidates())
    sel_spec = pl.BlockSpec((1, tm, nsel), lambda i, j: (i, j, 0))
    return pl.pallas_call(
        functools.partial(_peer_route_body, tm=tm, n_ctx=n_ctx),
        grid=(b, l // tm),
        in_specs=[
            pl.BlockSpec((1, tm, d), lambda i, j: (i, j, 0)),
            _mod_pair_spec(d),
            _mod_pair_spec(d),
            pl.BlockSpec(w_q.shape, lambda i, j: (0, 0)),
            pl.BlockSpec(keys.shape, lambda i, j: (0, 0, 0)),
        ],
        out_specs=[pl.BlockSpec((1, tm, d), lambda i, j: (i, j, 0)), sel_spec, sel_spec, sel_spec],
        out_shape=[
            jax.ShapeDtypeStruct((b, l, d), BF16),
            jax.ShapeDtypeStruct((b, l, nsel), F32),
            jax.ShapeDtypeStruct((b, l, nsel), F32),
            jax.ShapeDtypeStruct((b, l, nsel), F32),
        ],
        scratch_shapes=[
            pltpu.VMEM((2, PEER_TOPK, PEER_HEADS, tm), F32),
            pltpu.VMEM((2, PEER_TOPK, PEER_HEADS, tm), F32),
            pltpu.VMEM((ncand, PEER_HEADS, tm), F32),
            pltpu.VMEM((PEER_TOPK, PEER_HEADS, tm), F32),
            pltpu.VMEM((PEER_TOPK, PEER_HEADS, tm), F32),
            pltpu.VMEM((PEER_TOPK, PEER_HEADS, tm), F32),
            pltpu.VMEM((N_KEYS, PEER_HEADS, V7X_LANES), F32),
        ],
        compiler_params=_cparams(("parallel", "parallel")),
        name="peer_route",
    )(xs, shift, scale, w_q.astype(BF16), keys.astype(BF16))


PEER_G_CHUNK = 16
PEER_PAIR = 2 * N_KEYS
PEER_EXPERT_BLOCK = 2048


def _peer_expert_body(h_ref, i_ref, j_ref, g_ref, u_ref, v_ref, x_ref, gate_ref, o_ref, gw_ref, acc_ref, wt_ref,
                      *, tm, eb, n_ctx):
    e = pl.program_id(2)

    @pl.when(e == 0)
    def _():
        acc_ref[...] = jnp.zeros_like(acc_ref)
        isel, jsel, gsel = i_ref[0], j_ref[0], g_ref[0]
        kio = lax.broadcasted_iota(I32, (PEER_G_CHUNK, N_KEYS, isel.shape[1]), 1).astype(F32)
        for c in range(tm // PEER_G_CHUNK):
            rows = slice(c * PEER_G_CHUNK, (c + 1) * PEER_G_CHUNK)
            oh_i = jnp.where(isel[rows][:, None, :] == kio, 1.0, 0.0).astype(BF16)
            oh_j = jnp.where(jsel[rows][:, None, :] == kio, gsel[rows][:, None, :], 0.0).astype(BF16)
            g3 = jnp.einsum("pis,pjs->pij", oh_i, oh_j, preferred_element_type=F32)
            gw_ref[:, rows, :] = jnp.swapaxes(g3, 0, 1).astype(BF16)

    h = h_ref[0]
    for ip in range(eb // PEER_PAIR):
        rows = slice(ip * PEER_PAIR, (ip + 1) * PEER_PAIR)
        act = jax.nn.gelu(jnp.dot(h, u_ref[:, rows], preferred_element_type=F32)).astype(BF16)
        i0 = e * (eb // N_KEYS) + ip * 2
        wt_ref[:, ip * PEER_PAIR:ip * PEER_PAIR + N_KEYS] = act[:, :N_KEYS] * gw_ref[i0]
        wt_ref[:, ip * PEER_PAIR + N_KEYS:(ip + 1) * PEER_PAIR] = act[:, N_KEYS:] * gw_ref[i0 + 1]
    acc_ref[...] += jnp.dot(wt_ref[...], v_ref[...], preferred_element_type=F32)

    @pl.when(e == pl.num_programs(2) - 1)
    def _():
        gate = _segment_rows(gate_ref, pl.program_id(1), tm, n_ctx)
        o_ref[0] = x_ref[0] + gate * acc_ref[...]


def peer_experts(h, isel, jsel, gsel, u_t, v_tab, xs, gate, n_ctx):
    b, l, d = xs.shape
    ne = v_tab.shape[0]
    tm = _peer_tile(l)
    eb = PEER_EXPERT_BLOCK if tm < PEER_TILE_TARGET else PEER_EXPERT_BLOCK // 2
    nsel = isel.shape[2]
    tok_spec = pl.BlockSpec((1, tm, d), lambda i, j, e: (i, j, 0))
    sel_spec = pl.BlockSpec((1, tm, nsel), lambda i, j, e: (i, j, 0))
    return pl.pallas_call(
        functools.partial(_peer_expert_body, tm=tm, eb=eb, n_ctx=n_ctx),
        grid=(b, l // tm, ne // eb),
        in_specs=[
            tok_spec, sel_spec, sel_spec, sel_spec,
            pl.BlockSpec((d, eb), lambda i, j, e: (0, e)),
            pl.BlockSpec((eb, d), lambda i, j, e: (e, 0)),
            tok_spec,
            _mod_pair_spec(d),
        ],
        out_specs=tok_spec,
        out_shape=jax.ShapeDtypeStruct((b, l, d), F32),
        scratch_shapes=[
            pltpu.VMEM((N_KEYS, tm, N_KEYS), BF16),
            pltpu.VMEM((tm, d), F32),
            pltpu.VMEM((tm, eb), BF16),
        ],
        compiler_params=_cparams(("parallel", "parallel", "arbitrary")),
        name="peer_experts",
    )(h, isel, jsel, gsel, u_t, v_tab, xs, gate)


def _group_ones(width, group):
    g = jnp.arange(width) // group
    return (g[:, None] == g[None, :]).astype(BF16)


def _group_sum(x, ones_ref):
    hi = x.astype(BF16)
    rest = x - hi.astype(F32)
    mid = rest.astype(BF16)
    lo = (rest - mid.astype(F32)).astype(BF16)
    ones = ones_ref[...]
    return (jnp.dot(hi, ones, preferred_element_type=F32) + jnp.dot(mid, ones, preferred_element_type=F32)
            + jnp.dot(lo, ones, preferred_element_type=F32))


def _pad_rank_rows(w, lo, total):
    return jnp.pad(w, ((lo, total - lo - w.shape[0]), (0, 0)))


HALO_ROWS = V7X_SUBLANES


def _even_prep_body(xm_ref, xp_ref, xn_ref, sh_ref, sc_ref, win_ref, mu_ref, rvec_ref, wup_ref, aup_ref, gup_ref,
                    ones_ref, cw_ref, wa_ref, wx_ref, lvec_ref,
                    r_ref, decf_ref, decb_ref, ktf_ref, ktb_ref, kk_ref, akkf_ref, akkb_ref, v_ref,
                    bonus_ref, ga_ref, af_ref, uf_ref, ab_ref, ub_ref, gb_ref,
                    *, tm, ctx_tiles, n_tiles, width, a_proj):
    j = pl.program_id(1)
    first = jnp.logical_or(j == 0, j == ctx_tiles)
    last = jnp.logical_or(j == ctx_tiles - 1, j == n_tiles - 1)
    shift, scale = sh_ref[0, 0], sc_ref[0, 0]
    z = jnp.concatenate([_rms_modulate(ref[0], shift, scale) for ref in (xp_ref, xm_ref, xn_ref)], axis=0)
    proj = jnp.dot(z.astype(BF16), win_ref[...], preferred_element_type=F32)
    p = proj[HALO_ROWS:HALO_ROWS + tm]
    prev = jnp.where(first, 0.0, proj[:HALO_ROWS])
    nxt = jnp.where(last, 0.0, proj[HALO_ROWS + tm:])
    row = lax.broadcasted_iota(I32, (tm, 1), 0)

    def shifted(lo, hi, off):
        x = p[:, lo:hi]
        if off < 0:
            y = pltpu.roll(x, -off, axis=0)
            for s in range(-off):
                y = jnp.where(row == s, prev[HALO_ROWS + off + s:HALO_ROWS + off + s + 1, lo:hi], y)
        else:
            y = pltpu.roll(x, tm - off, axis=0)
            for s in range(off):
                y = jnp.where(row == tm - off + s, nxt[s:s + 1, lo:hi], y)
        return y

    pa = p[:, :a_proj]
    pa = pa + mu_ref[...] * (0.5 * (shifted(0, a_proj, -1) + shifted(0, a_proj, 1)) - pa)
    r = pa[:, :width]
    k = pa[:, width:2 * width]
    v = pa[:, 2 * width:3 * width]
    c0 = 3 * width
    wd = jnp.tanh(pa[:, c0:c0 + V7X_LANES]).astype(BF16)
    ad = pa[:, c0 + V7X_LANES:c0 + 2 * V7X_LANES].astype(BF16)
    gd = jax.nn.sigmoid(pa[:, c0 + 2 * V7X_LANES:c0 + 3 * V7X_LANES]).astype(BF16)
    w0_f, w0_b, a0_f, a0_b, k_k, k_a, r_k = (rvec_ref[n:n + 1] for n in range(7))

    kk = k * k_k
    kk = kk / jnp.maximum(jnp.sqrt(_group_sum(kk * kk, ones_ref)), KK_NORM_EPS)
    kts = []
    for w0, a0, d, dec_ref, kt_ref, akk_ref in ((w0_f, a0_f, 0, decf_ref, ktf_ref, akkf_ref),
                                                (w0_b, a0_b, 1, decb_ref, ktb_ref, akkb_ref)):
        lora_w = jnp.dot(wd, wup_ref[d], preferred_element_type=F32)
        dec_ref[0] = jnp.exp(-DECAY_SCALE * jax.nn.sigmoid(w0 + lora_w))
        a = jax.nn.sigmoid(a0 + jnp.dot(ad, aup_ref[d], preferred_element_type=F32))
        kt = k * (1.0 + (a - 1.0) * k_a)
        kt_ref[0] = kt
        akk_ref[0] = a * kk
        kts.append(kt)
    r_ref[0] = r
    kk_ref[0] = kk
    v_ref[0] = v
    bonus_ref[0] = _group_sum(r * (0.5 * (kts[0] + kts[1])) * r_k, ones_ref) * v
    ga_ref[0] = jnp.dot(gd, gup_ref[...], preferred_element_type=F32)

    b0 = a_proj
    conv_b, ba_f, ba_b, bx_f, bx_b, sp_f, sp_b = (lvec_ref[n:n + 1] for n in range(7))
    xb = (cw_ref[0:1] * shifted(b0, b0 + width, -2) + cw_ref[1:2] * shifted(b0, b0 + width, -1)
          + cw_ref[2:3] * p[:, b0:b0 + width] + cw_ref[3:4] * shifted(b0, b0 + width, 1) + conv_b)
    gb_ref[0] = jax.nn.gelu(p[:, b0 + width:b0 + 2 * width])
    xbb = xb.astype(BF16)
    for d, ba, bx, sp, a_ref, u_ref in ((0, ba_f, bx_f, sp_f, af_ref, uf_ref), (1, ba_b, bx_b, sp_b, ab_ref, ub_ref)):
        rg = jax.nn.sigmoid(jnp.dot(xbb, wa_ref[d], preferred_element_type=F32) + ba)
        ig = jax.nn.sigmoid(jnp.dot(xbb, wx_ref[d], preferred_element_type=F32) + bx)
        log_a = -LRU_C * rg * sp
        a_ref[0] = jnp.exp(log_a)
        th = jnp.tanh(log_a)
        u_ref[0] = jnp.sqrt(-2.0 * th / (1.0 - th)) * ig * xb


def even_prep(xs, shift, scale, w_in, n_ctx, mu, w0, w_up, a0, a_up, g_up, k_k, k_a, r_k, conv_w, conv_b, wa, ba, wx,
              bx, lam):
    b, l, d = xs.shape
    n_proj = w_in.shape[1]
    width = w0.shape[-1]
    a_proj = mu.shape[-1]
    assert CONV_W == 4 and 2 * DECAY_RANK == V7X_LANES and 2 * ICLR_RANK == V7X_LANES and GATE_RANK == V7X_LANES
    assert a_proj == 3 * width + 3 * V7X_LANES and n_proj == a_proj + 2 * width
    tm = _token_tile(n_ctx, l - n_ctx, TOKEN_TILE)
    n_tiles, ctx_tiles = l // tm, n_ctx // tm
    halo_per_tile = tm // HALO_ROWS
    last_halo = l // HALO_ROWS - 1
    rvec = jnp.concatenate([w0, a0, k_k[None], k_a[None], r_k.reshape(1, width), jnp.zeros((1, width), F32)], axis=0)
    lvec = jnp.concatenate([conv_b[None], ba, bx, jax.nn.softplus(-lam), jnp.zeros((1, width), F32)], axis=0)
    wup = jnp.stack([_pad_rank_rows(w_up[0], 0, V7X_LANES), _pad_rank_rows(w_up[1], DECAY_RANK, V7X_LANES)])
    aup = jnp.stack([_pad_rank_rows(a_up[0], 0, V7X_LANES), _pad_rank_rows(a_up[1], ICLR_RANK, V7X_LANES)])

    def block_diag(w):
        return jnp.stack([jax.scipy.linalg.block_diag(*w[d]) for d in range(2)]).astype(BF16)

    tok = pl.BlockSpec((1, tm, width), lambda i, j: (i, j, 0))

    def full(a):
        return pl.BlockSpec(a.shape, lambda i, j, nd=a.ndim: (0,) * nd)

    consts = [w_in.astype(BF16), mu[None], rvec, wup.astype(BF16), aup.astype(BF16), g_up.astype(BF16),
              _group_ones(width, A_HEAD_DIM), conv_w, block_diag(wa), block_diag(wx), lvec]
    return pl.pallas_call(
        functools.partial(_even_prep_body, tm=tm, ctx_tiles=ctx_tiles, n_tiles=n_tiles, width=width, a_proj=a_proj),
        grid=(b, n_tiles),
        in_specs=[
            pl.BlockSpec((1, tm, d), lambda i, j: (i, j, 0)),
            pl.BlockSpec((1, HALO_ROWS, d), lambda i, j: (i, jnp.maximum(j * halo_per_tile - 1, 0), 0)),
            pl.BlockSpec((1, HALO_ROWS, d), lambda i, j: (i, jnp.minimum((j + 1) * halo_per_tile, last_halo), 0)),
            _mod_spec(d, ctx_tiles),
            _mod_spec(d, ctx_tiles),
        ] + [full(a) for a in consts],
        out_specs=[tok] * 16,
        out_shape=[jax.ShapeDtypeStruct((b, l, width), F32)] * 16,
        compiler_params=_cparams(("parallel", "parallel")),
        name="even_prep",
    )(xs, xs, xs, shift, scale, *consts)


def _even_post_body(of_ref, ob_ref, bonus_ref, ga_ref, hf_ref, hb_ref, gb_ref, ln_ref, ones_ref,
                    x_ref, g_ref, w_ref, o_ref, *, width):
    o = of_ref[0] + ob_ref[0]
    inv_n = 1.0 / A_HEAD_DIM
    cen = o - _group_sum(o, ones_ref) * inv_n
    var = _group_sum(cen * cen, ones_ref) * inv_n
    on = cen * lax.rsqrt(var + GN_EPS) * ln_ref[0:1] + ln_ref[1:2]
    ya = ((on + bonus_ref[0]) * ga_ref[0]).astype(BF16)
    yb = ((hf_ref[0] + hb_ref[0]) * gb_ref[0]).astype(BF16)
    acc = (jnp.dot(ya, w_ref[:width], preferred_element_type=F32)
           + jnp.dot(yb, w_ref[width:], preferred_element_type=F32))
    o_ref[0] = x_ref[0] + g_ref[0, 0] * acc


def even_post(o_f, o_b, bonus, gate_a, h_f, h_b, gate_b, ln_w, ln_b, xs, gate, w_out, n_ctx):
    b, l, d = xs.shape
    width = o_f.shape[-1]
    tm = _token_tile(n_ctx, l - n_ctx, TOKEN_TILE)
    ctx_tiles = n_ctx // tm
    tok = pl.BlockSpec((1, tm, width), lambda i, j: (i, j, 0))
    ln = jnp.stack([ln_w, ln_b])
    ones = _group_ones(width, A_HEAD_DIM)
    return pl.pallas_call(
        functools.partial(_even_post_body, width=width),
        grid=(b, l // tm),
        in_specs=[tok] * 7 + [
            pl.BlockSpec(ln.shape, lambda i, j: (0, 0)),
            pl.BlockSpec(ones.shape, lambda i, j: (0, 0)),
            pl.BlockSpec((1, tm, d), lambda i, j: (i, j, 0)),
            _mod_spec(d, ctx_tiles),
            pl.BlockSpec(w_out.shape, lambda i, j: (0, 0)),
        ],
        out_specs=pl.BlockSpec((1, tm, d), lambda i, j: (i, j, 0)),
        out_shape=jax.ShapeDtypeStruct((b, l, d), F32),
        compiler_params=_cparams(("parallel", "parallel")),
        name="even_post",
    )(o_f, o_b, bonus, gate_a, h_f, h_b, gate_b, ln, ones, xs, gate, w_out.astype(BF16))


def _rwkv_chain_layout(z, heads):
    b, l, width = z.shape
    n = width // heads
    z = z.reshape(b, l, heads, n).transpose(1, 3, 0, 2).reshape(l, n, b * heads)
    z = jnp.pad(z, ((0, 0), (0, 0), (0, RWKV_CHAINS - b * heads)))
    return z.reshape(l, n, V7X_SUBLANES // RWKV_KPAR, V7X_LANES)


def _rwkv_keyed(z, heads):
    z = _rwkv_chain_layout(z, heads)
    l, n = z.shape[:2]
    return z.reshape(l, n // RWKV_KPAR, V7X_SUBLANES, V7X_LANES)


def _rwkv_valued(z, heads):
    z = _rwkv_chain_layout(z, heads)
    return jnp.concatenate([z] * RWKV_KPAR, axis=2)


def _rwkv_unchain(o, b, heads):
    l, n = o.shape[:2]
    o = o.reshape(l, n, RWKV_CHAINS)[:, :, :b * heads]
    return o.reshape(l, n, b, heads).transpose(2, 0, 3, 1).reshape(b, l, heads * n)


def _even_mixer(xs, shift, scale, gate, n_ctx, w_in, mu, w_out, w0, w_up, a0, a_up, g_up, k_k, k_a, r_k, ln_w, ln_b,
                conv_w, conv_b, wa, ba, wx, bx, lam):
    b = xs.shape[0]
    heads = w0.shape[-1] // A_HEAD_DIM
    assert b * heads <= RWKV_CHAINS
    (r, dec_f, dec_b, kt_f, kt_b, kk, akk_f, akk_b, v, bonus, gate_a, a_f, u_f, a_b, u_b, gate_b) = even_prep(
        xs, shift, scale, w_in, n_ctx, mu, w0, w_up, a0, a_up, g_up, k_k, k_a, r_k, conv_w, conv_b, wa, ba, wx, bx, lam)
    o_f, o_b = rwkv_scan(
        _rwkv_keyed(r, heads), _rwkv_keyed(dec_f, heads), _rwkv_keyed(dec_b, heads),
        _rwkv_keyed(kt_f, heads), _rwkv_keyed(kt_b, heads), _rwkv_keyed(kk, heads),
        _rwkv_keyed(akk_f, heads), _rwkv_keyed(akk_b, heads), _rwkv_valued(v, heads), n_ctx)
    h_f, h_b = diag_scan(a_f, u_f, a_b, u_b, n_ctx)
    return even_post(_rwkv_unchain(o_f, b, heads), _rwkv_unchain(o_b, b, heads), bonus, gate_a, h_f, h_b, gate_b,
                     ln_w, ln_b, xs, gate, w_out, n_ctx)


ROPE_HALF = C_QK_DIM // 4


def _qkv_body(x_ref, sh_ref, sc_ref, w_ref, ones_ref, g_ref, cos_ref, sin_ref, q_ref, k_ref, v_ref, *, width):
    z = _rms_modulate(x_ref[0], sh_ref[0, 0], sc_ref[0, 0])
    qkv = jnp.dot(z.astype(BF16), w_ref[...], preferred_element_type=F32)
    cos = cos_ref[...]
    sin = sin_ref[...]
    lane = lax.broadcasted_iota(I32, cos.shape, 1)
    first_half = (lane % (2 * ROPE_HALF)) < ROPE_HALF
    for part, out_ref in ((0, q_ref), (1, k_ref)):
        gain = g_ref[part:part + 1]
        for c in range(width // V7X_LANES):
            lo = part * width + c * V7X_LANES
            t = qkv[:, lo:lo + V7X_LANES]
            ms = _group_sum(t * t, ones_ref) * (1.0 / C_QK_DIM)
            t = t * lax.rsqrt(ms + RMS_EPS) * gain
            partner = jnp.where(first_half, pltpu.roll(t, V7X_LANES - ROPE_HALF, axis=1),
                                pltpu.roll(t, ROPE_HALF, axis=1))
            t = t * cos + partner * sin
            if part == 0:
                out_ref[0, :, c * V7X_LANES:(c + 1) * V7X_LANES] = t.astype(BF16)
            else:
                out_ref[0, c * V7X_LANES:(c + 1) * V7X_LANES, :] = t.T.astype(BF16)
    v_ref[0] = qkv[:, 2 * width:].astype(BF16)


def _rope_tables(n_ctx, n_lat):
    n_rows = n_lat // GRID_W
    row_pos = jnp.repeat(jnp.arange(n_rows), GRID_W).astype(F32)
    col_pos = jnp.tile(jnp.arange(GRID_W), n_rows).astype(F32)
    inv_freq = ROPE_BASE ** (-jnp.arange(ROPE_HALF, dtype=F32) / ROPE_HALF)

    def one(pos):
        ang = pos[:, None] * inv_freq
        c, s = jnp.cos(ang), jnp.sin(ang)
        return jnp.concatenate([c, c], axis=-1), jnp.concatenate([-s, s], axis=-1)

    (cr, sr), (cc, sc) = one(row_pos), one(col_pos)
    cos = jnp.concatenate([cr, cc], axis=-1)
    sin = jnp.concatenate([sr, sc], axis=-1)
    cos = jnp.concatenate([jnp.ones((n_ctx, C_QK_DIM), F32), cos], axis=0)
    sin = jnp.concatenate([jnp.zeros((n_ctx, C_QK_DIM), F32), sin], axis=0)
    reps = V7X_LANES // C_QK_DIM
    return jnp.tile(cos, (1, reps)), jnp.tile(sin, (1, reps))


def qkv_project(xs, shift, scale, w_qkv, q_g, k_g, n_ctx):
    b, l, d = xs.shape
    width = w_qkv.shape[1] // 3
    tm = _token_tile(n_ctx, l - n_ctx, TOKEN_TILE)
    ctx_tiles = n_ctx // tm
    cos, sin = _rope_tables(n_ctx, l - n_ctx)
    gains = jnp.stack([jnp.tile(q_g, V7X_LANES // C_QK_DIM), jnp.tile(k_g, V7X_LANES // C_QK_DIM)])
    ones = _group_ones(V7X_LANES, C_QK_DIM)
    out = pl.BlockSpec((1, tm, width), lambda i, j: (i, j, 0))
    out_t = pl.BlockSpec((1, width, tm), lambda i, j: (i, 0, j))
    return pl.pallas_call(
        functools.partial(_qkv_body, width=width),
        grid=(b, l // tm),
        in_specs=[
            pl.BlockSpec((1, tm, d), lambda i, j: (i, j, 0)),
            _mod_spec(d, ctx_tiles),
            _mod_spec(d, ctx_tiles),
            pl.BlockSpec(w_qkv.shape, lambda i, j: (0, 0)),
            pl.BlockSpec(ones.shape, lambda i, j: (0, 0)),
            pl.BlockSpec(gains.shape, lambda i, j: (0, 0)),
            pl.BlockSpec((tm, V7X_LANES), lambda i, j: (j, 0)),
            pl.BlockSpec((tm, V7X_LANES), lambda i, j: (j, 0)),
        ],
        out_specs=[out, out_t, out],
        out_shape=[jax.ShapeDtypeStruct((b, l, width), BF16), jax.ShapeDtypeStruct((b, width, l), BF16),
                   jax.ShapeDtypeStruct((b, l, width), BF16)],
        compiler_params=_cparams(("parallel", "parallel")),
        name="qkv_project",
    )(xs, shift, scale, w_qkv.astype(BF16), ones, gains, cos, sin)


def kernel(x, c, ctx, c_ctx, w_mod, b_mod, even_w_in, even_mu, even_w_out, rwkv_w0, rwkv_w_up, rwkv_a0, rwkv_a_up, rwkv_g_up, rwkv_k_k, rwkv_k_a, rwkv_r_k, rwkv_ln_w, rwkv_ln_b, lru_conv_w, lru_conv_b, lru_wa, lru_ba, lru_wx, lru_bx, lru_lam, attn_w_qkv, attn_q_g, attn_k_g, attn_lam_q1, attn_lam_k1, attn_lam_q2, attn_lam_k2, attn_sub_g, attn_w_o, peer_w_q, peer_keys, peer_u, peer_v):
    bsz, n_lat, d = x.shape
    n_ctx = ctx.shape[1]
    depth = w_mod.shape[0]
    a_proj = even_mu.shape[-1]
    s_lat = jax.nn.silu(c)
    s_ctx = jax.nn.silu(c_ctx)
    xs = jnp.concatenate([ctx, x], axis=1)
    for layer in range(depth):
        i = layer // 2
        m_l = s_lat @ w_mod[layer] + b_mod[layer]
        m_c = s_ctx @ w_mod[layer] + b_mod[layer]
        mods = jnp.stack([jnp.broadcast_to(m_c, m_l.shape), m_l], axis=1).reshape(bsz, 2, N_MOD, 1, d)
        shift1, scale1, gate1, shift2, scale2, gate2 = (mods[:, :, n] for n in range(N_MOD))
        if layer % 2 == 0:
            xs = _even_mixer(xs, shift1, scale1, gate1, n_ctx, even_w_in[i], even_mu[i], even_w_out[i],
                             rwkv_w0[i], rwkv_w_up[i], rwkv_a0[i], rwkv_a_up[i], rwkv_g_up[i], rwkv_k_k[i],
                             rwkv_k_a[i], rwkv_r_k[i], rwkv_ln_w[i], rwkv_ln_b[i], lru_conv_w[i], lru_conv_b[i],
                             lru_wa[i], lru_ba[i], lru_wx[i], lru_bx[i], lru_lam[i])
        else:
            lam_init = 0.8 - 0.6 * math.exp(-0.3 * layer)
            q, k, v = qkv_project(xs, shift1, scale1, attn_w_qkv[i], attn_q_g[i], attn_k_g[i], n_ctx)
            lam = (jnp.exp(jnp.sum(attn_lam_q1[i] * attn_lam_k1[i]))
                   - jnp.exp(jnp.sum(attn_lam_q2[i] * attn_lam_k2[i])) + lam_init)
            o = diff_attention_core(q, k, v, lam.reshape(1), attn_sub_g[i].reshape(1, C_V_DIM), n_ctx, lam_init)
            xs = gated_out_proj(o, xs, gate1, attn_w_o[i], n_ctx)
        seg_ctx = n_ctx if layer < depth - 1 else 0
        if layer == depth - 1:
            xs = xs[:, n_ctx:]
        h, isel, jsel, gsel = peer_route(xs, shift2, scale2, peer_w_q[layer], peer_keys[layer], seg_ctx)
        xs = peer_experts(h, isel, jsel, gsel, peer_u[layer].astype(BF16).T, peer_v[layer].astype(BF16), xs,
                          gate2, seg_ctx)
    return xs
```

```python
import functools
import math

import jax
import jax.numpy as jnp
from jax import lax
from jax.experimental import pallas as pl
from jax.experimental.pallas import tpu as pltpu

F32 = jnp.float32
BF16 = jnp.bfloat16
I32 = jnp.int32

V7X_LANES = 128
V7X_SUBLANES = 8
V7X_VMEM_LIMIT_BYTES = 56 * 1024 * 1024

TOKEN_TILE = 256

GRID_W = 64
N_MOD = 6
A_HEAD_DIM = 64
DECAY_SCALE = math.exp(-0.5)
GN_EPS = 64e-5
DECAY_RANK = 64
ICLR_RANK = 64
GATE_RANK = 128
KK_NORM_EPS = 1e-12
CONV_W = 4
LRU_C = 8.0
C_QK_DIM = 64
C_V_DIM = 128
ROPE_BASE = 10000.0
N_KEYS = 128
PEER_HEADS = 8
PEER_TOPK = 16
RMS_EPS = 1e-6


def _cparams(sem):
    return pltpu.CompilerParams(dimension_semantics=sem, vmem_limit_bytes=V7X_VMEM_LIMIT_BYTES)


def _token_tile(n_ctx, n_lat, want):
    t = want
    while n_ctx % t or n_lat % t:
        t //= 2
    return t


def _mod_spec(d, ctx_tiles):
    return pl.BlockSpec((1, 1, 1, d), lambda b, j, *_: (b, jnp.where(j >= ctx_tiles, 1, 0), 0, 0))


def _rms_modulate(x, shift, scale):
    xn = x * lax.rsqrt(jnp.mean(x * x, axis=-1, keepdims=True) + RMS_EPS)
    return xn * (1.0 + scale) + shift


def _outproj_body(y_ref, x_ref, g_ref, w_ref, o_ref):
    acc = jnp.dot(y_ref[0].astype(BF16), w_ref[...], preferred_element_type=F32)
    o_ref[0] = x_ref[0] + g_ref[0, 0] * acc


def gated_out_proj(y, xs, gate, w, n_ctx):
    b, l, d = xs.shape
    k = y.shape[-1]
    tm = _token_tile(n_ctx, l - n_ctx, TOKEN_TILE)
    ctx_tiles = n_ctx // tm
    return pl.pallas_call(
        _outproj_body,
        grid=(b, l // tm),
        in_specs=[
            pl.BlockSpec((1, tm, k), lambda i, j: (i, j, 0)),
            pl.BlockSpec((1, tm, d), lambda i, j: (i, j, 0)),
            _mod_spec(d, ctx_tiles),
            pl.BlockSpec((k, d), lambda i, j: (0, 0)),
        ],
        out_specs=pl.BlockSpec((1, tm, d), lambda i, j: (i, j, 0)),
        out_shape=jax.ShapeDtypeStruct((b, l, d), F32),
        compiler_params=_cparams(("parallel", "parallel")),
        name="gated_out_proj",
    )(y, xs, gate, w.astype(BF16))


RWKV_VBLK_A = 16
RWKV_VBLK_B = 8
RWKV_KPAR = 4
RWKV_CHAINS = (V7X_SUBLANES // RWKV_KPAR) * V7X_LANES
SCAN_TB = 8


def _reversed_block_map(n_blocks, ctx_blocks):
    def rev(i):
        return jnp.where(i < ctx_blocks, ctx_blocks - 1 - i, n_blocks - 1 - i + ctx_blocks)
    return rev


def _rwkv_scan_body(rf_ref, wf_ref, kf_ref, kkf_ref, akkf_ref, vf_ref,
                    rb_ref, wb_ref, kb_ref, kkb_ref, akkb_ref, vb_ref,
                    of_ref, ob_ref, tf_ref, tb_ref, sa_ref, *, tb, nk4, nv):
    @pl.when(pl.program_id(0) == 0)
    def _():
        tf_ref[...] = jnp.zeros_like(tf_ref)
        tb_ref[...] = jnp.zeros_like(tb_ref)

    def fold(acc):
        acc = acc + pltpu.roll(acc, V7X_SUBLANES // 2, axis=1)
        return acc + pltpu.roll(acc, V7X_SUBLANES // 4, axis=1)

    def one_direction(t, r_ref, w_ref, k_ref, kk_ref, akk_ref, v_ref, o_ref, t_ref):
        for vb in range(nv // RWKV_VBLK_A):
            vs = pl.ds(vb * RWKV_VBLK_A, RWKV_VBLK_A)

            def acc_sa(k4, acc):
                return acc + t_ref[k4, vs] * kk_ref[t, k4][None]

            acc = lax.fori_loop(0, nk4, acc_sa, jnp.zeros((RWKV_VBLK_A, V7X_SUBLANES, V7X_LANES), F32), unroll=True)
            sa_ref[vs] = fold(acc)
        for vb in range(nv // RWKV_VBLK_B):
            vs = pl.ds(vb * RWKV_VBLK_B, RWKV_VBLK_B)
            sa = sa_ref[vs]
            vv = v_ref[t, vs]

            def upd(k4, acc):
                tn = (t_ref[k4, vs] * w_ref[t, k4][None] - akk_ref[t, k4][None] * sa
                      + k_ref[t, k4][None] * vv)
                t_ref[k4, vs] = tn
                return acc + tn * r_ref[t, k4][None]

            acc = lax.fori_loop(0, nk4, upd, jnp.zeros((RWKV_VBLK_B, V7X_SUBLANES, V7X_LANES), F32), unroll=True)
            o_ref[t, vs] = fold(acc)[:, :V7X_SUBLANES // RWKV_KPAR]

    def step(t, carry):
        one_direction(t, rf_ref, wf_ref, kf_ref, kkf_ref, akkf_ref, vf_ref, of_ref, tf_ref)
        one_direction(tb - 1 - t, rb_ref, wb_ref, kb_ref, kkb_ref, akkb_ref, vb_ref, ob_ref, tb_ref)
        return carry

    lax.fori_loop(0, tb, step, 0)


def rwkv_scan(r, w_f, w_b, k_f, k_b, kk, akk_f, akk_b, v, n_ctx):
    l, nk4 = r.shape[0], r.shape[1]
    nv = v.shape[1]
    tb = SCAN_TB
    rev = _reversed_block_map(l // tb, n_ctx // tb)
    kblk = (tb, nk4, V7X_SUBLANES, V7X_LANES)
    vblk = (tb, nv, V7X_SUBLANES, V7X_LANES)
    kf = pl.BlockSpec(kblk, lambda i: (i, 0, 0, 0))
    kb = pl.BlockSpec(kblk, lambda i: (rev(i), 0, 0, 0))
    vf = pl.BlockSpec(vblk, lambda i: (i, 0, 0, 0))
    vb = pl.BlockSpec(vblk, lambda i: (rev(i), 0, 0, 0))
    oblk = (tb, nv, V7X_SUBLANES // RWKV_KPAR, V7X_LANES)
    out = jax.ShapeDtypeStruct((l,) + oblk[1:], F32)
    state = pltpu.VMEM((nk4, nv, V7X_SUBLANES, V7X_LANES), F32)
    return pl.pallas_call(
        functools.partial(_rwkv_scan_body, tb=tb, nk4=nk4, nv=nv),
        grid=(l // tb,),
        in_specs=[kf, kf, kf, kf, kf, vf, kb, kb, kb, kb, kb, vb],
        out_specs=[pl.BlockSpec(oblk, lambda i: (i, 0, 0, 0)), pl.BlockSpec(oblk, lambda i: (rev(i), 0, 0, 0))],
        out_shape=[out, out],
        scratch_shapes=[state, state, pltpu.VMEM((nv, V7X_SUBLANES, V7X_LANES), F32)],
        compiler_params=_cparams(("arbitrary",)),
        name="rwkv_scan",
    )(r, w_f, k_f, kk, akk_f, v, r, w_b, k_b, kk, akk_b, v)


def _diag_scan_body(af_ref, uf_ref, ab_ref, ub_ref, of_ref, ob_ref, hf_ref, hb_ref, *, tb):
    @pl.when(pl.program_id(0) == 0)
    def _():
        hf_ref[...] = jnp.zeros_like(hf_ref)
        hb_ref[...] = jnp.zeros_like(hb_ref)

    def step(t, carry):
        hf, hb = carry
        hf = af_ref[:, t, :] * hf + uf_ref[:, t, :]
        of_ref[:, t, :] = hf
        tr = tb - 1 - t
        hb = ab_ref[:, tr, :] * hb + ub_ref[:, tr, :]
        ob_ref[:, tr, :] = hb
        return hf, hb

    hf, hb = lax.fori_loop(0, tb, step, (hf_ref[...], hb_ref[...]))
    hf_ref[...] = hf
    hb_ref[...] = hb


def diag_scan(a_f, u_f, a_b, u_b, n_ctx):
    b, l, c = a_f.shape
    tb = SCAN_TB
    rev = _reversed_block_map(l // tb, n_ctx // tb)
    fwd = pl.BlockSpec((b, tb, c), lambda i: (0, i, 0))
    bwd = pl.BlockSpec((b, tb, c), lambda i: (0, rev(i), 0))
    out = jax.ShapeDtypeStruct((b, l, c), F32)
    return pl.pallas_call(
        functools.partial(_diag_scan_body, tb=tb),
        grid=(l // tb,),
        in_specs=[fwd, fwd, bwd, bwd],
        out_specs=[fwd, bwd],
        out_shape=[out, out],
        scratch_shapes=[pltpu.VMEM((b, c), F32), pltpu.VMEM((b, c), F32)],
        compiler_params=_cparams(("arbitrary",)),
        name="diag_scan",
    )(a_f, u_f, a_b, u_b)


ATTN_ROW_GROUPS = 2


def _attn_body(lam_ref, q_ref, k_ref, v_ref, sg_ref, o_ref, *, ctx_tiles, n_ctx, scale, out_scale):
    j = pl.program_id(2)
    q = q_ref[0]
    lane = lax.broadcasted_iota(I32, q.shape, 1)
    qs = q * jnp.asarray(scale, BF16)
    q1 = jnp.where(lane < C_QK_DIM, qs, jnp.zeros_like(qs))
    q2 = jnp.where(lane >= C_QK_DIM, qs, jnp.zeros_like(qs))
    lam = lam_ref[0]

    def attend(kt, vb):
        rows = q.shape[0] // ATTN_ROW_GROUPS
        pieces = [qm[r * rows:(r + 1) * rows] for r in range(ATTN_ROW_GROUPS) for qm in (q1, q2)]

        def softmax_times_v(s):
            e = jnp.exp(s - jnp.max(s, axis=-1, keepdims=True))
            pv = jnp.dot(e.astype(BF16), vb, preferred_element_type=F32)
            return pv / jnp.sum(e, axis=-1, keepdims=True)

        ahead = 2
        scores = [jnp.dot(qp, kt, preferred_element_type=F32) for qp in pieces[:ahead]]
        outs = []
        for n in range(len(pieces)):
            if n + ahead < len(pieces):
                scores.append(jnp.dot(pieces[n + ahead], kt, preferred_element_type=F32))
            outs.append(softmax_times_v(scores[n]))
        o = jnp.concatenate([outs[2 * r] - lam * outs[2 * r + 1] for r in range(ATTN_ROW_GROUPS)], axis=0)
        o = o * lax.rsqrt(jnp.mean(o * o, axis=-1, keepdims=True) + RMS_EPS)
        o_ref[0] = o * sg_ref[...] * out_scale

    @pl.when(j < ctx_tiles)
    def _():
        attend(k_ref[0, :, :n_ctx], v_ref[0, :n_ctx])

    @pl.when(j >= ctx_tiles)
    def _():
        attend(k_ref[0], v_ref[0])


def diff_attention_core(q, k_t, v, lam, sub_g, n_ctx, lam_init):
    b, l, width = q.shape
    heads = width // C_V_DIM
    assert n_ctx % V7X_LANES == 0
    tq = _token_tile(n_ctx, l - n_ctx, TOKEN_TILE)
    ctx_tiles = n_ctx // tq
    scale = C_QK_DIM ** -0.5
    assert math.frexp(scale)[0] == 0.5, "the kernel folds the score scale into q, exact only for a power of two"
    body = functools.partial(_attn_body, ctx_tiles=ctx_tiles, n_ctx=n_ctx, scale=scale, out_scale=1.0 - lam_init)
    return pl.pallas_call(
        body,
        grid=(b, heads, l // tq),
        in_specs=[
            pl.BlockSpec(memory_space=pltpu.SMEM),
            pl.BlockSpec((1, tq, C_V_DIM), lambda i, h, j: (i, j, h)),
            pl.BlockSpec((1, C_V_DIM, l), lambda i, h, j: (i, h, 0)),
            pl.BlockSpec((1, l, C_V_DIM), lambda i, h, j: (i, 0, h)),
            pl.BlockSpec((1, C_V_DIM), lambda i, h, j: (0, 0)),
        ],
        out_specs=pl.BlockSpec((1, tq, C_V_DIM), lambda i, h, j: (i, j, h)),
        out_shape=jax.ShapeDtypeStruct((b, l, width), F32),
        compiler_params=_cparams(("parallel", "parallel", "parallel")),
        name="diff_attention",
    )(lam, q, k_t, v, sub_g)


def _peer_candidates():
    return [(a, b) for a in range(PEER_TOPK) for b in range(PEER_TOPK) if (a + 1) * (b + 1) <= PEER_TOPK]


def _segment_rows(mod_ref, tile, tm, n_ctx):
    row = tile * tm + lax.broadcasted_iota(I32, (tm, 1), 0)
    return jnp.where(row < n_ctx, mod_ref[0, 0], mod_ref[0, 1])


def _peer_route_body(x_ref, sh_ref, sc_ref, wq_ref, keys_ref, h_ref, i_ref, j_ref, g_ref,
                     sv_ref, si_ref, cand_ref, best_ref, bi_ref, bj_ref, s_ref, *, tm, n_ctx):
    tile = pl.program_id(1)
    h = _rms_modulate(x_ref[0], _segment_rows(sh_ref, tile, tm, n_ctx),
                      _segment_rows(sc_ref, tile, tm, n_ctx)).astype(BF16)
    h_ref[0] = h
    q = jnp.dot(h, wq_ref[...], preferred_element_type=F32).astype(BF16)
    nt = (((1,), (1,)), ((), ()))
    neg = jnp.float32(-jnp.inf)

    def best_of(nodes):
        while len(nodes) > 1:
            nxt = []
            for (va, ia), (vb, ib) in zip(nodes[0::2], nodes[1::2]):
                left = va >= vb
                nxt.append((jnp.maximum(va, vb), jnp.where(left, ia, ib)))
            if len(nodes) % 2:
                nxt.append(nodes[-1])
            nodes = nxt
        return nodes[0]

    for m in range(2):
        for col in range(tm // V7X_LANES):
            toks = slice(col * V7X_LANES, (col + 1) * V7X_LANES)
            per_head = [
                lax.dot_general(keys_ref[m], q[toks, (hd * 2 + m) * N_KEYS:(hd * 2 + m + 1) * N_KEYS], nt,
                                preferred_element_type=F32) for hd in range(PEER_HEADS)]
            s_ref[...] = jnp.swapaxes(jnp.stack(per_head, axis=0), 0, 1)

            def extract(a, carry, m=m, toks=toks):
                vals = [s_ref[n] for n in range(N_KEYS)]
                mx, idx = best_of([(vals[n], jnp.float32(n)) for n in range(N_KEYS)])
                sv_ref[m, a, :, toks] = mx
                si_ref[m, a, :, toks] = idx
                for n in range(N_KEYS):
                    s_ref[n] = jnp.where(idx == jnp.float32(n), neg, vals[n])
                return carry

            lax.fori_loop(0, PEER_TOPK, extract, 0)

    cands = _peer_candidates()
    for c, (a, b) in enumerate(cands):
        cand_ref[c] = sv_ref[0, a] + sv_ref[1, b]
    def pick(k, carry):
        vals = [cand_ref[c] for c in range(len(cands))]
        mx, pos = best_of([(vals[c], jnp.float32(a * PEER_TOPK + b)) for c, (a, b) in enumerate(cands)])
        for c, (a, b) in enumerate(cands):
            cand_ref[c] = jnp.where(pos == jnp.float32(a * PEER_TOPK + b), neg, vals[c])
        ra = jnp.floor(pos * (1.0 / PEER_TOPK))
        rb = pos - ra * PEER_TOPK
        ii = si_ref[0, 0]
        jj = si_ref[1, 0]
        for n in range(1, PEER_TOPK):
            ii = jnp.where(ra == jnp.float32(n), si_ref[0, n], ii)
            jj = jnp.where(rb == jnp.float32(n), si_ref[1, n], jj)
        best_ref[k] = mx
        bi_ref[k] = ii
        bj_ref[k] = jj
        return carry

    lax.fori_loop(0, PEER_TOPK, pick, 0)
    best = best_ref[...]
    e = jnp.exp(best - best[0][None])
    g = e / jnp.sum(e, axis=0, keepdims=True)
    g_ref[0] = g.reshape(PEER_TOPK * PEER_HEADS, tm).T
    i_ref[0] = bi_ref[...].reshape(PEER_TOPK * PEER_HEADS, tm).T
    j_ref[0] = bj_ref[...].reshape(PEER_TOPK * PEER_HEADS, tm).T


PEER_TILE_TARGET = 512


def _peer_tile(l):
    t = (PEER_TILE_TARGET // V7X_LANES) * V7X_LANES
    while l % t:
        t -= V7X_LANES
    return t


def _mod_pair_spec(d):
    return pl.BlockSpec((1, 2, 1, d), lambda b, *_: (b, 0, 0, 0))


def peer_route(xs, shift, scale, w_q, keys, n_ctx):
    b, l, d = xs.shape
    tm = _peer_tile(l)
    nsel = PEER_TOPK * PEER_HEADS
    ncand = len(_peer_candidates())
    sel_spec = pl.BlockSpec((1, tm, nsel), lambda i, j: (i, j, 0))
    return pl.pallas_call(
        functools.partial(_peer_route_body, tm=tm, n_ctx=n_ctx),
        grid=(b, l // tm),
        in_specs=[
            pl.BlockSpec((1, tm, d), lambda i, j: (i, j, 0)),
            _mod_pair_spec(d),
            _mod_pair_spec(d),
            pl.BlockSpec(w_q.shape, lambda i, j: (0, 0)),
            pl.BlockSpec(keys.shape, lambda i, j: (0, 0, 0)),
        ],
        out_specs=[pl.BlockSpec((1, tm, d), lambda i, j: (i, j, 0)), sel_spec, sel_spec, sel_spec],
        out_shape=[
            jax.ShapeDtypeStruct((b, l, d), BF16),
            jax.ShapeDtypeStruct((b, l, nsel), F32),
            jax.ShapeDtypeStruct((b, l, nsel), F32),
            jax.ShapeDtypeStruct((b, l, nsel), F32),
        ],
        scratch_shapes=[
            pltpu.VMEM((2, PEER_TOPK, PEER_HEADS, tm), F32),
            pltpu.VMEM((2, PEER_TOPK, PEER_HEADS, tm), F32),
            pltpu.VMEM((ncand, PEER_HEADS, tm), F32),
            pltpu.VMEM((PEER_TOPK, PEER_HEADS, tm), F32),
            pltpu.VMEM((PEER_TOPK, PEER_HEADS, tm), F32),
            pltpu.VMEM((PEER_TOPK, PEER_HEADS, tm), F32),
            pltpu.VMEM((N_KEYS, PEER_HEADS, V7X_LANES), F32),
        ],
        compiler_params=_cparams(("parallel", "parallel")),
        name="peer_route",
    )(xs, shift, scale, w_q.astype(BF16), keys.astype(BF16))


PEER_G_CHUNK = 16
PEER_PAIR = 2 * N_KEYS
PEER_EXPERT_BLOCK = 2048


def _peer_expert_body(h_ref, i_ref, j_ref, g_ref, u_ref, v_ref, x_ref, gate_ref, o_ref, gw_ref, acc_ref, wt_ref,
                      *, tm, eb, n_ctx):
    e = pl.program_id(2)

    @pl.when(e == 0)
    def _():
        acc_ref[...] = jnp.zeros_like(acc_ref)
        isel, jsel, gsel = i_ref[0], j_ref[0], g_ref[0]
        kio = lax.broadcasted_iota(I32, (PEER_G_CHUNK, N_KEYS, isel.shape[1]), 1).astype(F32)
        for c in range(tm // PEER_G_CHUNK):
            rows = slice(c * PEER_G_CHUNK, (c + 1) * PEER_G_CHUNK)
            oh_i = jnp.where(isel[rows][:, None, :] == kio, 1.0, 0.0).astype(BF16)
            oh_j = jnp.where(jsel[rows][:, None, :] == kio, gsel[rows][:, None, :], 0.0).astype(BF16)
            g3 = jnp.einsum("pis,pjs->pij", oh_i, oh_j, preferred_element_type=F32)
            gw_ref[:, rows, :] = jnp.swapaxes(g3, 0, 1).astype(BF16)

    h = h_ref[0]
    for ip in range(eb // PEER_PAIR):
        rows = slice(ip * PEER_PAIR, (ip + 1) * PEER_PAIR)
        act = jax.nn.gelu(jnp.dot(h, u_ref[:, rows], preferred_element_type=F32)).astype(BF16)
        i0 = e * (eb // N_KEYS) + ip * 2
        wt_ref[:, ip * PEER_PAIR:ip * PEER_PAIR + N_KEYS] = act[:, :N_KEYS] * gw_ref[i0]
        wt_ref[:, ip * PEER_PAIR + N_KEYS:(ip + 1) * PEER_PAIR] = act[:, N_KEYS:] * gw_ref[i0 + 1]
    acc_ref[...] += jnp.dot(wt_ref[...], v_ref[...], preferred_element_type=F32)

    @pl.when(e == pl.num_programs(2) - 1)
    def _():
        gate = _segment_rows(gate_ref, pl.program_id(1), tm, n_ctx)
        o_ref[0] = x_ref[0] + gate * acc_ref[...]


def peer_experts(h, isel, jsel, gsel, u_t, v_tab, xs, gate, n_ctx):
    b, l, d = xs.shape
    ne = v_tab.shape[0]
    tm = _peer_tile(l)
    eb = PEER_EXPERT_BLOCK if tm < PEER_TILE_TARGET else PEER_EXPERT_BLOCK // 2
    nsel = isel.shape[2]
    tok_spec = pl.BlockSpec((1, tm, d), lambda i, j, e: (i, j, 0))
    sel_spec = pl.BlockSpec((1, tm, nsel), lambda i, j, e: (i, j, 0))
    return pl.pallas_call(
        functools.partial(_peer_expert_body, tm=tm, eb=eb, n_ctx=n_ctx),
        grid=(b, l // tm, ne // eb),
        in_specs=[
            tok_spec, sel_spec, sel_spec, sel_spec,
            pl.BlockSpec((d, eb), lambda i, j, e: (0, e)),
            pl.BlockSpec((eb, d), lambda i, j, e: (e, 0)),
            tok_spec,
            _mod_pair_spec(d),
        ],
        out_specs=tok_spec,
        out_shape=jax.ShapeDtypeStruct((b, l, d), F32),
        scratch_shapes=[
            pltpu.VMEM((N_KEYS, tm, N_KEYS), BF16),
            pltpu.VMEM((tm, d), F32),
            pltpu.VMEM((tm, eb), BF16),
        ],
        compiler_params=_cparams(("parallel", "parallel", "arbitrary")),
        name="peer_experts",
    )(h, isel, jsel, gsel, u_t, v_tab, xs, gate)


def _group_ones(width, group):
    g = jnp.arange(width) // group
    return (g[:, None] == g[None, :]).astype(BF16)


def _group_sum(x, ones_ref):
    hi = x.astype(BF16)
    rest = x - hi.astype(F32)
    mid = rest.astype(BF16)
    lo = (rest - mid.astype(F32)).astype(BF16)
    ones = ones_ref[...]
    return (jnp.dot(hi, ones, preferred_element_type=F32) + jnp.dot(mid, ones, preferred_element_type=F32)
            + jnp.dot(lo, ones, preferred_element_type=F32))


def _pad_rank_rows(w, lo, total):
    return jnp.pad(w, ((lo, total - lo - w.shape[0]), (0, 0)))


HALO_ROWS = V7X_SUBLANES


def _even_prep_body(xm_ref, xp_ref, xn_ref, sh_ref, sc_ref, win_ref, mu_ref, rvec_ref, wup_ref, aup_ref, gup_ref,
                    ones_ref, cw_ref, wa_ref, wx_ref, lvec_ref,
                    r_ref, decf_ref, decb_ref, ktf_ref, ktb_ref, kk_ref, akkf_ref, akkb_ref, v_ref,
                    bonus_ref, ga_ref, af_ref, uf_ref, ab_ref, ub_ref, gb_ref,
                    *, tm, ctx_tiles, n_tiles, width, a_proj):
    j = pl.program_id(1)
    first = jnp.logical_or(j == 0, j == ctx_tiles)
    last = jnp.logical_or(j == ctx_tiles - 1, j == n_tiles - 1)
    shift, scale = sh_ref[0, 0], sc_ref[0, 0]
    z = jnp.concatenate([_rms_modulate(ref[0], shift, scale) for ref in (xp_ref, xm_ref, xn_ref)], axis=0)
    proj = jnp.dot(z.astype(BF16), win_ref[...], preferred_element_type=F32)
    p = proj[HALO_ROWS:HALO_ROWS + tm]
    prev = jnp.where(first, 0.0, proj[:HALO_ROWS])
    nxt = jnp.where(last, 0.0, proj[HALO_ROWS + tm:])
    row = lax.broadcasted_iota(I32, (tm, 1), 0)

    def shifted(lo, hi, off):
        x = p[:, lo:hi]
        if off < 0:
            y = pltpu.roll(x, -off, axis=0)
            for s in range(-off):
                y = jnp.where(row == s, prev[HALO_ROWS + off + s:HALO_ROWS + off + s + 1, lo:hi], y)
        else:
            y = pltpu.roll(x, tm - off, axis=0)
            for s in range(off):
                y = jnp.where(row == tm - off + s, nxt[s:s + 1, lo:hi], y)
        return y

    pa = p[:, :a_proj]
    pa = pa + mu_ref[...] * (0.5 * (shifted(0, a_proj, -1) + shifted(0, a_proj, 1)) - pa)
    r = pa[:, :width]
    k = pa[:, width:2 * width]
    v = pa[:, 2 * width:3 * width]
    c0 = 3 * width
    wd = jnp.tanh(pa[:, c0:c0 + V7X_LANES]).astype(BF16)
    ad = pa[:, c0 + V7X_LANES:c0 + 2 * V7X_LANES].astype(BF16)
    gd = jax.nn.sigmoid(pa[:, c0 + 2 * V7X_LANES:c0 + 3 * V7X_LANES]).astype(BF16)
    w0_f, w0_b, a0_f, a0_b, k_k, k_a, r_k = (rvec_ref[n:n + 1] for n in range(7))

    kk = k * k_k
    kk = kk / jnp.maximum(jnp.sqrt(_group_sum(kk * kk, ones_ref)), KK_NORM_EPS)
    kts = []
    for w0, a0, d, dec_ref, kt_ref, akk_ref in ((w0_f, a0_f, 0, decf_ref, ktf_ref, akkf_ref),
                                                (w0_b, a0_b, 1, decb_ref, ktb_ref, akkb_ref)):
        lora_w = jnp.dot(wd, wup_ref[d], preferred_element_type=F32)
        dec_ref[0] = jnp.exp(-DECAY_SCALE * jax.nn.sigmoid(w0 + lora_w))
        a = jax.nn.sigmoid(a0 + jnp.dot(ad, aup_ref[d], preferred_element_type=F32))
        kt = k * (1.0 + (a - 1.0) * k_a)
        kt_ref[0] = kt
        akk_ref[0] = a * kk
        kts.append(kt)
    r_ref[0] = r
    kk_ref[0] = kk
    v_ref[0] = v
    bonus_ref[0] = _group_sum(r * (0.5 * (kts[0] + kts[1])) * r_k, ones_ref) * v
    ga_ref[0] = jnp.dot(gd, gup_ref[...], preferred_element_type=F32)

    b0 = a_proj
    conv_b, ba_f, ba_b, bx_f, bx_b, sp_f, sp_b = (lvec_ref[n:n + 1] for n in range(7))
    xb = (cw_ref[0:1] * shifted(b0, b0 + width, -2) + cw_ref[1:2] * shifted(b0, b0 + width, -1)
          + cw_ref[2:3] * p[:, b0:b0 + width] + cw_ref[3:4] * shifted(b0, b0 + width, 1) + conv_b)
    gb_ref[0] = jax.nn.gelu(p[:, b0 + width:b0 + 2 * width])
    xbb = xb.astype(BF16)
    for d, ba, bx, sp, a_ref, u_ref in ((0, ba_f, bx_f, sp_f, af_ref, uf_ref), (1, ba_b, bx_b, sp_b, ab_ref, ub_ref)):
        rg = jax.nn.sigmoid(jnp.dot(xbb, wa_ref[d], preferred_element_type=F32) + ba)
        ig = jax.nn.sigmoid(jnp.dot(xbb, wx_ref[d], preferred_element_type=F32) + bx)
        log_a = -LRU_C * rg * sp
        a_ref[0] = jnp.exp(log_a)
        th = jnp.tanh(log_a)
        u_ref[0] = jnp.sqrt(-2.0 * th / (1.0 - th)) * ig * xb


def even_prep(xs, shift, scale, w_in, n_ctx, mu, w0, w_up, a0, a_up, g_up, k_k, k_a, r_k, conv_w, conv_b, wa, ba, wx,
              bx, lam):
    b, l, d = xs.shape
    n_proj = w_in.shape[1]
    width = w0.shape[-1]
    a_proj = mu.shape[-1]
    assert CONV_W == 4 and 2 * DECAY_RANK == V7X_LANES and 2 * ICLR_RANK == V7X_LANES and GATE_RANK == V7X_LANES
    assert a_proj == 3 * width + 3 * V7X_LANES and n_proj == a_proj + 2 * width
    tm = _token_tile(n_ctx, l - n_ctx, TOKEN_TILE)
    n_tiles, ctx_tiles = l // tm, n_ctx // tm
    halo_per_tile = tm // HALO_ROWS
    last_halo = l // HALO_ROWS - 1
    rvec = jnp.concatenate([w0, a0, k_k[None], k_a[None], r_k.reshape(1, width), jnp.zeros((1, width), F32)], axis=0)
    lvec = jnp.concatenate([conv_b[None], ba, bx, jax.nn.softplus(-lam), jnp.zeros((1, width), F32)], axis=0)
    wup = jnp.stack([_pad_rank_rows(w_up[0], 0, V7X_LANES), _pad_rank_rows(w_up[1], DECAY_RANK, V7X_LANES)])
    aup = jnp.stack([_pad_rank_rows(a_up[0], 0, V7X_LANES), _pad_rank_rows(a_up[1], ICLR_RANK, V7X_LANES)])

    def block_diag(w):
        return jnp.stack([jax.scipy.linalg.block_diag(*w[d]) for d in range(2)]).astype(BF16)

    tok = pl.BlockSpec((1, tm, width), lambda i, j: (i, j, 0))

    def full(a):
        return pl.BlockSpec(a.shape, lambda i, j, nd=a.ndim: (0,) * nd)

    consts = [w_in.astype(BF16), mu[None], rvec, wup.astype(BF16), aup.astype(BF16), g_up.astype(BF16),
              _group_ones(width, A_HEAD_DIM), conv_w, block_diag(wa), block_diag(wx), lvec]
    return pl.pallas_call(
        functools.partial(_even_prep_body, tm=tm, ctx_tiles=ctx_tiles, n_tiles=n_tiles, width=width, a_proj=a_proj),
        grid=(b, n_tiles),
        in_specs=[
            pl.BlockSpec((1, tm, d), lambda i, j: (i, j, 0)),
            pl.BlockSpec((1, HALO_ROWS, d), lambda i, j: (i, jnp.maximum(j * halo_per_tile - 1, 0), 0)),
            pl.BlockSpec((1, HALO_ROWS, d), lambda i, j: (i, jnp.minimum((j + 1) * halo_per_tile, last_halo), 0)),
            _mod_spec(d, ctx_tiles),
            _mod_spec(d, ctx_tiles),
        ] + [full(a) for a in consts],
        out_specs=[tok] * 16,
        out_shape=[jax.ShapeDtypeStruct((b, l, width), F32)] * 16,
        compiler_params=_cparams(("parallel", "parallel")),
        name="even_prep",
    )(xs, xs, xs, shift, scale, *consts)


def _even_post_body(of_ref, ob_ref, bonus_ref, ga_ref, hf_ref, hb_ref, gb_ref, ln_ref, ones_ref,
                    x_ref, g_ref, w_ref, o_ref, *, width):
    o = of_ref[0] + ob_ref[0]
    inv_n = 1.0 / A_HEAD_DIM
    cen = o - _group_sum(o, ones_ref) * inv_n
    var = _group_sum(cen * cen, ones_ref) * inv_n
    on = cen * lax.rsqrt(var + GN_EPS) * ln_ref[0:1] + ln_ref[1:2]
    ya = ((on + bonus_ref[0]) * ga_ref[0]).astype(BF16)
    yb = ((hf_ref[0] + hb_ref[0]) * gb_ref[0]).astype(BF16)
    acc = (jnp.dot(ya, w_ref[:width], preferred_element_type=F32)
           + jnp.dot(yb, w_ref[width:], preferred_element_type=F32))
    o_ref[0] = x_ref[0] + g_ref[0, 0] * acc


def even_post(o_f, o_b, bonus, gate_a, h_f, h_b, gate_b, ln_w, ln_b, xs, gate, w_out, n_ctx):
    b, l, d = xs.shape
    width = o_f.shape[-1]
    tm = _token_tile(n_ctx, l - n_ctx, TOKEN_TILE)
    ctx_tiles = n_ctx // tm
    tok = pl.BlockSpec((1, tm, width), lambda i, j: (i, j, 0))
    ln = jnp.stack([ln_w, ln_b])
    ones = _group_ones(width, A_HEAD_DIM)
    return pl.pallas_call(
        functools.partial(_even_post_body, width=width),
        grid=(b, l // tm),
        in_specs=[tok] * 7 + [
            pl.BlockSpec(ln.shape, lambda i, j: (0, 0)),
            pl.BlockSpec(ones.shape, lambda i, j: (0, 0)),
            pl.BlockSpec((1, tm, d), lambda i, j: (i, j, 0)),
            _mod_spec(d, ctx_tiles),
            pl.BlockSpec(w_out.shape, lambda i, j: (0, 0)),
        ],
        out_specs=pl.BlockSpec((1, tm, d), lambda i, j: (i, j, 0)),
        out_shape=jax.ShapeDtypeStruct((b, l, d), F32),
        compiler_params=_cparams(("parallel", "parallel")),
        name="even_post",
    )(o_f, o_b, bonus, gate_a, h_f, h_b, gate_b, ln, ones, xs, gate, w_out.astype(BF16))


def _rwkv_chain_layout(z, heads):
    b, l, width = z.shape
    n = width // heads
    z = z.reshape(b, l, heads, n).transpose(1, 3, 0, 2).reshape(l, n, b * heads)
    z = jnp.pad(z, ((0, 0), (0, 0), (0, RWKV_CHAINS - b * heads)))
    return z.reshape(l, n, V7X_SUBLANES // RWKV_KPAR, V7X_LANES)


def _rwkv_keyed(z, heads):
    z = _rwkv_chain_layout(z, heads)
    l, n = z.shape[:2]
    return z.reshape(l, n // RWKV_KPAR, V7X_SUBLANES, V7X_LANES)


def _rwkv_valued(z, heads):
    z = _rwkv_chain_layout(z, heads)
    return jnp.concatenate([z] * RWKV_KPAR, axis=2)


def _rwkv_unchain(o, b, heads):
    l, n = o.shape[:2]
    o = o.reshape(l, n, RWKV_CHAINS)[:, :, :b * heads]
    return o.reshape(l, n, b, heads).transpose(2, 0, 3, 1).reshape(b, l, heads * n)


def _even_mixer(xs, shift, scale, gate, n_ctx, w_in, mu, w_out, w0, w_up, a0, a_up, g_up, k_k, k_a, r_k, ln_w, ln_b,
                conv_w, conv_b, wa, ba, wx, bx, lam):
    b = xs.shape[0]
    heads = w0.shape[-1] // A_HEAD_DIM
    assert b * heads <= RWKV_CHAINS
    (r, dec_f, dec_b, kt_f, kt_b, kk, akk_f, akk_b, v, bonus, gate_a, a_f, u_f, a_b, u_b, gate_b) = even_prep(
        xs, shift, scale, w_in, n_ctx, mu, w0, w_up, a0, a_up, g_up, k_k, k_a, r_k, conv_w, conv_b, wa, ba, wx, bx, lam)
    o_f, o_b = rwkv_scan(
        _rwkv_keyed(r, heads), _rwkv_keyed(dec_f, heads), _rwkv_keyed(dec_b, heads),
        _rwkv_keyed(kt_f, heads), _rwkv_keyed(kt_b, heads), _rwkv_keyed(kk, heads),
        _rwkv_keyed(akk_f, heads), _rwkv_keyed(akk_b, heads), _rwkv_valued(v, heads), n_ctx)
    h_f, h_b = diag_scan(a_f, u_f, a_b, u_b, n_ctx)
    return even_post(_rwkv_unchain(o_f, b, heads), _rwkv_unchain(o_b, b, heads), bonus, gate_a, h_f, h_b, gate_b,
                     ln_w, ln_b, xs, gate, w_out, n_ctx)


ROPE_HALF = C_QK_DIM // 4


def _qkv_body(x_ref, sh_ref, sc_ref, w_ref, ones_ref, g_ref, cos_ref, sin_ref, q_ref, k_ref, v_ref, *, width):
    z = _rms_modulate(x_ref[0], sh_ref[0, 0], sc_ref[0, 0])
    qkv = jnp.dot(z.astype(BF16), w_ref[...], preferred_element_type=F32)
    cos = cos_ref[...]
    sin = sin_ref[...]
    lane = lax.broadcasted_iota(I32, cos.shape, 1)
    first_half = (lane % (2 * ROPE_HALF)) < ROPE_HALF
    for part, out_ref in ((0, q_ref), (1, k_ref)):
        gain = g_ref[part:part + 1]
        for c in range(width // V7X_LANES):
            lo = part * width + c * V7X_LANES
            t = qkv[:, lo:lo + V7X_LANES]
            ms = _group_sum(t * t, ones_ref) * (1.0 / C_QK_DIM)
            t = t * lax.rsqrt(ms + RMS_EPS) * gain
            partner = jnp.where(first_half, pltpu.roll(t, V7X_LANES - ROPE_HALF, axis=1),
                                pltpu.roll(t, ROPE_HALF, axis=1))
            t = t * cos + partner * sin
            if part == 0:
                out_ref[0, :, c * V7X_LANES:(c + 1) * V7X_LANES] = t.astype(BF16)
            else:
                out_ref[0, c * V7X_LANES:(c + 1) * V7X_LANES, :] = t.T.astype(BF16)
    v_ref[0] = qkv[:, 2 * width:].astype(BF16)


def _rope_tables(n_ctx, n_lat):
    n_rows = n_lat // GRID_W
    row_pos = jnp.repeat(jnp.arange(n_rows), GRID_W).astype(F32)
    col_pos = jnp.tile(jnp.arange(GRID_W), n_rows).astype(F32)
    inv_freq = ROPE_BASE ** (-jnp.arange(ROPE_HALF, dtype=F32) / ROPE_HALF)

    def one(pos):
        ang = pos[:, None] * inv_freq
        c, s = jnp.cos(ang), jnp.sin(ang)
        return jnp.concatenate([c, c], axis=-1), jnp.concatenate([-s, s], axis=-1)

    (cr, sr), (cc, sc) = one(row_pos), one(col_pos)
    cos = jnp.concatenate([cr, cc], axis=-1)
    sin = jnp.concatenate([sr, sc], axis=-1)
    cos = jnp.concatenate([jnp.ones((n_ctx, C_QK_DIM), F32), cos], axis=0)
    sin = jnp.concatenate([jnp.zeros((n_ctx, C_QK_DIM), F32), sin], axis=0)
    reps = V7X_LANES // C_QK_DIM
    return jnp.tile(cos, (1, reps)), jnp.tile(sin, (1, reps))


def qkv_project(xs, shift, scale, w_qkv, q_g, k_g, n_ctx):
    b, l, d = xs.shape
    width = w_qkv.shape[1] // 3
    tm = _token_tile(n_ctx, l - n_ctx, TOKEN_TILE)
    ctx_tiles = n_ctx // tm
    cos, sin = _rope_tables(n_ctx, l - n_ctx)
    gains = jnp.stack([jnp.tile(q_g, V7X_LANES // C_QK_DIM), jnp.tile(k_g, V7X_LANES // C_QK_DIM)])
    ones = _group_ones(V7X_LANES, C_QK_DIM)
    out = pl.BlockSpec((1, tm, width), lambda i, j: (i, j, 0))
    out_t = pl.BlockSpec((1, width, tm), lambda i, j: (i, 0, j))
    return pl.pallas_call(
        functools.partial(_qkv_body, width=width),
        grid=(b, l // tm),
        in_specs=[
            pl.BlockSpec((1, tm, d), lambda i, j: (i, j, 0)),
            _mod_spec(d, ctx_tiles),
            _mod_spec(d, ctx_tiles),
            pl.BlockSpec(w_qkv.shape, lambda i, j: (0, 0)),
            pl.BlockSpec(ones.shape, lambda i, j: (0, 0)),
            pl.BlockSpec(gains.shape, lambda i, j: (0, 0)),
            pl.BlockSpec((tm, V7X_LANES), lambda i, j: (j, 0)),
            pl.BlockSpec((tm, V7X_LANES), lambda i, j: (j, 0)),
        ],
        out_specs=[out, out_t, out],
        out_shape=[jax.ShapeDtypeStruct((b, l, width), BF16), jax.ShapeDtypeStruct((b, width, l), BF16),
                   jax.ShapeDtypeStruct((b, l, width), BF16)],
        compiler_params=_cparams(("parallel", "parallel")),
        name="qkv_project",
    )(xs, shift, scale, w_qkv.astype(BF16), ones, gains, cos, sin)


def kernel(x, c, ctx, c_ctx, w_mod, b_mod, even_w_in, even_mu, even_w_out, rwkv_w0, rwkv_w_up, rwkv_a0, rwkv_a_up, rwkv_g_up, rwkv_k_k, rwkv_k_a, rwkv_r_k, rwkv_ln_w, rwkv_ln_b, lru_conv_w, lru_conv_b, lru_wa, lru_ba, lru_wx, lru_bx, lru_lam, attn_w_qkv, attn_q_g, attn_k_g, attn_lam_q1, attn_lam_k1, attn_lam_q2, attn_lam_k2, attn_sub_g, attn_w_o, peer_w_q, peer_keys, peer_u, peer_v):
    bsz, n_lat, d = x.shape
    n_ctx = ctx.shape[1]
    depth = w_mod.shape[0]
    a_proj = even_mu.shape[-1]
    s_lat = jax.nn.silu(c)
    s_ctx = jax.nn.silu(c_ctx)
    xs = jnp.concatenate([ctx, x], axis=1)
    for layer in range(depth):
        i = layer // 2
        m_l = s_lat @ w_mod[layer] + b_mod[layer]
        m_c = s_ctx @ w_mod[layer] + b_mod[layer]
        mods = jnp.stack([jnp.broadcast_to(m_c, m_l.shape), m_l], axis=1).reshape(bsz, 2, N_MOD, 1, d)
        shift1, scale1, gate1, shift2, scale2, gate2 = (mods[:, :, n] for n in range(N_MOD))
        if layer % 2 == 0:
            xs = _even_mixer(xs, shift1, scale1, gate1, n_ctx, even_w_in[i], even_mu[i], even_w_out[i],
                             rwkv_w0[i], rwkv_w_up[i], rwkv_a0[i], rwkv_a_up[i], rwkv_g_up[i], rwkv_k_k[i],
                             rwkv_k_a[i], rwkv_r_k[i], rwkv_ln_w[i], rwkv_ln_b[i], lru_conv_w[i], lru_conv_b[i],
                             lru_wa[i], lru_ba[i], lru_wx[i], lru_bx[i], lru_lam[i])
        else:
            lam_init = 0.8 - 0.6 * math.exp(-0.3 * layer)
            q, k, v = qkv_project(xs, shift1, scale1, attn_w_qkv[i], attn_q_g[i], attn_k_g[i], n_ctx)
            lam = (jnp.exp(jnp.sum(attn_lam_q1[i] * attn_lam_k1[i]))
                   - jnp.exp(jnp.sum(attn_lam_q2[i] * attn_lam_k2[i])) + lam_init)
            o = diff_attention_core(q, k, v, lam.reshape(1), attn_sub_g[i].reshape(1, C_V_DIM), n_ctx, lam_init)
            xs = gated_out_proj(o, xs, gate1, attn_w_o[i], n_ctx)
        seg_ctx = n_ctx if layer < depth - 1 else 0
        if layer == depth - 1:
            xs = xs[:, n_ctx:]
        h, isel, jsel, gsel = peer_route(xs, shift2, scale2, peer_w_q[layer], peer_keys[layer], seg_ctx)
        xs = peer_experts(h, isel, jsel, gsel, peer_u[layer].astype(BF16).T, peer_v[layer].astype(BF16), xs,
                          gate2, seg_ctx)
    return xs
```
